```python
import jax
import jax.numpy as jnp
from jax import lax
import numpy as np

D_MODEL = 1024
BATCH = 4
SEQ = 4096
DEPTH = 2

CONV_DIM = 512
CONV_KERNEL = 3
GMLP_DIM = 512
GMLP_GROUPS = 8
GMLP_GROUP_DIM = GMLP_DIM // GMLP_GROUPS
GMLP_CHUNK = 128
MIX_IN_EVEN = 3 * CONV_DIM + 2 * GMLP_DIM
MIX_OUT_EVEN = CONV_DIM + GMLP_DIM
N_HEADS = 8
HEAD_DIM = 128
ROT_DIM = HEAD_DIM // 4
ROPE_THETA = 500000.0
MOBA_BLOCK = 256
MOBA_TOPK = 3
MOBA_Q_CHUNK = 64
D_FF_DENSE = 2816
N_EXPERTS = 8
TOP_K_EXPERTS = 2
D_FF_EXPERT = 3584
EPS = 1e-6
N_EVEN = (DEPTH + 1) // 2
N_ODD = DEPTH // 2

kernel_name = 'hybrid_conv_gmlp_moba_moe_block'


def rms_norm(x, g):
    xf = x.astype(jnp.float32)
    y = xf * lax.rsqrt(jnp.mean(jnp.square(xf), axis=-1, keepdims=True) + EPS)
    return (y * g.astype(jnp.float32)).astype(x.dtype)


def layer_norm(x, g, b):
    xf = x.astype(jnp.float32)
    mu = jnp.mean(xf, axis=-1, keepdims=True)
    var = jnp.mean(jnp.square(xf - mu), axis=-1, keepdims=True)
    y = (xf - mu) * lax.rsqrt(var + EPS) * g.astype(jnp.float32) + b.astype(jnp.float32)
    return y.astype(x.dtype)


def rope_partial(x):
    seq = x.shape[1]
    half = ROT_DIM // 2
    inv_freq = ROPE_THETA ** (-jnp.arange(half, dtype=jnp.float32) / half)
    ang = jnp.arange(seq, dtype=jnp.float32)[:, None] * inv_freq[None, :]
    cos = jnp.cos(ang)[:, None, :]
    sin = jnp.sin(ang)[:, None, :]
    xf = x.astype(jnp.float32)
    x1 = xf[..., :half]
    x2 = xf[..., half:ROT_DIM]
    out = jnp.concatenate([x1 * cos - x2 * sin, x2 * cos + x1 * sin, xf[..., ROT_DIM:]], axis=-1)
    return out.astype(x.dtype)


def conv_gmlp_mixer(h, w_in, conv_w, ln_g, ln_b, w_s, b_s, w_out):
    bsz, seq, _ = h.shape
    proj = h @ w_in
    a_h, a_c, a_b, g_u, g_v = jnp.split(
        proj, [CONV_DIM, 2 * CONV_DIM, 3 * CONV_DIM, 3 * CONV_DIM + GMLP_DIM], axis=-1)
    z = a_c * a_h
    z = lax.conv_general_dilated(
        z, conv_w[:, None, :], window_strides=(1,), padding=[(CONV_KERNEL - 1, 0)],
        dimension_numbers=('NWC', 'WIO', 'NWC'), feature_group_count=CONV_DIM)
    y_a = a_b * z
    v = layer_norm(g_v, ln_g, ln_b)
    n_chunks = seq // GMLP_CHUNK
    v = v.reshape(bsz, n_chunks, GMLP_CHUNK, GMLP_GROUPS, GMLP_GROUP_DIM)
    causal = jnp.tril(jnp.ones((GMLP_CHUNK, GMLP_CHUNK), dtype=bool))
    w_causal = jnp.where(causal, w_s, jnp.zeros((), w_s.dtype))
    mixed = jnp.einsum('gts,bcsgd->bctgd', w_causal, v) + jnp.transpose(b_s)[:, :, None]
    y_b = g_u * mixed.reshape(bsz, seq, GMLP_DIM)
    return jnp.concatenate([y_a, y_b], axis=-1) @ w_out


def moba_attention(q, k, v):
    bsz, n_heads, seq, hd = q.shape
    n_blocks = -(-seq // MOBA_BLOCK)
    pad = n_blocks * MOBA_BLOCK - seq
    widths = ((0, 0), (0, 0), (0, pad), (0, 0))
    kb = jnp.pad(k, widths).reshape(bsz, n_heads, n_blocks, MOBA_BLOCK, hd)
    vb = jnp.pad(v, widths).reshape(bsz, n_heads, n_blocks, MOBA_BLOCK, hd)
    k_mean = jnp.mean(kb.astype(jnp.float32), axis=3)
    gate = jnp.einsum('bhsd,bhnd->bhsn', q.astype(jnp.float32), k_mean)
    q_block = jnp.arange(seq) // MOBA_BLOCK
    past = jnp.arange(n_blocks)[None, :] < q_block[:, None]
    gate = jnp.where(past, gate, -jnp.inf)
    n_sel = min(MOBA_TOPK, n_blocks)
    _, sel_idx = lax.top_k(gate, n_sel)
    sel_valid = sel_idx < q_block[:, None]
    n_chunks = seq // MOBA_Q_CHUNK

    def to_chunks(a):
        a = a.reshape(bsz, n_heads, n_chunks, MOBA_Q_CHUNK, *a.shape[3:])
        return jnp.moveaxis(a, 2, 0)

    starts = jnp.arange(n_chunks) * MOBA_Q_CHUNK
    b_ix = jnp.arange(bsz)[:, None, None, None]
    h_ix = jnp.arange(n_heads)[None, :, None, None]
    scale = hd ** -0.5

    def attend_chunk(args):
        q_c, idx_c, valid_c, start = args
        blk = start // MOBA_BLOCK
        k_own = lax.dynamic_index_in_dim(kb, blk, axis=2, keepdims=False)
        v_own = lax.dynamic_index_in_dim(vb, blk, axis=2, keepdims=False)
        k_sel = kb[b_ix, h_ix, idx_c]
        v_sel = vb[b_ix, h_ix, idx_c]
        s_sel = jnp.einsum('bhqd,bhqnld->bhqnl', q_c, k_sel).reshape(
            bsz, n_heads, MOBA_Q_CHUNK, n_sel * MOBA_BLOCK)
        s_own = jnp.einsum('bhqd,bhld->bhql', q_c, k_own)
        q_pos = start + jnp.arange(MOBA_Q_CHUNK)
        k_pos = blk * MOBA_BLOCK + jnp.arange(MOBA_BLOCK)
        sel_mask = jnp.repeat(valid_c, MOBA_BLOCK, axis=-1)
        own_mask = k_pos[None, :] <= q_pos[:, None]
        s = jnp.concatenate([
            jnp.where(sel_mask, s_sel.astype(jnp.float32) * scale, -jnp.inf),
            jnp.where(own_mask, s_own.astype(jnp.float32) * scale, -jnp.inf)], axis=-1)
        p = jax.nn.softmax(s, axis=-1).astype(v.dtype)
        p_sel = p[..., :n_sel * MOBA_BLOCK].reshape(bsz, n_heads, MOBA_Q_CHUNK, n_sel, MOBA_BLOCK)
        p_own = p[..., n_sel * MOBA_BLOCK:]
        return (jnp.einsum('bhqnl,bhqnld->bhqd', p_sel, v_sel)
                + jnp.einsum('bhql,bhld->bhqd', p_own, v_own))

    out = lax.map(attend_chunk, (to_chunks(q), to_chunks(sel_idx), to_chunks(sel_valid), starts))
    return jnp.moveaxis(out, 0, 2).reshape(bsz, n_heads, seq, hd)


def moba_mixer(h, w_qkv, q_g, k_g, w_o):
    bsz, seq, _ = h.shape
    qkv = (h @ w_qkv).reshape(bsz, seq, 3, N_HEADS, HEAD_DIM)
    q = rope_partial(rms_norm(qkv[:, :, 0], q_g))
    k = rope_partial(rms_norm(qkv[:, :, 1], k_g))
    v = qkv[:, :, 2]
    o = moba_attention(q.transpose(0, 2, 1, 3), k.transpose(0, 2, 1, 3), v.transpose(0, 2, 1, 3))
    o = o.transpose(0, 2, 1, 3).reshape(bsz, seq, N_HEADS * HEAD_DIM)
    return o @ w_o


def swiglu(h, w_gate, w_up, w_down):
    return (jax.nn.silu(h @ w_gate) * (h @ w_up)) @ w_down


def moe_swiglu(h, w_router, w_gate, w_up, w_down):
    logits = (h @ w_router).astype(jnp.float32)
    top_v, top_i = lax.top_k(logits, TOP_K_EXPERTS)
    probs = jax.nn.softmax(top_v, axis=-1)
    gates = jnp.sum(jax.nn.one_hot(top_i, N_EXPERTS, dtype=jnp.float32) * probs[..., None], axis=-2)
    gates = gates.astype(h.dtype)
    y = jnp.zeros_like(h)
    for e in range(N_EXPERTS):
        y = y + gates[..., e:e + 1] * swiglu(h, w_gate[e], w_up[e], w_down[e])
    return y


def setup_inputs(seed: int = 0) -> dict:
    key = jax.random.key(seed)
    ks = jax.random.split(key, 24)

    def nrm(k, shape, scale):
        return jax.random.normal(k, shape, jnp.float32) * scale

    def gain(k, shape):
        return 1.0 + 0.02 * jax.random.normal(k, shape, jnp.float32)

    return {
        'x': nrm(ks[0], (BATCH, SEQ, D_MODEL), 1.0),
        'e_mix_norm': gain(ks[1], (N_EVEN, D_MODEL)),
        'e_w_in': nrm(ks[2], (N_EVEN, D_MODEL, MIX_IN_EVEN), D_MODEL ** -0.5),
        'e_conv_w': nrm(ks[3], (N_EVEN, CONV_KERNEL, CONV_DIM), CONV_KERNEL ** -0.5),
        'e_gmlp_ln_g': gain(ks[4], (N_EVEN, GMLP_DIM)),
        'e_gmlp_ln_b': nrm(ks[5], (N_EVEN, GMLP_DIM), 0.02),
        'e_w_spatial': nrm(ks[6], (N_EVEN, GMLP_GROUPS, GMLP_CHUNK, GMLP_CHUNK), GMLP_CHUNK ** -0.5),
        'e_b_spatial': gain(ks[7], (N_EVEN, GMLP_GROUPS, GMLP_CHUNK)),
        'e_w_out': nrm(ks[8], (N_EVEN, MIX_OUT_EVEN, D_MODEL), MIX_OUT_EVEN ** -0.5),
        'e_ffn_norm': gain(ks[9], (N_EVEN, D_MODEL)),
        'e_w_gate': nrm(ks[10], (N_EVEN, D_MODEL, D_FF_DENSE), D_MODEL ** -0.5),
        'e_w_up': nrm(ks[11], (N_EVEN, D_MODEL, D_FF_DENSE), D_MODEL ** -0.5),
        'e_w_down': nrm(ks[12], (N_EVEN, D_FF_DENSE, D_MODEL), D_FF_DENSE ** -0.5),
        'o_mix_norm': gain(ks[13], (N_ODD, D_MODEL)),
        'o_w_qkv': nrm(ks[14], (N_ODD, D_MODEL, 3 * N_HEADS * HEAD_DIM), D_MODEL ** -0.5),
        'o_q_norm': gain(ks[15], (N_ODD, HEAD_DIM)),
        'o_k_norm': gain(ks[16], (N_ODD, HEAD_DIM)),
        'o_w_o': nrm(ks[17], (N_ODD, N_HEADS * HEAD_DIM, D_MODEL), (N_HEADS * HEAD_DIM) ** -0.5),
        'o_ffn_norm': gain(ks[18], (N_ODD, D_MODEL)),
        'o_w_router': nrm(ks[19], (N_ODD, D_MODEL, N_EXPERTS), D_MODEL ** -0.5),
        'o_w_gate': nrm(ks[20], (N_ODD, N_EXPERTS, D_MODEL, D_FF_EXPERT), D_MODEL ** -0.5),
        'o_w_up': nrm(ks[21], (N_ODD, N_EXPERTS, D_MODEL, D_FF_EXPERT), D_MODEL ** -0.5),
        'o_w_down': nrm(ks[22], (N_ODD, N_EXPERTS, D_FF_EXPERT, D_MODEL), D_FF_EXPERT ** -0.5),
    }


def reference(x, e_mix_norm, e_w_in, e_conv_w, e_gmlp_ln_g, e_gmlp_ln_b, e_w_spatial,
              e_b_spatial, e_w_out, e_ffn_norm, e_w_gate, e_w_up, e_w_down,
              o_mix_norm, o_w_qkv, o_q_norm, o_k_norm, o_w_o, o_ffn_norm, o_w_router,
              o_w_gate, o_w_up, o_w_down):
    h = x
    for layer in range(DEPTH):
        i = layer // 2
        if layer % 2 == 0:
            h = h + conv_gmlp_mixer(rms_norm(h, e_mix_norm[i]), e_w_in[i], e_conv_w[i],
                                    e_gmlp_ln_g[i], e_gmlp_ln_b[i], e_w_spatial[i],
                                    e_b_spatial[i], e_w_out[i])
            h = h + swiglu(rms_norm(h, e_ffn_norm[i]), e_w_gate[i], e_w_up[i], e_w_down[i])
        else:
            h = h + moba_mixer(rms_norm(h, o_mix_norm[i]), o_w_qkv[i], o_q_norm[i],
                               o_k_norm[i], o_w_o[i])
            h = h + moe_swiglu(rms_norm(h, o_ffn_norm[i]), o_w_router[i], o_w_gate[i],
                               o_w_up[i], o_w_down[i])
    return h
```

```python
import functools

import jax
import jax.numpy as jnp
from jax import lax
from jax.experimental import pallas as pl
from jax.experimental.pallas import tpu as pltpu

F32 = jnp.float32
BF16 = jnp.bfloat16

D_MODEL = 1024
CONV_DIM = 512
CONV_KERNEL = 3
GMLP_DIM = 512
GMLP_GROUPS = 8
GMLP_GROUP_DIM = GMLP_DIM // GMLP_GROUPS
GMLP_CHUNK = 128
MIX_IN = 3 * CONV_DIM + 2 * GMLP_DIM
N_HEADS = 8
HEAD_DIM = 128
ROT_DIM = HEAD_DIM // 4
ROPE_THETA = 500000.0
MOBA_BLOCK = 256
MOBA_TOPK = 3
N_EXPERTS = 8
EPS = 1e-6

V7X_LANES = 128
VMEM_LIMIT = 56 * 1024 * 1024

NEG_INF = float("-inf")


def _rms(x, g):
    return x * lax.rsqrt(jnp.mean(x * x, axis=-1, keepdims=True) + EPS) * g


def _dot(a, b):
    return jnp.dot(a, b, preferred_element_type=F32)


def _dot_nt(a, b):
    return lax.dot_general(a, b, (((1,), (1,)), ((), ())), preferred_element_type=F32)


def _split_bf16(x):
    hi = x.astype(BF16)
    lo = (x - hi.astype(F32)).astype(BF16)
    return hi, lo


def _mixer0_kernel(x_ref, g_ref, win_ref, convw_ref, lng_ref, lnb_ref, ws_ref, bias_ref,
                   wout_ref, g2_ref, h_ref, hn_ref, carry_ref, y_ref, *, tm, tiles_per_seq):
    i = pl.program_id(0)
    x = x_ref[...]
    hn = _rms(x, g_ref[...]).astype(BF16)
    proj = _dot(hn, win_ref[...])
    a_h = proj[:, 0:CONV_DIM]
    a_c = proj[:, CONV_DIM:2 * CONV_DIM]
    a_b = proj[:, 2 * CONV_DIM:3 * CONV_DIM]
    g_u = proj[:, 3 * CONV_DIM:3 * CONV_DIM + GMLP_DIM]
    g_v = proj[:, 3 * CONV_DIM + GMLP_DIM:]

    z = a_c * a_h

    @pl.when(i % tiles_per_seq == 0)
    def _():
        carry_ref[...] = jnp.zeros_like(carry_ref)

    prev = carry_ref[...]
    row = lax.broadcasted_iota(jnp.int32, (tm, CONV_DIM), 0)
    z1 = jnp.where(row == 0, prev[7:8, :], pltpu.roll(z, 1, 0))
    z2 = jnp.where(row == 0, prev[6:7, :], jnp.where(row == 1, prev[7:8, :], pltpu.roll(z, 2, 0)))
    carry_ref[...] = z[tm - 8:tm, :]
    cw = convw_ref[...]
    y_a = a_b * (cw[0:1, :] * z2 + cw[1:2, :] * z1 + cw[2:3, :] * z)
    y_ref[:, 0:CONV_DIM] = y_a.astype(BF16)

    mu = jnp.mean(g_v, axis=-1, keepdims=True)
    cen = g_v - mu
    var = jnp.mean(cen * cen, axis=-1, keepdims=True)
    v = cen * lax.rsqrt(var + EPS) * lng_ref[...] + lnb_ref[...]
    tr = lax.broadcasted_iota(jnp.int32, (GMLP_CHUNK, GMLP_CHUNK), 0)
    tc = lax.broadcasted_iota(jnp.int32, (GMLP_CHUNK, GMLP_CHUNK), 1)
    causal = tc <= tr
    lo_half = tc < GMLP_GROUP_DIM
    bias = bias_ref[...]
    for c in range(tm // GMLP_CHUNK):
        rows = slice(c * GMLP_CHUNK, (c + 1) * GMLP_CHUNK)
        pieces = []
        for j in range(GMLP_DIM // V7X_LANES):
            vj = v[rows, j * V7X_LANES:(j + 1) * V7X_LANES]
            v_lo = jnp.where(lo_half, vj, 0.0).astype(BF16)
            v_hi = jnp.where(lo_half, 0.0, vj).astype(BF16)
            w_lo = jnp.where(causal, ws_ref[2 * j], 0.0).astype(BF16)
            w_hi = jnp.where(causal, ws_ref[2 * j + 1], 0.0).astype(BF16)
            pieces.append(_dot(w_lo, v_lo) + _dot(w_hi, v_hi))
        mixed = jnp.concatenate(pieces, axis=1) + bias
        y_ref[rows, CONV_DIM:] = (g_u[rows, :] * mixed).astype(BF16)

    h = x + _dot(y_ref[...], wout_ref[...])
    h_ref[...] = h
    hn_ref[...] = _rms(h, g2_ref[...]).astype(BF16)


def _mixer0(x, g, w_in, conv_w, ln_g, ln_b, w_s, bias_full, w_out, g2, *, seq, tm=512):
    t, d = x.shape
    const = lambda *shape: pl.BlockSpec(shape, lambda i: (0,) * len(shape))
    return pl.pallas_call(
        functools.partial(_mixer0_kernel, tm=tm, tiles_per_seq=seq // tm),
        grid=(t // tm,),
        in_specs=[
            pl.BlockSpec((tm, d), lambda i: (i, 0)),
            const(1, d),
            const(d, MIX_IN),
            const(CONV_KERNEL, CONV_DIM),
            const(1, GMLP_DIM),
            const(1, GMLP_DIM),
            const(GMLP_GROUPS, GMLP_CHUNK, GMLP_CHUNK),
            const(GMLP_CHUNK, GMLP_DIM),
            const(CONV_DIM + GMLP_DIM, d),
            const(1, d),
        ],
        out_specs=[pl.BlockSpec((tm, d), lambda i: (i, 0)), pl.BlockSpec((tm, d), lambda i: (i, 0))],
        out_shape=[jax.ShapeDtypeStruct((t, d), F32), jax.ShapeDtypeStruct((t, d), BF16)],
        scratch_shapes=[pltpu.VMEM((8, CONV_DIM), F32), pltpu.VMEM((tm, CONV_DIM + GMLP_DIM), BF16)],
        compiler_params=pltpu.CompilerParams(dimension_semantics=("arbitrary",),
                                             vmem_limit_bytes=VMEM_LIMIT),
        name="mixer0",
    )(x, g, w_in, conv_w, ln_g, ln_b, w_s, bias_full, w_out, g2)


def _ffn_kernel(*refs, use_gates):
    if use_gates:
        x_ref, res_ref, gate_ref, wg_ref, wu_ref, wd_ref, o_ref = refs
    else:
        x_ref, res_ref, wg_ref, wu_ref, wd_ref, o_ref = refs
    e = pl.program_id(1)
    f = pl.program_id(2)

    @pl.when((e == 0) & (f == 0))
    def _():
        o_ref[...] = res_ref[...]

    x = x_ref[...]
    a = _dot(x, wg_ref[0].astype(BF16))
    u = _dot(x, wu_ref[0].astype(BF16))
    hact = (a * jax.nn.sigmoid(a) * u).astype(BF16)
    y = _dot(hact, wd_ref[0].astype(BF16))
    if use_gates:
        y = y * gate_ref[0]
    o_ref[...] += y


def _ffn(xn, res, gates, w_gate, w_up, w_down, *, tm, tf):
    t, d = xn.shape
    n_e, _, f_dim = w_gate.shape
    use_gates = gates is not None
    in_specs = [pl.BlockSpec((tm, d), lambda i, e, f: (i, 0)),
                pl.BlockSpec((tm, d), lambda i, e, f: (i, 0))]
    args = [xn, res]
    if use_gates:
        in_specs.append(pl.BlockSpec((1, tm, 1), lambda i, e, f: (e, i, 0)))
        args.append(gates)
    in_specs += [
        pl.BlockSpec((1, d, tf), lambda i, e, f: (e, 0, f)),
        pl.BlockSpec((1, d, tf), lambda i, e, f: (e, 0, f)),
        pl.BlockSpec((1, tf, d), lambda i, e, f: (e, f, 0)),
    ]
    args += [w_gate, w_up, w_down]
    return pl.pallas_call(
        functools.partial(_ffn_kernel, use_gates=use_gates),
        grid=(t // tm, n_e, f_dim // tf),
        in_specs=in_specs,
        out_specs=pl.BlockSpec((tm, d), lambda i, e, f: (i, 0)),
        out_shape=jax.ShapeDtypeStruct((t, d), F32),
        compiler_params=pltpu.CompilerParams(
            dimension_semantics=("parallel", "arbitrary", "arbitrary"),
            vmem_limit_bytes=VMEM_LIMIT),
        name="moe_ffn" if use_gates else "dense_ffn",
    )(*args)


def _qkv_kernel(h_ref, g_ref, w_ref, qg_ref, kg_ref, cos_ref, sin_ref,
                q_ref, k_ref, v_ref, km_ref, *, tm):
    hn = _rms(h_ref[...], g_ref[...]).astype(BF16)
    qkv = _dot(hn, w_ref[...])
    cos = cos_ref[...]
    sin = sin_ref[...]
    lane = lax.broadcasted_iota(jnp.int32, (tm, HEAD_DIM), 1)
    first = lane < ROT_DIM // 2

    def rope(x):
        partner = jnp.where(first, pltpu.roll(x, HEAD_DIM - ROT_DIM // 2, 1),
                            pltpu.roll(x, ROT_DIM // 2, 1))
        return x * cos + partner * sin

    hd = N_HEADS * HEAD_DIM
    for h in range(N_HEADS):
        cols = slice(h * HEAD_DIM, (h + 1) * HEAD_DIM)
        q = rope(_rms(qkv[:, cols], qg_ref[...]))
        k = rope(_rms(qkv[:, hd + h * HEAD_DIM:hd + (h + 1) * HEAD_DIM], kg_ref[...]))
        v = qkv[:, 2 * hd + h * HEAD_DIM:2 * hd + (h + 1) * HEAD_DIM]
        q_ref[0, h] = q.astype(BF16)
        k_ref[0, h] = k.astype(BF16)
        v_ref[0, h] = v.astype(BF16)
        for c in range(tm // MOBA_BLOCK):
            km_ref[0, c, h:h + 1, :] = jnp.mean(
                k[c * MOBA_BLOCK:(c + 1) * MOBA_BLOCK, :], axis=0, keepdims=True)


def _qkv(h, g, w, qg, kg, cos, sin, *, batch, seq, tm=512):
    t, d = h.shape
    ns = seq // tm
    nb = seq // MOBA_BLOCK
    const = lambda *shape: pl.BlockSpec(shape, lambda b, s: (0,) * len(shape))
    head_spec = pl.BlockSpec((1, N_HEADS, tm, HEAD_DIM), lambda b, s: (b, 0, s, 0))
    head_shape = jax.ShapeDtypeStruct((batch, N_HEADS, seq, HEAD_DIM), BF16)
    return pl.pallas_call(
        functools.partial(_qkv_kernel, tm=tm),
        grid=(batch, ns),
        in_specs=[
            pl.BlockSpec((tm, d), lambda b, s: (b * ns + s, 0)),
            const(1, d),
            const(d, 3 * N_HEADS * HEAD_DIM),
            const(1, HEAD_DIM),
            const(1, HEAD_DIM),
            pl.BlockSpec((tm, HEAD_DIM), lambda b, s: (s, 0)),
            pl.BlockSpec((tm, HEAD_DIM), lambda b, s: (s, 0)),
        ],
        out_specs=[head_spec, head_spec, head_spec,
                   pl.BlockSpec((1, tm // MOBA_BLOCK, N_HEADS, HEAD_DIM), lambda b, s: (b, s, 0, 0))],
        out_shape=[head_shape, head_shape, head_shape,
                   jax.ShapeDtypeStruct((batch, nb, N_HEADS, HEAD_DIM), F32)],
        compiler_params=pltpu.CompilerParams(dimension_semantics=("parallel", "parallel"),
                                             vmem_limit_bytes=VMEM_LIMIT),
        name="qkv",
    )(h, g, w, qg, kg, cos, sin)


def _attn_kernel(q_ref, k_ref, v_ref, km_ref, o_ref, *, n_blocks):
    i = pl.program_id(2)
    bs = MOBA_BLOCK
    scale = HEAD_DIM ** -0.5
    q = q_ref[0, 0]

    km_hi, km_lo = _split_bf16(km_ref[0, 0])
    gate = _dot_nt(q, km_hi) + _dot_nt(q, km_lo)
    blk = lax.broadcasted_iota(jnp.int32, (bs, n_blocks), 1)
    past = blk < i
    cur = jnp.where(past, gate, NEG_INF)
    sel = jnp.zeros((bs, n_blocks), F32)
    for _ in range(min(MOBA_TOPK, n_blocks)):
        top = jnp.max(cur, axis=-1, keepdims=True)
        idx = jnp.min(jnp.where(cur == top, blk, n_blocks), axis=-1, keepdims=True)
        pick = blk == idx
        sel = jnp.where(pick & past, 1.0, sel)
        cur = jnp.where(pick, NEG_INF, cur)

    def block(j):
        start = pl.multiple_of(j * bs, bs)
        kj = k_ref[0, 0, pl.ds(start, bs), :]
        vj = v_ref[0, 0, pl.ds(start, bs), :]
        return _dot_nt(q, kj) * scale, vj

    s, vj = block(i)
    r = lax.broadcasted_iota(jnp.int32, (bs, bs), 0)
    c = lax.broadcasted_iota(jnp.int32, (bs, bs), 1)
    s = jnp.where(c <= r, s, NEG_INF)
    m = jnp.max(s, axis=-1, keepdims=True)
    p = jnp.exp(s - m)
    l = jnp.sum(p, axis=-1, keepdims=True)
    acc = _dot(p.astype(BF16), vj)

    def body(j, carry):
        m, l, acc = carry
        s, vj = block(j)
        chosen = jnp.sum(jnp.where(blk == j, sel, 0.0), axis=-1, keepdims=True) > 0.5
        s = jnp.where(chosen, s, NEG_INF)
        m_new = jnp.maximum(m, jnp.max(s, axis=-1, keepdims=True))
        alpha = jnp.exp(m - m_new)
        p = jnp.exp(s - m_new)
        l = alpha * l + jnp.sum(p, axis=-1, keepdims=True)
        acc = alpha * acc + _dot(p.astype(BF16), vj)
        return m_new, l, acc

    m, l, acc = lax.fori_loop(0, i, body, (m, l, acc))
    o_ref[0] = (acc / l).astype(BF16)


def _attn(q, k, v, km):
    batch, n_heads, seq, hd = q.shape
    nb = seq // MOBA_BLOCK
    return pl.pallas_call(
        functools.partial(_attn_kernel, n_blocks=nb),
        grid=(batch, n_heads, nb),
        in_specs=[
            pl.BlockSpec((1, 1, MOBA_BLOCK, hd), lambda b, h, i: (b, h, i, 0)),
            pl.BlockSpec((1, 1, seq, hd), lambda b, h, i: (b, h, 0, 0)),
            pl.BlockSpec((1, 1, seq, hd), lambda b, h, i: (b, h, 0, 0)),
            pl.BlockSpec((1, 1, nb, hd), lambda b, h, i: (b, h, 0, 0)),
        ],
        out_specs=pl.BlockSpec((1, MOBA_BLOCK, hd), lambda b, h, i: (b, i, h)),
        out_shape=jax.ShapeDtypeStruct((batch, seq, n_heads * hd), BF16),
        compiler_params=pltpu.CompilerParams(
            dimension_semantics=("parallel", "parallel", "arbitrary"),
            vmem_limit_bytes=VMEM_LIMIT),
        name="moba_attn",
    )(q, k, v, km)


def _wo_router_kernel(o_ref, res_ref, wo_ref, g_ref, wr_ref, h_ref, hn_ref, gates_ref, *, tm):
    h = res_ref[...] + _dot(o_ref[...], wo_ref[...])
    h_ref[...] = h
    hn = _rms(h, g_ref[...])
    hn_ref[...] = hn.astype(BF16)

    x_hi, x_lo = _split_bf16(hn)
    w_hi, w_lo = _split_bf16(wr_ref[...])
    logits = _dot(x_hi, w_hi) + _dot(x_hi, w_lo) + _dot(x_lo, w_hi)
    lane = lax.broadcasted_iota(jnp.int32, (tm, N_EXPERTS), 1)
    m1 = jnp.max(logits, axis=-1, keepdims=True)
    i1 = jnp.min(jnp.where(logits == m1, lane, N_EXPERTS), axis=-1, keepdims=True)
    pick1 = lane == i1
    rest = jnp.where(pick1, NEG_INF, logits)
    m2 = jnp.max(rest, axis=-1, keepdims=True)
    i2 = jnp.min(jnp.where(rest == m2, lane, N_EXPERTS), axis=-1, keepdims=True)
    pick2 = lane == i2
    e2 = jnp.exp(m2 - m1)
    denom = 1.0 + e2
    gates_ref[...] = jnp.where(pick1, 1.0 / denom, 0.0) + jnp.where(pick2, e2 / denom, 0.0)


def _wo_router(o, res, w_o, g, w_r, *, tm=512):
    t, d = res.shape
    const = lambda *shape: pl.BlockSpec(shape, lambda i: (0,) * len(shape))
    row = lambda width: pl.BlockSpec((tm, width), lambda i: (i, 0))
    return pl.pallas_call(
        functools.partial(_wo_router_kernel, tm=tm),
        grid=(t // tm,),
        in_specs=[row(d), row(d), const(d, d), const(1, d), const(d, N_EXPERTS)],
        out_specs=[row(d), row(d), row(N_EXPERTS)],
        out_shape=[jax.ShapeDtypeStruct((t, d), F32), jax.ShapeDtypeStruct((t, d), BF16),
                   jax.ShapeDtypeStruct((t, N_EXPERTS), F32)],
        compiler_params=pltpu.CompilerParams(dimension_semantics=("parallel",),
                                             vmem_limit_bytes=VMEM_LIMIT),
        name="wo_router",
    )(o, res, w_o, g, w_r)


def _rope_tables(seq):
    half = ROT_DIM // 2
    inv_freq = ROPE_THETA ** (-jnp.arange(half, dtype=F32) / half)
    ang = jnp.arange(seq, dtype=F32)[:, None] * inv_freq[None, :]
    cos = jnp.cos(ang)
    sin = jnp.sin(ang)
    rest = HEAD_DIM - ROT_DIM
    cos_full = jnp.concatenate([cos, cos, jnp.ones((seq, rest), F32)], axis=-1)
    sin_signed = jnp.concatenate([-sin, sin, jnp.zeros((seq, rest), F32)], axis=-1)
    return cos_full, sin_signed


def kernel(x, e_mix_norm, e_w_in, e_conv_w, e_gmlp_ln_g, e_gmlp_ln_b, e_w_spatial, e_b_spatial, e_w_out, e_ffn_norm, e_w_gate, e_w_up, e_w_down, o_mix_norm, o_w_qkv, o_q_norm, o_k_norm, o_w_o, o_ffn_norm, o_w_router, o_w_gate, o_w_up, o_w_down):
    batch, seq, d = x.shape
    t = batch * seq
    xf = x.reshape(t, d)

    bias_full = jnp.repeat(jnp.transpose(e_b_spatial[0]), GMLP_GROUP_DIM, axis=1)
    h1, hn1 = _mixer0(
        xf, e_mix_norm[0][None, :], e_w_in[0].astype(BF16), e_conv_w[0],
        e_gmlp_ln_g[0][None, :], e_gmlp_ln_b[0][None, :], e_w_spatial[0], bias_full,
        e_w_out[0].astype(BF16), e_ffn_norm[0][None, :], seq=seq)
    h2 = _ffn(hn1, h1, None, e_w_gate.astype(BF16), e_w_up.astype(BF16), e_w_down.astype(BF16),
              tm=512, tf=1408)

    cos, sin = _rope_tables(seq)
    q, k, v, km = _qkv(h2, o_mix_norm[0][None, :], o_w_qkv[0].astype(BF16),
                       o_q_norm[0][None, :], o_k_norm[0][None, :], cos, sin, batch=batch, seq=seq)
    o = _attn(q, k, v, jnp.transpose(km, (0, 2, 1, 3)))
    h3, hn3, gates = _wo_router(o.reshape(t, d), h2, o_w_o[0].astype(BF16),
                                o_ffn_norm[0][None, :], o_w_router[0])
    gates_t = jnp.transpose(gates)[:, :, None]
    out = _ffn(hn3, h3, gates_t, o_w_gate[0], o_w_up[0], o_w_down[0], tm=1024, tf=512)
    return out.reshape(batch, seq, d)
```

```python
import functools
import math

import jax
import jax.numpy as jnp
from jax import lax
from jax.experimental import pallas as pl
from jax.experimental.pallas import tpu as pltpu

F32 = jnp.float32
BF16 = jnp.bfloat16

D_MODEL = 1024
CONV_DIM = 512
CONV_KERNEL = 3
GMLP_DIM = 512
GMLP_GROUPS = 8
GMLP_GROUP_DIM = GMLP_DIM // GMLP_GROUPS
GMLP_CHUNK = 128
MIX_IN = 3 * CONV_DIM + 2 * GMLP_DIM
N_HEADS = 8
HEAD_DIM = 128
ROT_DIM = HEAD_DIM // 4
ROPE_THETA = 500000.0
MOBA_BLOCK = 256
MOBA_TOPK = 3
N_EXPERTS = 8
EPS = 1e-6

V7X_LANES = 128
VMEM_LIMIT = 56 * 1024 * 1024

NEG_INF = float("-inf")
QK_SCALE = HEAD_DIM ** -0.5 * math.log2(math.e)
MASK_MARGIN = 256.0
ONES_ROWS = 16


def _rms(x, g):
    return x * lax.rsqrt(jnp.mean(x * x, axis=-1, keepdims=True) + EPS) * g


def _dot(a, b):
    return jnp.dot(a, b, preferred_element_type=F32)


def _dot_nt(a, b):
    return lax.dot_general(a, b, (((1,), (1,)), ((), ())), preferred_element_type=F32)


def _split_bf16(x):
    hi = x.astype(BF16)
    lo = (x - hi.astype(F32)).astype(BF16)
    return hi, lo


def _mixer0_kernel(x_ref, g_ref, win_ref, convw_ref, lng_ref, lnb_ref, ws_ref, bias_ref,
                   wout_ref, g2_ref, h_ref, hn_ref, carry_ref, y_ref, *, tm, tiles_per_seq):
    i = pl.program_id(0)
    x = x_ref[...]
    hn = _rms(x, g_ref[...]).astype(BF16)
    proj = _dot(hn, win_ref[...])
    a_h = proj[:, 0:CONV_DIM]
    a_c = proj[:, CONV_DIM:2 * CONV_DIM]
    a_b = proj[:, 2 * CONV_DIM:3 * CONV_DIM]
    g_u = proj[:, 3 * CONV_DIM:3 * CONV_DIM + GMLP_DIM]
    g_v = proj[:, 3 * CONV_DIM + GMLP_DIM:]

    z = a_c * a_h

    @pl.when(i % tiles_per_seq == 0)
    def _():
        carry_ref[...] = jnp.zeros_like(carry_ref)

    prev = carry_ref[...]
    row = lax.broadcasted_iota(jnp.int32, (tm, CONV_DIM), 0)
    z1 = jnp.where(row == 0, prev[7:8, :], pltpu.roll(z, 1, 0))
    z2 = jnp.where(row == 0, prev[6:7, :], jnp.where(row == 1, prev[7:8, :], pltpu.roll(z, 2, 0)))
    carry_ref[...] = z[tm - 8:tm, :]
    cw = convw_ref[...]
    y_a = a_b * (cw[0:1, :] * z2 + cw[1:2, :] * z1 + cw[2:3, :] * z)
    y_ref[:, 0:CONV_DIM] = y_a.astype(BF16)

    mu = jnp.mean(g_v, axis=-1, keepdims=True)
    cen = g_v - mu
    var = jnp.mean(cen * cen, axis=-1, keepdims=True)
    v = cen * lax.rsqrt(var + EPS) * lng_ref[...] + lnb_ref[...]
    tr = lax.broadcasted_iota(jnp.int32, (GMLP_CHUNK, GMLP_CHUNK), 0)
    tc = lax.broadcasted_iota(jnp.int32, (GMLP_CHUNK, GMLP_CHUNK), 1)
    causal = tc <= tr
    lo_half = tc < GMLP_GROUP_DIM
    bias = bias_ref[...]
    for c in range(tm // GMLP_CHUNK):
        rows = slice(c * GMLP_CHUNK, (c + 1) * GMLP_CHUNK)
        pieces = []
        for j in range(GMLP_DIM // V7X_LANES):
            vj = v[rows, j * V7X_LANES:(j + 1) * V7X_LANES]
            v_lo = jnp.where(lo_half, vj, 0.0).astype(BF16)
            v_hi = jnp.where(lo_half, 0.0, vj).astype(BF16)
            w_lo = jnp.where(causal, ws_ref[2 * j], 0.0).astype(BF16)
            w_hi = jnp.where(causal, ws_ref[2 * j + 1], 0.0).astype(BF16)
            pieces.append(_dot(w_lo, v_lo) + _dot(w_hi, v_hi))
        mixed = jnp.concatenate(pieces, axis=1) + bias
        y_ref[rows, CONV_DIM:] = (g_u[rows, :] * mixed).astype(BF16)

    h = x + _dot(y_ref[...], wout_ref[...])
    h_ref[...] = h
    hn_ref[...] = _rms(h, g2_ref[...]).astype(BF16)


def _mixer0(x, g, w_in, conv_w, ln_g, ln_b, w_s, bias_full, w_out, g2, *, seq, tm=512):
    t, d = x.shape
    const = lambda *shape: pl.BlockSpec(shape, lambda i: (0,) * len(shape))
    return pl.pallas_call(
        functools.partial(_mixer0_kernel, tm=tm, tiles_per_seq=seq // tm),
        grid=(t // tm,),
        in_specs=[
            pl.BlockSpec((tm, d), lambda i: (i, 0)),
            const(1, d),
            const(d, MIX_IN),
            const(CONV_KERNEL, CONV_DIM),
            const(1, GMLP_DIM),
            const(1, GMLP_DIM),
            const(GMLP_GROUPS, GMLP_CHUNK, GMLP_CHUNK),
            const(GMLP_CHUNK, GMLP_DIM),
            const(CONV_DIM + GMLP_DIM, d),
            const(1, d),
        ],
        out_specs=[pl.BlockSpec((tm, d), lambda i: (i, 0)), pl.BlockSpec((tm, d), lambda i: (i, 0))],
        out_shape=[jax.ShapeDtypeStruct((t, d), F32), jax.ShapeDtypeStruct((t, d), BF16)],
        scratch_shapes=[pltpu.VMEM((8, CONV_DIM), F32), pltpu.VMEM((tm, CONV_DIM + GMLP_DIM), BF16)],
        compiler_params=pltpu.CompilerParams(dimension_semantics=("arbitrary",),
                                             vmem_limit_bytes=VMEM_LIMIT),
        name="mixer0",
    )(x, g, w_in, conv_w, ln_g, ln_b, w_s, bias_full, w_out, g2)


def _ffn_kernel(*refs, use_gates):
    if use_gates:
        x_ref, res_ref, gate_ref, wg_ref, wu_ref, wd_ref, o_ref = refs
    else:
        x_ref, res_ref, wg_ref, wu_ref, wd_ref, o_ref = refs
    e = pl.program_id(1)
    f = pl.program_id(2)

    @pl.when((e == 0) & (f == 0))
    def _():
        o_ref[...] = res_ref[...]

    x = x_ref[...]
    a = _dot(x, wg_ref[0].astype(BF16))
    u = _dot(x, wu_ref[0].astype(BF16))
    hact = (a * jax.nn.sigmoid(a) * u).astype(BF16)
    y = _dot(hact, wd_ref[0].astype(BF16))
    if use_gates:
        y = y * gate_ref[0]
    o_ref[...] += y


def _ffn(xn, res, gates, w_gate, w_up, w_down, *, tm, tf):
    t, d = xn.shape
    n_e, _, f_dim = w_gate.shape
    use_gates = gates is not None
    in_specs = [pl.BlockSpec((tm, d), lambda i, e, f: (i, 0)),
                pl.BlockSpec((tm, d), lambda i, e, f: (i, 0))]
    args = [xn, res]
    if use_gates:
        in_specs.append(pl.BlockSpec((1, tm, 1), lambda i, e, f: (e, i, 0)))
        args.append(gates)
    in_specs += [
        pl.BlockSpec((1, d, tf), lambda i, e, f: (e, 0, f)),
        pl.BlockSpec((1, d, tf), lambda i, e, f: (e, 0, f)),
        pl.BlockSpec((1, tf, d), lambda i, e, f: (e, f, 0)),
    ]
    args += [w_gate, w_up, w_down]
    return pl.pallas_call(
        functools.partial(_ffn_kernel, use_gates=use_gates),
        grid=(t // tm, n_e, f_dim // tf),
        in_specs=in_specs,
        out_specs=pl.BlockSpec((tm, d), lambda i, e, f: (i, 0)),
        out_shape=jax.ShapeDtypeStruct((t, d), F32),
        compiler_params=pltpu.CompilerParams(
            dimension_semantics=("parallel", "arbitrary", "arbitrary"),
            vmem_limit_bytes=VMEM_LIMIT),
        name="moe_ffn" if use_gates else "dense_ffn",
    )(*args)


def _qkv_kernel(h_ref, g_ref, w_ref, qg_ref, kg_ref, cos_ref, sin_ref,
                q_ref, k_ref, v_ref, km_ref, *, tm):
    hn = _rms(h_ref[...], g_ref[...]).astype(BF16)
    qkv = _dot(hn, w_ref[...])
    cos = cos_ref[...]
    sin = sin_ref[...]
    lane = lax.broadcasted_iota(jnp.int32, (tm, HEAD_DIM), 1)
    first = lane < ROT_DIM // 2

    def rope(x):
        partner = jnp.where(first, pltpu.roll(x, HEAD_DIM - ROT_DIM // 2, 1),
                            pltpu.roll(x, ROT_DIM // 2, 1))
        return x * cos + partner * sin

    hd = N_HEADS * HEAD_DIM
    for c in range(tm // MOBA_BLOCK):
        v_ref[c] = jnp.transpose(qkv[c * MOBA_BLOCK:(c + 1) * MOBA_BLOCK, 2 * hd:]).astype(BF16)
    for h in range(N_HEADS):
        cols = slice(h * HEAD_DIM, (h + 1) * HEAD_DIM)
        q = rope(_rms(qkv[:, cols], qg_ref[...]))
        k = rope(_rms(qkv[:, hd + h * HEAD_DIM:hd + (h + 1) * HEAD_DIM], kg_ref[...]))
        q_ref[:, cols] = (q * QK_SCALE).astype(BF16)
        k_ref[:, cols] = k.astype(BF16)
        for c in range(tm // MOBA_BLOCK):
            km_ref[c, :, cols] = jnp.mean(
                k[c * MOBA_BLOCK:(c + 1) * MOBA_BLOCK, :], axis=0, keepdims=True)


def _qkv(h, g, w, qg, kg, cos, sin, *, seq, tm=512):
    t, d = h.shape
    ns = seq // tm
    hd = N_HEADS * HEAD_DIM
    const = lambda *shape: pl.BlockSpec(shape, lambda i: (0,) * len(shape))
    row_spec = pl.BlockSpec((tm, hd), lambda i: (i, 0))
    row_shape = jax.ShapeDtypeStruct((t, hd), BF16)
    return pl.pallas_call(
        functools.partial(_qkv_kernel, tm=tm),
        grid=(t // tm,),
        in_specs=[
            pl.BlockSpec((tm, d), lambda i: (i, 0)),
            const(1, d),
            const(d, 3 * hd),
            const(1, HEAD_DIM),
            const(1, HEAD_DIM),
            pl.BlockSpec((tm, HEAD_DIM), lambda i: (i % ns, 0)),
            pl.BlockSpec((tm, HEAD_DIM), lambda i: (i % ns, 0)),
        ],
        out_specs=[row_spec, row_spec,
                   pl.BlockSpec((tm // MOBA_BLOCK, hd, MOBA_BLOCK), lambda i: (i, 0, 0)),
                   pl.BlockSpec((tm // MOBA_BLOCK, 1, hd), lambda i: (i, 0, 0))],
        out_shape=[row_shape, row_shape,
                   jax.ShapeDtypeStruct((t // MOBA_BLOCK, hd, MOBA_BLOCK), BF16),
                   jax.ShapeDtypeStruct((t // MOBA_BLOCK, 1, hd), F32)],
        compiler_params=pltpu.CompilerParams(dimension_semantics=("parallel",),
                                             vmem_limit_bytes=VMEM_LIMIT),
        name="qkv",
    )(h, g, w, qg, kg, cos, sin)


def _attn_kernel(q_ref, k_ref, v_ref, km_ref, qg_ref, kg_ref, o_ref, qx_ref, acc_ref, m_ref,
                 *, n_blocks):
    i = pl.program_id(1)
    bs = MOBA_BLOCK
    hd = HEAD_DIM
    bound = QK_SCALE * hd * jnp.max(jnp.abs(qg_ref[...]), axis=-1, keepdims=True) * jnp.max(
        jnp.abs(kg_ref[...]), axis=-1, keepdims=True)
    big = 2.1 * bound + MASK_MARGIN
    blk = lax.broadcasted_iota(jnp.int32, (n_blocks, bs), 0)
    past = blk < i
    key = lax.broadcasted_iota(jnp.int32, (bs, bs), 0)
    qry = lax.broadcasted_iota(jnp.int32, (bs, bs), 1)
    causal = key <= qry
    ones_rows = jnp.ones((ONES_ROWS, bs), BF16)
    own = pl.multiple_of(i * bs, bs)

    for h in range(N_HEADS):
        cols = slice(h * hd, (h + 1) * hd)
        q = q_ref[0, :, cols]
        km_hi, km_lo = _split_bf16(km_ref[0, :, cols])
        gate_t = _dot_nt(km_hi, q) + _dot_nt(km_lo, q)
        cur = jnp.where(past, gate_t, NEG_INF)
        sel = jnp.zeros((n_blocks, bs), F32)
        for _ in range(min(MOBA_TOPK, n_blocks)):
            top = jnp.max(cur, axis=0, keepdims=True)
            idx = jnp.min(jnp.where(cur == top, blk, n_blocks), axis=0, keepdims=True)
            pick = blk == idx
            sel = jnp.where(pick & past, 1.0, sel)
            cur = jnp.where(pick, NEG_INF, cur)
        bias_t = jnp.where(sel > 0.5, 0.0, -big)
        bias_t = jnp.concatenate([bias_t, jnp.zeros((hd - n_blocks, bs), F32)], axis=0)
        qx_ref[h] = jnp.concatenate([q, jnp.transpose(bias_t).astype(BF16)], axis=1)

        s = _dot_nt(k_ref[0, pl.ds(own, bs), cols], q)
        s = jnp.where(causal, s, NEG_INF)
        m = jnp.max(s, axis=0, keepdims=True)
        p = jnp.exp2(s - m)
        vx = jnp.concatenate([v_ref[0, i, cols, :], ones_rows], axis=0)
        acc_ref[h] = _dot(vx, p.astype(BF16))
        m_ref[h] = m

    def body(j, carry):
        start = pl.multiple_of(j * bs, bs)
        onehot = (lax.broadcasted_iota(jnp.int32, (bs, hd), 1) == j).astype(BF16)
        for h in range(N_HEADS):
            cols = slice(h * hd, (h + 1) * hd)
            kx = jnp.concatenate([k_ref[0, pl.ds(start, bs), cols], onehot], axis=1)
            vx = jnp.concatenate([v_ref[0, j, cols, :], ones_rows], axis=0)
            s = _dot_nt(kx, qx_ref[h])
            m_old = m_ref[h]
            m_new = jnp.maximum(m_old, jnp.max(s, axis=0, keepdims=True))
            p = jnp.exp2(s - m_new)
            acc_ref[h] = jnp.exp2(m_old - m_new) * acc_ref[h] + _dot(vx, p.astype(BF16))
            m_ref[h] = m_new
        return carry

    lax.fori_loop(0, i, body, 0)
    for h in range(N_HEADS):
        a = acc_ref[h]
        o_t = a[:hd, :] / a[hd:hd + 1, :]
        o_ref[0, :, h * hd:(h + 1) * hd] = jnp.transpose(o_t).astype(BF16)


def _attn(q, k, vt, km, qg, kg):
    batch, seq, d = q.shape
    nb = seq // MOBA_BLOCK
    tile = pl.BlockSpec((1, MOBA_BLOCK, d), lambda b, i: (b, i, 0))
    gain = pl.BlockSpec((1, HEAD_DIM), lambda b, i: (0, 0))
    return pl.pallas_call(
        functools.partial(_attn_kernel, n_blocks=nb),
        grid=(batch, nb),
        in_specs=[tile,
                  pl.BlockSpec((1, seq, d), lambda b, i: (b, 0, 0)),
                  pl.BlockSpec((1, nb, d, MOBA_BLOCK), lambda b, i: (b, 0, 0, 0)),
                  pl.BlockSpec((1, nb, d), lambda b, i: (b, 0, 0)), gain, gain],
        out_specs=tile,
        out_shape=jax.ShapeDtypeStruct((batch, seq, d), BF16),
        scratch_shapes=[
            pltpu.VMEM((N_HEADS, MOBA_BLOCK, 2 * HEAD_DIM), BF16),
            pltpu.VMEM((N_HEADS, HEAD_DIM + ONES_ROWS, MOBA_BLOCK), F32),
            pltpu.VMEM((N_HEADS, 1, MOBA_BLOCK), F32),
        ],
        compiler_params=pltpu.CompilerParams(dimension_semantics=("parallel", "arbitrary"),
                                             vmem_limit_bytes=VMEM_LIMIT),
        name="moba_attn",
    )(q, k, vt, km, qg, kg)


def _wo_router_kernel(o_ref, res_ref, wo_ref, g_ref, wr_ref, h_ref, hn_ref, gates_ref, *, tm):
    h = res_ref[...] + _dot(o_ref[...], wo_ref[...])
    h_ref[...] = h
    hn = _rms(h, g_ref[...])
    hn_ref[...] = hn.astype(BF16)

    x_hi, x_lo = _split_bf16(hn)
    w_hi, w_lo = _split_bf16(wr_ref[...])
    logits = _dot(x_hi, w_hi) + _dot(x_hi, w_lo) + _dot(x_lo, w_hi)
    lane = lax.broadcasted_iota(jnp.int32, (tm, N_EXPERTS), 1)
    m1 = jnp.max(logits, axis=-1, keepdims=True)
    i1 = jnp.min(jnp.where(logits == m1, lane, N_EXPERTS), axis=-1, keepdims=True)
    pick1 = lane == i1
    rest = jnp.where(pick1, NEG_INF, logits)
    m2 = jnp.max(rest, axis=-1, keepdims=True)
    i2 = jnp.min(jnp.where(rest == m2, lane, N_EXPERTS), axis=-1, keepdims=True)
    pick2 = lane == i2
    e2 = jnp.exp(m2 - m1)
    denom = 1.0 + e2
    gates_ref[...] = jnp.where(pick1, 1.0 / denom, 0.0) + jnp.where(pick2, e2 / denom, 0.0)


def _wo_router(o, res, w_o, g, w_r, *, tm=512):
    t, d = res.shape
    const = lambda *shape: pl.BlockSpec(shape, lambda i: (0,) * len(shape))
    row = lambda width: pl.BlockSpec((tm, width), lambda i: (i, 0))
    return pl.pallas_call(
        functools.partial(_wo_router_kernel, tm=tm),
        grid=(t // tm,),
        in_specs=[row(d), row(d), const(d, d), const(1, d), const(d, N_EXPERTS)],
        out_specs=[row(d), row(d), row(N_EXPERTS)],
        out_shape=[jax.ShapeDtypeStruct((t, d), F32), jax.ShapeDtypeStruct((t, d), BF16),
                   jax.ShapeDtypeStruct((t, N_EXPERTS), F32)],
        compiler_params=pltpu.CompilerParams(dimension_semantics=("parallel",),
                                             vmem_limit_bytes=VMEM_LIMIT),
        name="wo_router",
    )(o, res, w_o, g, w_r)


def _rope_tables(seq):
    half = ROT_DIM // 2
    inv_freq = ROPE_THETA ** (-jnp.arange(half, dtype=F32) / half)
    ang = jnp.arange(seq, dtype=F32)[:, None] * inv_freq[None, :]
    cos = jnp.cos(ang)
    sin = jnp.sin(ang)
    rest = HEAD_DIM - ROT_DIM
    cos_full = jnp.concatenate([cos, cos, jnp.ones((seq, rest), F32)], axis=-1)
    sin_signed = jnp.concatenate([-sin, sin, jnp.zeros((seq, rest), F32)], axis=-1)
    return cos_full, sin_signed


def kernel(x, e_mix_norm, e_w_in, e_conv_w, e_gmlp_ln_g, e_gmlp_ln_b, e_w_spatial, e_b_spatial, e_w_out, e_ffn_norm, e_w_gate, e_w_up, e_w_down, o_mix_norm, o_w_qkv, o_q_norm, o_k_norm, o_w_o, o_ffn_norm, o_w_router, o_w_gate, o_w_up, o_w_down):
    batch, seq, d = x.shape
    t = batch * seq
    xf = x.reshape(t, d)

    bias_full = jnp.repeat(jnp.transpose(e_b_spatial[0]), GMLP_GROUP_DIM, axis=1)
    h1, hn1 = _mixer0(
        xf, e_mix_norm[0][None, :], e_w_in[0].astype(BF16), e_conv_w[0],
        e_gmlp_ln_g[0][None, :], e_gmlp_ln_b[0][None, :], e_w_spatial[0], bias_full,
        e_w_out[0].astype(BF16), e_ffn_norm[0][None, :], seq=seq)
    h2 = _ffn(hn1, h1, None, e_w_gate.astype(BF16), e_w_up.astype(BF16), e_w_down.astype(BF16),
              tm=512, tf=1408)

    cos, sin = _rope_tables(seq)
    qg = o_q_norm[0][None, :]
    kg = o_k_norm[0][None, :]
    q, k, vt, km = _qkv(h2, o_mix_norm[0][None, :], o_w_qkv[0].astype(BF16), qg, kg, cos, sin, seq=seq)
    as_seq = lambda a: a.reshape(batch, -1, d)
    o = _attn(as_seq(q), as_seq(k), vt.reshape(batch, seq // MOBA_BLOCK, d, MOBA_BLOCK),
              as_seq(km), qg, kg)
    h3, hn3, gates = _wo_router(o.reshape(t, d), h2, o_w_o[0].astype(BF16),
                                o_ffn_norm[0][None, :], o_w_router[0])
    gates_t = jnp.transpose(gates)[:, :, None]
    out = _ffn(hn3, h3, gates_t, o_w_gate[0], o_w_up[0], o_w_down[0], tm=1024, tf=512)
    return out.reshape(batch, seq, d)
```

```python
import functools
import math

import jax
import jax.numpy as jnp
from jax import lax
from jax.experimental import pallas as pl
from jax.experimental.pallas import tpu as pltpu

F32 = jnp.float32
BF16 = jnp.bfloat16

D_MODEL = 1024
CONV_DIM = 512
CONV_KERNEL = 3
GMLP_DIM = 512
GMLP_GROUPS = 8
GMLP_GROUP_DIM = GMLP_DIM // GMLP_GROUPS
GMLP_CHUNK = 128
MIX_IN = 3 * CONV_DIM + 2 * GMLP_DIM
N_HEADS = 8
HEAD_DIM = 128
ROT_DIM = HEAD_DIM // 4
ROPE_THETA = 500000.0
MOBA_BLOCK = 256
MOBA_TOPK = 3
N_EXPERTS = 8
EPS = 1e-6

V7X_LANES = 128
VMEM_LIMIT = 56 * 1024 * 1024

NEG_INF = float("-inf")
QK_SCALE = HEAD_DIM ** -0.5 * math.log2(math.e)
MASK_MARGIN = 256.0
ONES_ROWS = 16


def _rms(x, g):
    return x * lax.rsqrt(jnp.mean(x * x, axis=-1, keepdims=True) + EPS) * g


def _dot(a, b):
    return jnp.dot(a, b, preferred_element_type=F32)


def _dot_nt(a, b):
    return lax.dot_general(a, b, (((1,), (1,)), ((), ())), preferred_element_type=F32)


def _split_bf16(x):
    hi = x.astype(BF16)
    lo = (x - hi.astype(F32)).astype(BF16)
    return hi, lo


def _mixer0_kernel(x_ref, g_ref, win_ref, convw_ref, lng_ref, lnb_ref, ws_ref, bias_ref,
                   wout_ref, g2_ref, h_ref, hn_ref, carry_ref, y_ref, *, tm, tiles_per_seq):
    i = pl.program_id(0)
    x = x_ref[...]
    hn = _rms(x, g_ref[...]).astype(BF16)
    proj = _dot(hn, win_ref[...])
    a_h = proj[:, 0:CONV_DIM]
    a_c = proj[:, CONV_DIM:2 * CONV_DIM]
    a_b = proj[:, 2 * CONV_DIM:3 * CONV_DIM]
    g_u = proj[:, 3 * CONV_DIM:3 * CONV_DIM + GMLP_DIM]
    g_v = proj[:, 3 * CONV_DIM + GMLP_DIM:]

    z = a_c * a_h

    @pl.when(i % tiles_per_seq == 0)
    def _():
        carry_ref[...] = jnp.zeros_like(carry_ref)

    prev = carry_ref[...]
    row = lax.broadcasted_iota(jnp.int32, (tm, CONV_DIM), 0)
    z1 = jnp.where(row == 0, prev[7:8, :], pltpu.roll(z, 1, 0))
    z2 = jnp.where(row == 0, prev[6:7, :], jnp.where(row == 1, prev[7:8, :], pltpu.roll(z, 2, 0)))
    carry_ref[...] = z[tm - 8:tm, :]
    cw = convw_ref[...]
    y_a = a_b * (cw[0:1, :] * z2 + cw[1:2, :] * z1 + cw[2:3, :] * z)
    y_ref[:, 0:CONV_DIM] = y_a.astype(BF16)

    mu = jnp.mean(g_v, axis=-1, keepdims=True)
    cen = g_v - mu
    var = jnp.mean(cen * cen, axis=-1, keepdims=True)
    v = cen * lax.rsqrt(var + EPS) * lng_ref[...] + lnb_ref[...]
    tr = lax.broadcasted_iota(jnp.int32, (GMLP_CHUNK, GMLP_CHUNK), 0)
    tc = lax.broadcasted_iota(jnp.int32, (GMLP_CHUNK, GMLP_CHUNK), 1)
    causal = tc <= tr
    lo_half = tc < GMLP_GROUP_DIM
    bias = bias_ref[...]
    for c in range(tm // GMLP_CHUNK):
        rows = slice(c * GMLP_CHUNK, (c + 1) * GMLP_CHUNK)
        pieces = []
        for j in range(GMLP_DIM // V7X_LANES):
            vj = v[rows, j * V7X_LANES:(j + 1) * V7X_LANES]
            v_lo = jnp.where(lo_half, vj, 0.0).astype(BF16)
            v_hi = jnp.where(lo_half, 0.0, vj).astype(BF16)
            w_lo = jnp.where(causal, ws_ref[2 * j], 0.0).astype(BF16)
            w_hi = jnp.where(causal, ws_ref[2 * j + 1], 0.0).astype(BF16)
            pieces.append(_dot(w_lo, v_lo) + _dot(w_hi, v_hi))
        mixed = jnp.concatenate(pieces, axis=1) + bias
        y_ref[rows, CONV_DIM:] = (g_u[rows, :] * mixed).astype(BF16)

    h = x + _dot(y_ref[...], wout_ref[...])
    h_ref[...] = h
    hn_ref[...] = _rms(h, g2_ref[...]).astype(BF16)


def _mixer0(x, g, w_in, conv_w, ln_g, ln_b, w_s, bias_full, w_out, g2, *, seq, tm=512):
    t, d = x.shape
    const = lambda *shape: pl.BlockSpec(shape, lambda i: (0,) * len(shape))
    return pl.pallas_call(
        functools.partial(_mixer0_kernel, tm=tm, tiles_per_seq=seq // tm),
        grid=(t // tm,),
        in_specs=[
            pl.BlockSpec((tm, d), lambda i: (i, 0)),
            const(1, d),
            const(d, MIX_IN),
            const(CONV_KERNEL, CONV_DIM),
            const(1, GMLP_DIM),
            const(1, GMLP_DIM),
            const(GMLP_GROUPS, GMLP_CHUNK, GMLP_CHUNK),
            const(GMLP_CHUNK, GMLP_DIM),
            const(CONV_DIM + GMLP_DIM, d),
            const(1, d),
        ],
        out_specs=[pl.BlockSpec((tm, d), lambda i: (i, 0)), pl.BlockSpec((tm, d), lambda i: (i, 0))],
        out_shape=[jax.ShapeDtypeStruct((t, d), F32), jax.ShapeDtypeStruct((t, d), BF16)],
        scratch_shapes=[pltpu.VMEM((8, CONV_DIM), F32), pltpu.VMEM((tm, CONV_DIM + GMLP_DIM), BF16)],
        compiler_params=pltpu.CompilerParams(dimension_semantics=("arbitrary",),
                                             vmem_limit_bytes=VMEM_LIMIT),
        name="mixer0",
    )(x, g, w_in, conv_w, ln_g, ln_b, w_s, bias_full, w_out, g2)


def _swiglu(x, wg, wu, wd):
    a = _dot(x, wg.astype(BF16))
    u = _dot(x, wu.astype(BF16))
    hact = (a * jax.nn.sigmoid(a) * u).astype(BF16)
    return _dot(hact, wd.astype(BF16))


def _ffn_kernel(x_ref, res_ref, wg_ref, wu_ref, wd_ref, o_ref):
    @pl.when(pl.program_id(1) == 0)
    def _():
        o_ref[...] = res_ref[...]

    o_ref[...] += _swiglu(x_ref[...], wg_ref[...], wu_ref[...], wd_ref[...])


def _ffn(xn, res, w_gate, w_up, w_down, *, tm, tf):
    t, d = xn.shape
    f_dim = w_gate.shape[1]
    row = pl.BlockSpec((tm, d), lambda i, f: (i, 0))
    return pl.pallas_call(
        _ffn_kernel,
        grid=(t // tm, f_dim // tf),
        in_specs=[row, row,
                  pl.BlockSpec((d, tf), lambda i, f: (0, f)),
                  pl.BlockSpec((d, tf), lambda i, f: (0, f)),
                  pl.BlockSpec((tf, d), lambda i, f: (f, 0))],
        out_specs=row,
        out_shape=jax.ShapeDtypeStruct((t, d), F32),
        compiler_params=pltpu.CompilerParams(dimension_semantics=("parallel", "arbitrary"),
                                             vmem_limit_bytes=VMEM_LIMIT),
        name="dense_ffn",
    )(xn, res, w_gate, w_up, w_down)


def _qkv_kernel(h_ref, g_ref, w_ref, qg_ref, kg_ref, cos_ref, sin_ref,
                q_ref, k_ref, v_ref, km_ref, *, tm):
    hn = _rms(h_ref[...], g_ref[...]).astype(BF16)
    qkv = _dot(hn, w_ref[...])
    cos = cos_ref[...]
    sin = sin_ref[...]
    lane = lax.broadcasted_iota(jnp.int32, (tm, HEAD_DIM), 1)
    first = lane < ROT_DIM // 2

    def rope(x):
        partner = jnp.where(first, pltpu.roll(x, HEAD_DIM - ROT_DIM // 2, 1),
                            pltpu.roll(x, ROT_DIM // 2, 1))
        return x * cos + partner * sin

    hd = N_HEADS * HEAD_DIM
    for c in range(tm // MOBA_BLOCK):
        v_ref[c] = jnp.transpose(qkv[c * MOBA_BLOCK:(c + 1) * MOBA_BLOCK, 2 * hd:]).astype(BF16)
    for h in range(N_HEADS):
        cols = slice(h * HEAD_DIM, (h + 1) * HEAD_DIM)
        q = rope(_rms(qkv[:, cols], qg_ref[...]))
        k = rope(_rms(qkv[:, hd + h * HEAD_DIM:hd + (h + 1) * HEAD_DIM], kg_ref[...]))
        q_ref[:, cols] = (q * QK_SCALE).astype(BF16)
        k_ref[:, cols] = k.astype(BF16)
        for c in range(tm // MOBA_BLOCK):
            km_ref[c, :, cols] = jnp.mean(
                k[c * MOBA_BLOCK:(c + 1) * MOBA_BLOCK, :], axis=0, keepdims=True)


def _qkv(h, g, w, qg, kg, cos, sin, *, seq, tm=512):
    t, d = h.shape
    ns = seq // tm
    hd = N_HEADS * HEAD_DIM
    const = lambda *shape: pl.BlockSpec(shape, lambda i: (0,) * len(shape))
    row_spec = pl.BlockSpec((tm, hd), lambda i: (i, 0))
    row_shape = jax.ShapeDtypeStruct((t, hd), BF16)
    return pl.pallas_call(
        functools.partial(_qkv_kernel, tm=tm),
        grid=(t // tm,),
        in_specs=[
            pl.BlockSpec((tm, d), lambda i: (i, 0)),
            const(1, d),
            const(d, 3 * hd),
            const(1, HEAD_DIM),
            const(1, HEAD_DIM),
            pl.BlockSpec((tm, HEAD_DIM), lambda i: (i % ns, 0)),
            pl.BlockSpec((tm, HEAD_DIM), lambda i: (i % ns, 0)),
        ],
        out_specs=[row_spec, row_spec,
                   pl.BlockSpec((tm // MOBA_BLOCK, hd, MOBA_BLOCK), lambda i: (i, 0, 0)),
                   pl.BlockSpec((tm // MOBA_BLOCK, 1, hd), lambda i: (i, 0, 0))],
        out_shape=[row_shape, row_shape,
                   jax.ShapeDtypeStruct((t // MOBA_BLOCK, hd, MOBA_BLOCK), BF16),
                   jax.ShapeDtypeStruct((t // MOBA_BLOCK, 1, hd), F32)],
        compiler_params=pltpu.CompilerParams(dimension_semantics=("parallel",),
                                             vmem_limit_bytes=VMEM_LIMIT),
        name="qkv",
    )(h, g, w, qg, kg, cos, sin)


def _attn_kernel(q_ref, k_ref, v_ref, km_ref, qg_ref, kg_ref, o_ref, qx_ref, acc_ref, m_ref,
                 *, n_blocks):
    i = pl.program_id(1)
    bs = MOBA_BLOCK
    hd = HEAD_DIM
    bound = QK_SCALE * hd * jnp.max(jnp.abs(qg_ref[...]), axis=-1, keepdims=True) * jnp.max(
        jnp.abs(kg_ref[...]), axis=-1, keepdims=True)
    big = 2.1 * bound + MASK_MARGIN
    blk = lax.broadcasted_iota(jnp.int32, (n_blocks, bs), 0)
    past = blk < i
    key = lax.broadcasted_iota(jnp.int32, (bs, bs), 0)
    qry = lax.broadcasted_iota(jnp.int32, (bs, bs), 1)
    causal = key <= qry
    ones_rows = jnp.ones((ONES_ROWS, bs), BF16)
    own = pl.multiple_of(i * bs, bs)

    for h in range(N_HEADS):
        cols = slice(h * hd, (h + 1) * hd)
        q = q_ref[0, :, cols]
        km_hi, km_lo = _split_bf16(km_ref[0, :, cols])
        gate_t = _dot_nt(km_hi, q) + _dot_nt(km_lo, q)
        cur = jnp.where(past, gate_t, NEG_INF)
        sel = jnp.zeros((n_blocks, bs), F32)
        for _ in range(min(MOBA_TOPK, n_blocks)):
            top = jnp.max(cur, axis=0, keepdims=True)
            idx = jnp.min(jnp.where(cur == top, blk, n_blocks), axis=0, keepdims=True)
            pick = blk == idx
            sel = jnp.where(pick & past, 1.0, sel)
            cur = jnp.where(pick, NEG_INF, cur)
        bias_t = jnp.where(sel > 0.5, 0.0, -big)
        bias_t = jnp.concatenate([bias_t, jnp.zeros((hd - n_blocks, bs), F32)], axis=0)
        qx_ref[h] = jnp.concatenate([q, jnp.transpose(bias_t).astype(BF16)], axis=1)

        s = _dot_nt(k_ref[0, pl.ds(own, bs), cols], q)
        s = jnp.where(causal, s, NEG_INF)
        m = jnp.max(s, axis=0, keepdims=True)
        p = jnp.exp2(s - m)
        vx = jnp.concatenate([v_ref[0, i, cols, :], ones_rows], axis=0)
        acc_ref[h] = _dot(vx, p.astype(BF16))
        m_ref[h] = m

    def body(j, carry):
        start = pl.multiple_of(j * bs, bs)
        onehot = (lax.broadcasted_iota(jnp.int32, (bs, hd), 1) == j).astype(BF16)
        for h in range(N_HEADS):
            cols = slice(h * hd, (h + 1) * hd)
            kx = jnp.concatenate([k_ref[0, pl.ds(start, bs), cols], onehot], axis=1)
            vx = jnp.concatenate([v_ref[0, j, cols, :], ones_rows], axis=0)
            s = _dot_nt(kx, qx_ref[h])
            m_old = m_ref[h]
            m_new = jnp.maximum(m_old, jnp.max(s, axis=0, keepdims=True))
            p = jnp.exp2(s - m_new)
            acc_ref[h] = jnp.exp2(m_old - m_new) * acc_ref[h] + _dot(vx, p.astype(BF16))
            m_ref[h] = m_new
        return carry

    lax.fori_loop(0, i, body, 0)
    for h in range(N_HEADS):
        a = acc_ref[h]
        o_t = a[:hd, :] / a[hd:hd + 1, :]
        o_ref[0, :, h * hd:(h + 1) * hd] = jnp.transpose(o_t).astype(BF16)


def _attn(q, k, vt, km, qg, kg):
    batch, seq, d = q.shape
    nb = seq // MOBA_BLOCK
    tile = pl.BlockSpec((1, MOBA_BLOCK, d), lambda b, i: (b, i, 0))
    gain = pl.BlockSpec((1, HEAD_DIM), lambda b, i: (0, 0))
    return pl.pallas_call(
        functools.partial(_attn_kernel, n_blocks=nb),
        grid=(batch, nb),
        in_specs=[tile,
                  pl.BlockSpec((1, seq, d), lambda b, i: (b, 0, 0)),
                  pl.BlockSpec((1, nb, d, MOBA_BLOCK), lambda b, i: (b, 0, 0, 0)),
                  pl.BlockSpec((1, nb, d), lambda b, i: (b, 0, 0)), gain, gain],
        out_specs=tile,
        out_shape=jax.ShapeDtypeStruct((batch, seq, d), BF16),
        scratch_shapes=[
            pltpu.VMEM((N_HEADS, MOBA_BLOCK, 2 * HEAD_DIM), BF16),
            pltpu.VMEM((N_HEADS, HEAD_DIM + ONES_ROWS, MOBA_BLOCK), F32),
            pltpu.VMEM((N_HEADS, 1, MOBA_BLOCK), F32),
        ],
        compiler_params=pltpu.CompilerParams(dimension_semantics=("parallel", "arbitrary"),
                                             vmem_limit_bytes=VMEM_LIMIT),
        name="moba_attn",
    )(q, k, vt, km, qg, kg)


def _wo_router_kernel(o_ref, res_ref, wo_ref, g_ref, wr_ref, h_ref, hn_ref, route_ref, counts_ref,
                      cnt_ref, *, tm):
    h = res_ref[...] + _dot(o_ref[...], wo_ref[...])
    h_ref[...] = h
    hn = _rms(h, g_ref[...])
    hn_ref[...] = hn

    @pl.when(pl.program_id(0) == 0)
    def _():
        cnt_ref[...] = jnp.zeros_like(cnt_ref)

    x_hi, x_lo = _split_bf16(hn)
    w_hi, w_lo = _split_bf16(wr_ref[...])
    logits = _dot(x_hi, w_hi) + _dot(x_hi, w_lo) + _dot(x_lo, w_hi)
    lane = lax.broadcasted_iota(jnp.int32, (tm, N_EXPERTS), 1)
    m1 = jnp.max(logits, axis=-1, keepdims=True)
    i1 = jnp.min(jnp.where(logits == m1, lane, N_EXPERTS), axis=-1, keepdims=True)
    pick1 = lane == i1
    rest = jnp.where(pick1, NEG_INF, logits)
    m2 = jnp.max(rest, axis=-1, keepdims=True)
    i2 = jnp.min(jnp.where(rest == m2, lane, N_EXPERTS), axis=-1, keepdims=True)
    pick2 = lane == i2
    e2 = jnp.exp(m2 - m1)
    denom = 1.0 + e2

    chosen = jnp.where(pick1 | pick2, 1.0, 0.0)
    tr = lax.broadcasted_iota(jnp.int32, (tm, tm), 0)
    tc = lax.broadcasted_iota(jnp.int32, (tm, tm), 1)
    earlier = jnp.where(tc < tr, 1.0, 0.0).astype(BF16)
    rank = _dot(earlier, chosen.astype(BF16)) + cnt_ref[...]
    cnt_ref[...] += jnp.sum(chosen, axis=0, keepdims=True)
    counts_ref[...] = cnt_ref[...]
    rank1 = jnp.sum(jnp.where(pick1, rank, 0.0), axis=-1, keepdims=True)
    rank2 = jnp.sum(jnp.where(pick2, rank, 0.0), axis=-1, keepdims=True)
    fields = [i1.astype(F32), i2.astype(F32), 1.0 / denom, e2 / denom, rank1, rank2]
    route = jnp.zeros((tm, N_EXPERTS), F32)
    for slot, val in enumerate(fields):
        route = jnp.where(lane == slot, val, route)
    route_ref[...] = route


def _wo_router(o, res, w_o, g, w_r, *, tm=512):
    t, d = res.shape
    const = lambda *shape: pl.BlockSpec(shape, lambda i: (0,) * len(shape))
    row = lambda width: pl.BlockSpec((tm, width), lambda i: (i, 0))
    return pl.pallas_call(
        functools.partial(_wo_router_kernel, tm=tm),
        grid=(t // tm,),
        in_specs=[row(d), row(d), const(d, d), const(1, d), const(d, N_EXPERTS)],
        out_specs=[row(d), row(d), row(N_EXPERTS), const(1, N_EXPERTS)],
        out_shape=[jax.ShapeDtypeStruct((t, d), F32), jax.ShapeDtypeStruct((t, d), F32),
                   jax.ShapeDtypeStruct((t, N_EXPERTS), F32),
                   jax.ShapeDtypeStruct((1, N_EXPERTS), F32)],
        scratch_shapes=[pltpu.VMEM((1, N_EXPERTS), F32)],
        compiler_params=pltpu.CompilerParams(dimension_semantics=("arbitrary",),
                                             vmem_limit_bytes=VMEM_LIMIT),
        name="wo_router",
    )(o, res, w_o, g, w_r)


def _row_copy(src_ref, src_row, dst_ref, dst_row, sem):
    return pltpu.make_async_copy(src_ref.at[pl.ds(src_row, 1), :], dst_ref.at[pl.ds(dst_row, 1), :], sem)


def _moe_scatter_kernel(pos_ref, x_ref, zeros_ref, xs_ref, sem, *, tm):
    del zeros_ref

    def start(r, carry):
        _row_copy(x_ref, r, xs_ref, pos_ref[0, 0, 2 * r], sem).start()
        _row_copy(x_ref, r, xs_ref, pos_ref[0, 0, 2 * r + 1], sem).start()
        return carry

    lax.fori_loop(0, tm, start, 0, unroll=8)

    def wait(r, carry):
        _row_copy(x_ref, 0, xs_ref, 0, sem).wait()
        _row_copy(x_ref, 0, xs_ref, 0, sem).wait()
        return carry

    lax.fori_loop(0, tm, wait, 0, unroll=8)


def _moe_scatter(pos, x, n_rows, *, tm=256):
    t, d = x.shape
    return pl.pallas_call(
        functools.partial(_moe_scatter_kernel, tm=tm),
        grid=(t // tm,),
        in_specs=[pl.BlockSpec((1, 1, 2 * tm), lambda i: (i, 0, 0), memory_space=pltpu.SMEM),
                  pl.BlockSpec((tm, d), lambda i: (i, 0)),
                  pl.BlockSpec(memory_space=pl.ANY)],
        out_specs=pl.BlockSpec(memory_space=pl.ANY),
        out_shape=jax.ShapeDtypeStruct((n_rows, d), F32),
        scratch_shapes=[pltpu.SemaphoreType.DMA(())],
        input_output_aliases={2: 0},
        compiler_params=pltpu.CompilerParams(dimension_semantics=("arbitrary",),
                                             vmem_limit_bytes=VMEM_LIMIT),
        name="moe_scatter",
    )(pos.reshape(t // tm, 1, 2 * tm), x, jnp.zeros((n_rows, d), F32))


def _moe_gemm_kernel(te_ref, used_ref, x_ref, wg_ref, wu_ref, wd_ref, y_ref, xb_ref):
    del te_ref
    i = pl.program_id(0)
    f = pl.program_id(1)
    live = i < used_ref[0]

    @pl.when(f == 0)
    def _():
        xb_ref[...] = x_ref[...].astype(BF16)
        y_ref[...] = jnp.zeros_like(y_ref)

    @pl.when(live)
    def _():
        y_ref[...] += _swiglu(xb_ref[...], wg_ref[0], wu_ref[0], wd_ref[0])


def _moe_gemm(tile_expert, n_used, xs, w_gate, w_up, w_down, *, tg, tf):
    n_rows, d = xs.shape
    f_dim = w_gate.shape[-1]
    row = pl.BlockSpec((tg, d), lambda i, f, te, used: (i, 0))
    return pl.pallas_call(
        _moe_gemm_kernel,
        grid_spec=pltpu.PrefetchScalarGridSpec(
            num_scalar_prefetch=2,
            grid=(n_rows // tg, f_dim // tf),
            in_specs=[row,
                      pl.BlockSpec((1, d, tf), lambda i, f, te, used: (te[i], 0, f)),
                      pl.BlockSpec((1, d, tf), lambda i, f, te, used: (te[i], 0, f)),
                      pl.BlockSpec((1, tf, d), lambda i, f, te, used: (te[i], f, 0))],
            out_specs=row,
            scratch_shapes=[pltpu.VMEM((tg, d), BF16)]),
        out_shape=jax.ShapeDtypeStruct((n_rows, d), F32),
        compiler_params=pltpu.CompilerParams(dimension_semantics=("parallel", "arbitrary"),
                                             vmem_limit_bytes=VMEM_LIMIT),
        name="moe_gemm",
    )(tile_expert, n_used, xs, w_gate, w_up, w_down)


def _moe_combine_kernel(pos_ref, h_ref, route_ref, ys_ref, o_ref, buf_ref, sem, *, tm):
    def start(r, carry):
        _row_copy(ys_ref, pos_ref[0, 0, 2 * r], buf_ref.at[0], r, sem).start()
        _row_copy(ys_ref, pos_ref[0, 0, 2 * r + 1], buf_ref.at[1], r, sem).start()
        return carry

    lax.fori_loop(0, tm, start, 0, unroll=8)

    def wait(r, carry):
        _row_copy(ys_ref, 0, buf_ref.at[0], 0, sem).wait()
        _row_copy(ys_ref, 0, buf_ref.at[1], 0, sem).wait()
        return carry

    lax.fori_loop(0, tm, wait, 0, unroll=8)
    route = route_ref[...]
    o_ref[...] = h_ref[...] + route[:, 2:3] * buf_ref[0] + route[:, 3:4] * buf_ref[1]


def _moe_combine(pos, h, route, ys, *, tm=256):
    t, d = h.shape
    return pl.pallas_call(
        functools.partial(_moe_combine_kernel, tm=tm),
        grid=(t // tm,),
        in_specs=[pl.BlockSpec((1, 1, 2 * tm), lambda i: (i, 0, 0), memory_space=pltpu.SMEM),
                  pl.BlockSpec((tm, d), lambda i: (i, 0)),
                  pl.BlockSpec((tm, N_EXPERTS), lambda i: (i, 0)),
                  pl.BlockSpec(memory_space=pl.ANY)],
        out_specs=pl.BlockSpec((tm, d), lambda i: (i, 0)),
        out_shape=jax.ShapeDtypeStruct((t, d), F32),
        scratch_shapes=[pltpu.VMEM((2, tm, d), F32), pltpu.SemaphoreType.DMA(())],
        compiler_params=pltpu.CompilerParams(dimension_semantics=("arbitrary",),
                                             vmem_limit_bytes=VMEM_LIMIT),
        name="moe_combine",
    )(pos.reshape(t // tm, 1, 2 * tm), h, route, ys)


def _moe(h, hn, route, counts, w_gate, w_up, w_down, *, tg=1024, tf=512):
    t, d = h.shape
    n_tiles = (2 * t + N_EXPERTS * (tg - 1)) // tg
    counts = counts[0].astype(jnp.int32)
    padded = (counts + tg - 1) // tg * tg
    ends = jnp.cumsum(padded)
    starts = ends - padded
    experts = route[:, 0:2].astype(jnp.int32)
    pos = starts[experts] + route[:, 4:6].astype(jnp.int32)
    tile_expert = jnp.minimum(
        jnp.searchsorted(ends, jnp.arange(n_tiles, dtype=jnp.int32) * tg, side="right"),
        N_EXPERTS - 1).astype(jnp.int32)
    n_used = (ends[-1:] // tg).astype(jnp.int32)
    xs = _moe_scatter(pos, hn, n_tiles * tg)
    ys = _moe_gemm(tile_expert, n_used, xs, w_gate, w_up, w_down, tg=tg, tf=tf)
    return _moe_combine(pos, h, route, ys)


def _rope_tables(seq):
    half = ROT_DIM // 2
    inv_freq = ROPE_THETA ** (-jnp.arange(half, dtype=F32) / half)
    ang = jnp.arange(seq, dtype=F32)[:, None] * inv_freq[None, :]
    cos = jnp.cos(ang)
    sin = jnp.sin(ang)
    rest = HEAD_DIM - ROT_DIM
    cos_full = jnp.concatenate([cos, cos, jnp.ones((seq, rest), F32)], axis=-1)
    sin_signed = jnp.concatenate([-sin, sin, jnp.zeros((seq, rest), F32)], axis=-1)
    return cos_full, sin_signed


def kernel(x, e_mix_norm, e_w_in, e_conv_w, e_gmlp_ln_g, e_gmlp_ln_b, e_w_spatial, e_b_spatial, e_w_out, e_ffn_norm, e_w_gate, e_w_up, e_w_down, o_mix_norm, o_w_qkv, o_q_norm, o_k_norm, o_w_o, o_ffn_norm, o_w_router, o_w_gate, o_w_up, o_w_down):
    batch, seq, d = x.shape
    t = batch * seq
    xf = x.reshape(t, d)

    bias_full = jnp.repeat(jnp.transpose(e_b_spatial[0]), GMLP_GROUP_DIM, axis=1)
    h1, hn1 = _mixer0(
        xf, e_mix_norm[0][None, :], e_w_in[0].astype(BF16), e_conv_w[0],
        e_gmlp_ln_g[0][None, :], e_gmlp_ln_b[0][None, :], e_w_spatial[0], bias_full,
        e_w_out[0].astype(BF16), e_ffn_norm[0][None, :], seq=seq)
    h2 = _ffn(hn1, h1, e_w_gate[0].astype(BF16), e_w_up[0].astype(BF16), e_w_down[0].astype(BF16),
              tm=512, tf=1408)

    cos, sin = _rope_tables(seq)
    qg = o_q_norm[0][None, :]
    kg = o_k_norm[0][None, :]
    q, k, vt, km = _qkv(h2, o_mix_norm[0][None, :], o_w_qkv[0].astype(BF16), qg, kg, cos, sin, seq=seq)
    as_seq = lambda a: a.reshape(batch, -1, d)
    o = _attn(as_seq(q), as_seq(k), vt.reshape(batch, seq // MOBA_BLOCK, d, MOBA_BLOCK),
              as_seq(km), qg, kg)
    h3, hn3, route, counts = _wo_router(o.reshape(t, d), h2, o_w_o[0].astype(BF16),
                                        o_ffn_norm[0][None, :], o_w_router[0])
    out = _moe(h3, hn3, route, counts, o_w_gate[0], o_w_up[0], o_w_down[0])
    return out.reshape(batch, seq, d)
```

```python
import functools
import math

import jax
import jax.numpy as jnp
from jax import lax
from jax.experimental import pallas as pl
from jax.experimental.pallas import tpu as pltpu

F32 = jnp.float32
BF16 = jnp.bfloat16

D_MODEL = 1024
CONV_DIM = 512
CONV_KERNEL = 3
GMLP_DIM = 512
GMLP_GROUPS = 8
GMLP_GROUP_DIM = GMLP_DIM // GMLP_GROUPS
GMLP_CHUNK = 128
MIX_IN = 3 * CONV_DIM + 2 * GMLP_DIM
N_HEADS = 8
HEAD_DIM = 128
ROT_DIM = HEAD_DIM // 4
ROPE_THETA = 500000.0
MOBA_BLOCK = 256
MOBA_TOPK = 3
N_EXPERTS = 8
EPS = 1e-6

V7X_LANES = 128
VMEM_LIMIT = 56 * 1024 * 1024

NEG_INF = float("-inf")
QK_SCALE = HEAD_DIM ** -0.5 * math.log2(math.e)
MASK_MARGIN = 256.0
M_INIT = -1e30
ONES_ROWS = 16


def _rms(x, g):
    return x * lax.rsqrt(jnp.mean(x * x, axis=-1, keepdims=True) + EPS) * g


def _dot(a, b):
    return jnp.dot(a, b, preferred_element_type=F32)


def _dot_nt(a, b):
    return lax.dot_general(a, b, (((1,), (1,)), ((), ())), preferred_element_type=F32)


def _split_bf16(x):
    hi = x.astype(BF16)
    lo = (x - hi.astype(F32)).astype(BF16)
    return hi, lo


def _mixer0_kernel(x_ref, g_ref, win_ref, convw_ref, lng_ref, lnb_ref, ws_ref, bias_ref,
                   wout_ref, g2_ref, h_ref, hn_ref, carry_ref, y_ref, *, tm, tiles_per_seq):
    i = pl.program_id(0)
    x = x_ref[...]
    hn = _rms(x, g_ref[...]).astype(BF16)
    proj = _dot(hn, win_ref[...])
    a_h = proj[:, 0:CONV_DIM]
    a_c = proj[:, CONV_DIM:2 * CONV_DIM]
    a_b = proj[:, 2 * CONV_DIM:3 * CONV_DIM]
    g_u = proj[:, 3 * CONV_DIM:3 * CONV_DIM + GMLP_DIM]
    g_v = proj[:, 3 * CONV_DIM + GMLP_DIM:]

    z = a_c * a_h

    @pl.when(i % tiles_per_seq == 0)
    def _():
        carry_ref[...] = jnp.zeros_like(carry_ref)

    prev = carry_ref[...]
    row = lax.broadcasted_iota(jnp.int32, (tm, CONV_DIM), 0)
    z1 = jnp.where(row == 0, prev[7:8, :], pltpu.roll(z, 1, 0))
    z2 = jnp.where(row == 0, prev[6:7, :], jnp.where(row == 1, prev[7:8, :], pltpu.roll(z, 2, 0)))
    carry_ref[...] = z[tm - 8:tm, :]
    cw = convw_ref[...]
    y_a = a_b * (cw[0:1, :] * z2 + cw[1:2, :] * z1 + cw[2:3, :] * z)
    y_ref[:, 0:CONV_DIM] = y_a.astype(BF16)

    mu = jnp.mean(g_v, axis=-1, keepdims=True)
    cen = g_v - mu
    var = jnp.mean(cen * cen, axis=-1, keepdims=True)
    v = cen * lax.rsqrt(var + EPS) * lng_ref[...] + lnb_ref[...]
    tr = lax.broadcasted_iota(jnp.int32, (GMLP_CHUNK, GMLP_CHUNK), 0)
    tc = lax.broadcasted_iota(jnp.int32, (GMLP_CHUNK, GMLP_CHUNK), 1)
    causal = tc <= tr
    lo_half = tc < GMLP_GROUP_DIM
    bias = bias_ref[...]
    for c in range(tm // GMLP_CHUNK):
        rows = slice(c * GMLP_CHUNK, (c + 1) * GMLP_CHUNK)
        pieces = []
        for j in range(GMLP_DIM // V7X_LANES):
            vj = v[rows, j * V7X_LANES:(j + 1) * V7X_LANES]
            v_lo = jnp.where(lo_half, vj, 0.0).astype(BF16)
            v_hi = jnp.where(lo_half, 0.0, vj).astype(BF16)
            w_lo = jnp.where(causal, ws_ref[2 * j], 0.0).astype(BF16)
            w_hi = jnp.where(causal, ws_ref[2 * j + 1], 0.0).astype(BF16)
            pieces.append(_dot(w_lo, v_lo) + _dot(w_hi, v_hi))
        mixed = jnp.concatenate(pieces, axis=1) + bias
        y_ref[rows, CONV_DIM:] = (g_u[rows, :] * mixed).astype(BF16)

    h = x + _dot(y_ref[...], wout_ref[...])
    h_ref[...] = h
    hn_ref[...] = _rms(h, g2_ref[...]).astype(BF16)


def _mixer0(x, g, w_in, conv_w, ln_g, ln_b, w_s, bias_full, w_out, g2, *, seq, tm=512):
    t, d = x.shape
    const = lambda *shape: pl.BlockSpec(shape, lambda i: (0,) * len(shape))
    return pl.pallas_call(
        functools.partial(_mixer0_kernel, tm=tm, tiles_per_seq=seq // tm),
        grid=(t // tm,),
        in_specs=[
            pl.BlockSpec((tm, d), lambda i: (i, 0)),
            const(1, d),
            const(d, MIX_IN),
            const(CONV_KERNEL, CONV_DIM),
            const(1, GMLP_DIM),
            const(1, GMLP_DIM),
            const(GMLP_GROUPS, GMLP_CHUNK, GMLP_CHUNK),
            const(GMLP_CHUNK, GMLP_DIM),
            const(CONV_DIM + GMLP_DIM, d),
            const(1, d),
        ],
        out_specs=[pl.BlockSpec((tm, d), lambda i: (i, 0)), pl.BlockSpec((tm, d), lambda i: (i, 0))],
        out_shape=[jax.ShapeDtypeStruct((t, d), F32), jax.ShapeDtypeStruct((t, d), BF16)],
        scratch_shapes=[pltpu.VMEM((8, CONV_DIM), F32), pltpu.VMEM((tm, CONV_DIM + GMLP_DIM), BF16)],
        compiler_params=pltpu.CompilerParams(dimension_semantics=("arbitrary",),
                                             vmem_limit_bytes=VMEM_LIMIT),
        name="mixer0",
    )(x, g, w_in, conv_w, ln_g, ln_b, w_s, bias_full, w_out, g2)


def _swiglu(x, wg, wu, wd):
    a = _dot(x, wg.astype(BF16))
    u = _dot(x, wu.astype(BF16))
    hact = (a * jax.nn.sigmoid(a) * u).astype(BF16)
    return _dot(hact, wd.astype(BF16))


def _ffn_kernel(x_ref, res_ref, wg_ref, wu_ref, wd_ref, o_ref):
    @pl.when(pl.program_id(1) == 0)
    def _():
        o_ref[...] = res_ref[...]

    o_ref[...] += _swiglu(x_ref[...], wg_ref[...], wu_ref[...], wd_ref[...])


def _ffn(xn, res, w_gate, w_up, w_down, *, tm, tf):
    t, d = xn.shape
    f_dim = w_gate.shape[1]
    row = pl.BlockSpec((tm, d), lambda i, f: (i, 0))
    return pl.pallas_call(
        _ffn_kernel,
        grid=(t // tm, f_dim // tf),
        in_specs=[row, row,
                  pl.BlockSpec((d, tf), lambda i, f: (0, f)),
                  pl.BlockSpec((d, tf), lambda i, f: (0, f)),
                  pl.BlockSpec((tf, d), lambda i, f: (f, 0))],
        out_specs=row,
        out_shape=jax.ShapeDtypeStruct((t, d), F32),
        compiler_params=pltpu.CompilerParams(dimension_semantics=("parallel", "arbitrary"),
                                             vmem_limit_bytes=VMEM_LIMIT),
        name="dense_ffn",
    )(xn, res, w_gate, w_up, w_down)


def _qkv_kernel(h_ref, g_ref, w_ref, qg_ref, kg_ref, cos_ref, sin_ref,
                q_ref, k_ref, v_ref, km_ref, *, tm):
    hn = _rms(h_ref[...], g_ref[...]).astype(BF16)
    qkv = _dot(hn, w_ref[...])
    cos = cos_ref[...]
    sin = sin_ref[...]
    lane = lax.broadcasted_iota(jnp.int32, (tm, HEAD_DIM), 1)
    first = lane < ROT_DIM // 2

    def rope(x):
        partner = jnp.where(first, pltpu.roll(x, HEAD_DIM - ROT_DIM // 2, 1),
                            pltpu.roll(x, ROT_DIM // 2, 1))
        return x * cos + partner * sin

    hd = N_HEADS * HEAD_DIM
    for c in range(tm // MOBA_BLOCK):
        v_ref[c] = jnp.transpose(qkv[c * MOBA_BLOCK:(c + 1) * MOBA_BLOCK, 2 * hd:]).astype(BF16)
    for h in range(N_HEADS):
        cols = slice(h * HEAD_DIM, (h + 1) * HEAD_DIM)
        q = rope(_rms(qkv[:, cols], qg_ref[...]))
        k = rope(_rms(qkv[:, hd + h * HEAD_DIM:hd + (h + 1) * HEAD_DIM], kg_ref[...]))
        q_ref[:, cols] = (q * QK_SCALE).astype(BF16)
        k_ref[:, cols] = k.astype(BF16)
        for c in range(tm // MOBA_BLOCK):
            km_ref[c, :, cols] = jnp.mean(
                k[c * MOBA_BLOCK:(c + 1) * MOBA_BLOCK, :], axis=0, keepdims=True)


def _qkv(h, g, w, qg, kg, cos, sin, *, seq, tm=512):
    t, d = h.shape
    ns = seq // tm
    hd = N_HEADS * HEAD_DIM
    const = lambda *shape: pl.BlockSpec(shape, lambda i: (0,) * len(shape))
    row_spec = pl.BlockSpec((tm, hd), lambda i: (i, 0))
    row_shape = jax.ShapeDtypeStruct((t, hd), BF16)
    return pl.pallas_call(
        functools.partial(_qkv_kernel, tm=tm),
        grid=(t // tm,),
        in_specs=[
            pl.BlockSpec((tm, d), lambda i: (i, 0)),
            const(1, d),
            const(d, 3 * hd),
            const(1, HEAD_DIM),
            const(1, HEAD_DIM),
            pl.BlockSpec((tm, HEAD_DIM), lambda i: (i % ns, 0)),
            pl.BlockSpec((tm, HEAD_DIM), lambda i: (i % ns, 0)),
        ],
        out_specs=[row_spec, row_spec,
                   pl.BlockSpec((tm // MOBA_BLOCK, hd, MOBA_BLOCK), lambda i: (i, 0, 0)),
                   pl.BlockSpec((tm // MOBA_BLOCK, 1, hd), lambda i: (i, 0, 0))],
        out_shape=[row_shape, row_shape,
                   jax.ShapeDtypeStruct((t // MOBA_BLOCK, hd, MOBA_BLOCK), BF16),
                   jax.ShapeDtypeStruct((t // MOBA_BLOCK, 1, hd), F32)],
        compiler_params=pltpu.CompilerParams(dimension_semantics=("parallel",),
                                             vmem_limit_bytes=VMEM_LIMIT),
        name="qkv",
    )(h, g, w, qg, kg, cos, sin)


def _attn_kernel(q_ref, k_ref, v_ref, km_ref, qg_ref, kg_ref, o_ref, qx_ref, acc_ref, m_ref,
                 s0_ref, s1_ref, *, n_blocks):
    i = pl.program_id(1)
    bs = MOBA_BLOCK
    hd = HEAD_DIM
    bound = QK_SCALE * hd * jnp.max(jnp.abs(qg_ref[...]), axis=-1, keepdims=True) * jnp.max(
        jnp.abs(kg_ref[...]), axis=-1, keepdims=True)
    big = 2.1 * bound + MASK_MARGIN
    blk = lax.broadcasted_iota(jnp.int32, (n_blocks, bs), 0)
    past = blk < i
    key = lax.broadcasted_iota(jnp.int32, (bs, bs), 0)
    qry = lax.broadcasted_iota(jnp.int32, (bs, bs), 1)
    causal = key <= qry
    ones_rows = jnp.ones((ONES_ROWS, bs), BF16)
    own = pl.multiple_of(i * bs, bs)

    for h in range(N_HEADS):
        cols = slice(h * hd, (h + 1) * hd)
        q = q_ref[0, :, cols]
        km_hi, km_lo = _split_bf16(km_ref[0, :, cols])
        gate_t = _dot_nt(km_hi, q) + _dot_nt(km_lo, q)
        cur = jnp.where(past, gate_t, NEG_INF)
        sel = jnp.zeros((n_blocks, bs), F32)
        for _ in range(min(MOBA_TOPK, n_blocks)):
            top = jnp.max(cur, axis=0, keepdims=True)
            idx = jnp.min(jnp.where(cur == top, blk, n_blocks), axis=0, keepdims=True)
            pick = blk == idx
            sel = jnp.where(pick & past, 1.0, sel)
            cur = jnp.where(pick, NEG_INF, cur)
        bias_t = jnp.where(sel > 0.5, 0.0, -big)
        bias_t = jnp.concatenate([bias_t, jnp.zeros((hd - n_blocks, bs), F32)], axis=0)
        qx_ref[h] = jnp.concatenate([q, jnp.transpose(bias_t).astype(BF16)], axis=1)

        s = _dot_nt(k_ref[0, pl.ds(own, bs), cols], q)
        s0_ref[h] = jnp.where(causal, s, NEG_INF)
        acc_ref[h] = jnp.zeros(acc_ref.shape[1:], F32)
        m_ref[h] = jnp.full(m_ref.shape[1:], M_INIT, F32)

    def consume(h, s_ref, v_blk):
        cols = slice(h * hd, (h + 1) * hd)
        s = s_ref[h]
        m_old = m_ref[h]
        m_new = jnp.maximum(m_old, jnp.max(s, axis=0, keepdims=True))
        p = jnp.exp2(s - m_new)
        vx = jnp.concatenate([v_ref[0, v_blk, cols, :], ones_rows], axis=0)
        acc_ref[h] = jnp.exp2(m_old - m_new) * acc_ref[h] + _dot(vx, p.astype(BF16))
        m_ref[h] = m_new

    def step(j, s_in, s_out):
        start = pl.multiple_of(j * bs, bs)
        onehot = (lax.broadcasted_iota(jnp.int32, (bs, hd), 1) == j).astype(BF16)
        v_blk = jnp.where(j == 0, i, j - 1)
        for h in range(N_HEADS):
            cols = slice(h * hd, (h + 1) * hd)
            kx = jnp.concatenate([k_ref[0, pl.ds(start, bs), cols], onehot], axis=1)
            s_out[h] = _dot_nt(kx, qx_ref[h])
            consume(h, s_in, v_blk)

    def body(j, carry):
        @pl.when(j % 2 == 0)
        def _():
            step(j, s0_ref, s1_ref)

        @pl.when(j % 2 == 1)
        def _():
            step(j, s1_ref, s0_ref)

        return carry

    lax.fori_loop(0, i, body, 0)

    def finish(s_ref):
        v_blk = jnp.where(i == 0, i, i - 1)
        for h in range(N_HEADS):
            consume(h, s_ref, v_blk)
            a = acc_ref[h]
            o_t = a[:hd, :] / a[hd:hd + 1, :]
            o_ref[0, :, h * hd:(h + 1) * hd] = jnp.transpose(o_t).astype(BF16)

    @pl.when(i % 2 == 0)
    def _():
        finish(s0_ref)

    @pl.when(i % 2 == 1)
    def _():
        finish(s1_ref)


def _attn(q, k, vt, km, qg, kg):
    batch, seq, d = q.shape
    nb = seq // MOBA_BLOCK
    tile = pl.BlockSpec((1, MOBA_BLOCK, d), lambda b, i: (b, i, 0))
    gain = pl.BlockSpec((1, HEAD_DIM), lambda b, i: (0, 0))
    return pl.pallas_call(
        functools.partial(_attn_kernel, n_blocks=nb),
        grid=(batch, nb),
        in_specs=[tile,
                  pl.BlockSpec((1, seq, d), lambda b, i: (b, 0, 0)),
                  pl.BlockSpec((1, nb, d, MOBA_BLOCK), lambda b, i: (b, 0, 0, 0)),
                  pl.BlockSpec((1, nb, d), lambda b, i: (b, 0, 0)), gain, gain],
        out_specs=tile,
        out_shape=jax.ShapeDtypeStruct((batch, seq, d), BF16),
        scratch_shapes=[
            pltpu.VMEM((N_HEADS, MOBA_BLOCK, 2 * HEAD_DIM), BF16),
            pltpu.VMEM((N_HEADS, HEAD_DIM + ONES_ROWS, MOBA_BLOCK), F32),
            pltpu.VMEM((N_HEADS, 1, MOBA_BLOCK), F32),
            pltpu.VMEM((N_HEADS, MOBA_BLOCK, MOBA_BLOCK), F32),
            pltpu.VMEM((N_HEADS, MOBA_BLOCK, MOBA_BLOCK), F32),
        ],
        compiler_params=pltpu.CompilerParams(dimension_semantics=("parallel", "arbitrary"),
                                             vmem_limit_bytes=VMEM_LIMIT),
        name="moba_attn",
    )(q, k, vt, km, qg, kg)


def _wo_router_kernel(o_ref, res_ref, wo_ref, g_ref, wr_ref, h_ref, hn_ref, route_ref, counts_ref,
                      cnt_ref, *, tm):
    h = res_ref[...] + _dot(o_ref[...], wo_ref[...])
    h_ref[...] = h
    hn = _rms(h, g_ref[...])
    hn_ref[...] = hn

    @pl.when(pl.program_id(0) == 0)
    def _():
        cnt_ref[...] = jnp.zeros_like(cnt_ref)

    x_hi, x_lo = _split_bf16(hn)
    w_hi, w_lo = _split_bf16(wr_ref[...])
    logits = _dot(x_hi, w_hi) + _dot(x_hi, w_lo) + _dot(x_lo, w_hi)
    lane = lax.broadcasted_iota(jnp.int32, (tm, N_EXPERTS), 1)
    m1 = jnp.max(logits, axis=-1, keepdims=True)
    i1 = jnp.min(jnp.where(logits == m1, lane, N_EXPERTS), axis=-1, keepdims=True)
    pick1 = lane == i1
    rest = jnp.where(pick1, NEG_INF, logits)
    m2 = jnp.max(rest, axis=-1, keepdims=True)
    i2 = jnp.min(jnp.where(rest == m2, lane, N_EXPERTS), axis=-1, keepdims=True)
    pick2 = lane == i2
    e2 = jnp.exp(m2 - m1)
    denom = 1.0 + e2

    chosen = jnp.where(pick1 | pick2, 1.0, 0.0)
    tr = lax.broadcasted_iota(jnp.int32, (tm, tm), 0)
    tc = lax.broadcasted_iota(jnp.int32, (tm, tm), 1)
    earlier = jnp.where(tc < tr, 1.0, 0.0).astype(BF16)
    rank = _dot(earlier, chosen.astype(BF16)) + cnt_ref[...]
    cnt_ref[...] += jnp.sum(chosen, axis=0, keepdims=True)
    counts_ref[...] = cnt_ref[...]
    rank1 = jnp.sum(jnp.where(pick1, rank, 0.0), axis=-1, keepdims=True)
    rank2 = jnp.sum(jnp.where(pick2, rank, 0.0), axis=-1, keepdims=True)
    fields = [i1.astype(F32), i2.astype(F32), 1.0 / denom, e2 / denom, rank1, rank2]
    route = jnp.zeros((tm, N_EXPERTS), F32)
    for slot, val in enumerate(fields):
        route = jnp.where(lane == slot, val, route)
    route_ref[...] = route


def _wo_router(o, res, w_o, g, w_r, *, tm=512):
    t, d = res.shape
    const = lambda *shape: pl.BlockSpec(shape, lambda i: (0,) * len(shape))
    row = lambda width: pl.BlockSpec((tm, width), lambda i: (i, 0))
    return pl.pallas_call(
        functools.partial(_wo_router_kernel, tm=tm),
        grid=(t // tm,),
        in_specs=[row(d), row(d), const(d, d), const(1, d), const(d, N_EXPERTS)],
        out_specs=[row(d), row(d), row(N_EXPERTS), const(1, N_EXPERTS)],
        out_shape=[jax.ShapeDtypeStruct((t, d), F32), jax.ShapeDtypeStruct((t, d), F32),
                   jax.ShapeDtypeStruct((t, N_EXPERTS), F32),
                   jax.ShapeDtypeStruct((1, N_EXPERTS), F32)],
        scratch_shapes=[pltpu.VMEM((1, N_EXPERTS), F32)],
        compiler_params=pltpu.CompilerParams(dimension_semantics=("arbitrary",),
                                             vmem_limit_bytes=VMEM_LIMIT),
        name="wo_router",
    )(o, res, w_o, g, w_r)


def _row_copy(src_ref, src_row, dst_ref, dst_row, sem):
    return pltpu.make_async_copy(src_ref.at[pl.ds(src_row, 1), :], dst_ref.at[pl.ds(dst_row, 1), :], sem)


def _moe_scatter_kernel(meta_ref, pos_ref, x_ref, xs_ref, zero_ref, sem, zero_sem, *, tm, tg):
    zr = zero_ref.shape[0]
    n_rows = xs_ref.shape[0]
    total = meta_ref[N_EXPERTS - 1]

    def zero_tile(base):
        base = pl.multiple_of(base, tg)
        return [pltpu.make_async_copy(zero_ref, xs_ref.at[pl.ds(base + c * zr, zr), :], zero_sem)
                for c in range(tg // zr)]

    @pl.when(pl.program_id(0) == 0)
    def _():
        zero_ref[...] = jnp.zeros_like(zero_ref)
        for wait in (False, True):
            for e in range(N_EXPERTS):
                @pl.when(meta_ref[N_EXPERTS + e] > 0)
                def _():
                    for cp in zero_tile(meta_ref[e] - tg):
                        cp.wait() if wait else cp.start()
            for k in range(N_EXPERTS - 1):
                @pl.when(total + k * tg < n_rows)
                def _():
                    for cp in zero_tile(total + k * tg):
                        cp.wait() if wait else cp.start()

    def start(r, carry):
        _row_copy(x_ref, r, xs_ref, pos_ref[0, 0, 2 * r], sem).start(priority=0)
        _row_copy(x_ref, r, xs_ref, pos_ref[0, 0, 2 * r + 1], sem).start(priority=1)
        return carry

    lax.fori_loop(0, tm, start, 0, unroll=8)

    def wait(r, carry):
        _row_copy(x_ref, 0, xs_ref, 0, sem).wait()
        _row_copy(x_ref, 0, xs_ref, 0, sem).wait()
        return carry

    lax.fori_loop(0, tm, wait, 0, unroll=8)


def _moe_scatter(meta, pos, x, n_rows, *, tg, tm=256, zero_rows=256):
    t, d = x.shape
    return pl.pallas_call(
        functools.partial(_moe_scatter_kernel, tm=tm, tg=tg),
        grid_spec=pltpu.PrefetchScalarGridSpec(
            num_scalar_prefetch=1,
            grid=(t // tm,),
            in_specs=[pl.BlockSpec((1, 1, 2 * tm), lambda i, meta: (i, 0, 0), memory_space=pltpu.SMEM),
                      pl.BlockSpec((tm, d), lambda i, meta: (i, 0))],
            out_specs=pl.BlockSpec(memory_space=pl.ANY),
            scratch_shapes=[pltpu.VMEM((zero_rows, d), F32), pltpu.SemaphoreType.DMA(()),
                            pltpu.SemaphoreType.DMA(())]),
        out_shape=jax.ShapeDtypeStruct((n_rows, d), F32),
        compiler_params=pltpu.CompilerParams(dimension_semantics=("arbitrary",),
                                             vmem_limit_bytes=VMEM_LIMIT),
        name="moe_scatter",
    )(meta, pos.reshape(t // tm, 1, 2 * tm), x)


def _moe_gemm_kernel(te_ref, used_ref, x_ref, wg_ref, wu_ref, wd_ref, y_ref, xb_ref):
    del te_ref
    f = pl.program_id(1)
    live = pl.program_id(0) < used_ref[0]

    @pl.when(jnp.logical_not(live) & (f == 0))
    def _():
        y_ref[...] = jnp.zeros_like(y_ref)

    @pl.when(live)
    def _():
        @pl.when(f == 0)
        def _():
            xb_ref[...] = x_ref[...].astype(BF16)

        y = _swiglu(xb_ref[...], wg_ref[0], wu_ref[0], wd_ref[0])

        @pl.when(f == 0)
        def _():
            y_ref[...] = y

        @pl.when(f > 0)
        def _():
            y_ref[...] += y


def _moe_gemm(tile_expert, n_used, xs, w_gate, w_up, w_down, *, tg, tf):
    n_rows, d = xs.shape
    f_dim = w_gate.shape[-1]
    x_row = pl.BlockSpec((tg, d), lambda i, f, te, used: (jnp.minimum(i, used[0] - 1), 0))
    y_row = pl.BlockSpec((tg, d), lambda i, f, te, used: (i, 0))
    w_tile = lambda i, f, used: jnp.where(i < used[0], f, f_dim // tf - 1)
    return pl.pallas_call(
        _moe_gemm_kernel,
        grid_spec=pltpu.PrefetchScalarGridSpec(
            num_scalar_prefetch=2,
            grid=(n_rows // tg, f_dim // tf),
            in_specs=[x_row,
                      pl.BlockSpec((1, d, tf), lambda i, f, te, used: (te[i], 0, w_tile(i, f, used))),
                      pl.BlockSpec((1, d, tf), lambda i, f, te, used: (te[i], 0, w_tile(i, f, used))),
                      pl.BlockSpec((1, tf, d), lambda i, f, te, used: (te[i], w_tile(i, f, used), 0))],
            out_specs=y_row,
            scratch_shapes=[pltpu.VMEM((tg, d), BF16)]),
        out_shape=jax.ShapeDtypeStruct((n_rows, d), F32),
        compiler_params=pltpu.CompilerParams(dimension_semantics=("arbitrary", "arbitrary"),
                                             vmem_limit_bytes=VMEM_LIMIT),
        name="moe_gemm",
    )(tile_expert, n_used, xs, w_gate, w_up, w_down)


def _moe_combine_kernel(pos_ref, h_ref, route_ref, ys_ref, o_ref, buf_ref, sem, *, tm):
    def start(r, carry):
        _row_copy(ys_ref, pos_ref[0, 0, 2 * r], buf_ref.at[0], r, sem).start(priority=0)
        _row_copy(ys_ref, pos_ref[0, 0, 2 * r + 1], buf_ref.at[1], r, sem).start(priority=1)
        return carry

    lax.fori_loop(0, tm, start, 0, unroll=8)

    def wait(r, carry):
        _row_copy(ys_ref, 0, buf_ref.at[0], 0, sem).wait()
        _row_copy(ys_ref, 0, buf_ref.at[1], 0, sem).wait()
        return carry

    lax.fori_loop(0, tm, wait, 0, unroll=8)
    route = route_ref[...]
    o_ref[...] = h_ref[...] + route[:, 2:3] * buf_ref[0] + route[:, 3:4] * buf_ref[1]


def _moe_combine(pos, h, route, ys, *, tm=256):
    t, d = h.shape
    return pl.pallas_call(
        functools.partial(_moe_combine_kernel, tm=tm),
        grid=(t // tm,),
        in_specs=[pl.BlockSpec((1, 1, 2 * tm), lambda i: (i, 0, 0), memory_space=pltpu.SMEM),
                  pl.BlockSpec((tm, d), lambda i: (i, 0)),
                  pl.BlockSpec((tm, N_EXPERTS), lambda i: (i, 0)),
                  pl.BlockSpec(memory_space=pl.ANY)],
        out_specs=pl.BlockSpec((tm, d), lambda i: (i, 0)),
        out_shape=jax.ShapeDtypeStruct((t, d), F32),
        scratch_shapes=[pltpu.VMEM((2, tm, d), F32), pltpu.SemaphoreType.DMA(())],
        compiler_params=pltpu.CompilerParams(dimension_semantics=("arbitrary",),
                                             vmem_limit_bytes=VMEM_LIMIT),
        name="moe_combine",
    )(pos.reshape(t // tm, 1, 2 * tm), h, route, ys)


def _moe(h, hn, route, counts, w_gate, w_up, w_down, *, tg=1024, tf=512):
    t, d = h.shape
    n_tiles = (2 * t + N_EXPERTS * (tg - 1)) // tg
    counts = counts[0].astype(jnp.int32)
    padded = (counts + tg - 1) // tg * tg
    e_ids = jnp.arange(N_EXPERTS, dtype=jnp.int32)
    ends = jnp.sum(jnp.where(e_ids[:, None] <= e_ids[None, :], padded[:, None], 0), axis=0)
    starts = ends - padded
    experts = route[:, 0:2].astype(jnp.int32)
    start_of = jnp.sum(jnp.where(experts[:, :, None] == e_ids, starts, 0), axis=-1)
    pos = start_of + route[:, 4:6].astype(jnp.int32)
    n_used = ends[-1:] // tg
    tile_start = jnp.minimum(jnp.arange(n_tiles, dtype=jnp.int32), n_used - 1) * tg
    tile_expert = jnp.sum((ends[None, :] <= tile_start[:, None]).astype(jnp.int32), axis=1)
    xs = _moe_scatter(jnp.concatenate([ends, padded]), pos, hn, n_tiles * tg, tg=tg)
    ys = _moe_gemm(tile_expert, n_used, xs, w_gate, w_up, w_down, tg=tg, tf=tf)
    return _moe_combine(pos, h, route, ys)


def _rope_tables(seq):
    half = ROT_DIM // 2
    inv_freq = ROPE_THETA ** (-jnp.arange(half, dtype=F32) / half)
    ang = jnp.arange(seq, dtype=F32)[:, None] * inv_freq[None, :]
    cos = jnp.cos(ang)
    sin = jnp.sin(ang)
    rest = HEAD_DIM - ROT_DIM
    cos_full = jnp.concatenate([cos, cos, jnp.ones((seq, rest), F32)], axis=-1)
    sin_signed = jnp.concatenate([-sin, sin, jnp.zeros((seq, rest), F32)], axis=-1)
    return cos_full, sin_signed


def kernel(x, e_mix_norm, e_w_in, e_conv_w, e_gmlp_ln_g, e_gmlp_ln_b, e_w_spatial, e_b_spatial, e_w_out, e_ffn_norm, e_w_gate, e_w_up, e_w_down, o_mix_norm, o_w_qkv, o_q_norm, o_k_norm, o_w_o, o_ffn_norm, o_w_router, o_w_gate, o_w_up, o_w_down):
    batch, seq, d = x.shape
    t = batch * seq
    xf = x.reshape(t, d)

    bias_full = jnp.repeat(jnp.transpose(e_b_spatial[0]), GMLP_GROUP_DIM, axis=1)
    h1, hn1 = _mixer0(
        xf, e_mix_norm[0][None, :], e_w_in[0].astype(BF16), e_conv_w[0],
        e_gmlp_ln_g[0][None, :], e_gmlp_ln_b[0][None, :], e_w_spatial[0], bias_full,
        e_w_out[0].astype(BF16), e_ffn_norm[0][None, :], seq=seq)
    h2 = _ffn(hn1, h1, e_w_gate[0].astype(BF16), e_w_up[0].astype(BF16), e_w_down[0].astype(BF16),
              tm=512, tf=1408)

    cos, sin = _rope_tables(seq)
    qg = o_q_norm[0][None, :]
    kg = o_k_norm[0][None, :]
    q, k, vt, km = _qkv(h2, o_mix_norm[0][None, :], o_w_qkv[0].astype(BF16), qg, kg, cos, sin, seq=seq)
    as_seq = lambda a: a.reshape(batch, -1, d)
    o = _attn(as_seq(q), as_seq(k), vt.reshape(batch, seq // MOBA_BLOCK, d, MOBA_BLOCK),
              as_seq(km), qg, kg)
    h3, hn3, route, counts = _wo_router(o.reshape(t, d), h2, o_w_o[0].astype(BF16),
                                        o_ffn_norm[0][None, :], o_w_router[0])
    out = _moe(h3, hn3, route, counts, o_w_gate[0], o_w_up[0], o_w_down[0])
    return out.reshape(batch, seq, d)
```

```python
import functools
import math

import jax
import jax.numpy as jnp
from jax import lax
from jax.experimental import pallas as pl
from jax.experimental.pallas import tpu as pltpu

F32 = jnp.float32
BF16 = jnp.bfloat16

D_MODEL = 1024
CONV_DIM = 512
CONV_KERNEL = 3
GMLP_DIM = 512
GMLP_GROUPS = 8
GMLP_GROUP_DIM = GMLP_DIM // GMLP_GROUPS
GMLP_CHUNK = 128
MIX_IN = 3 * CONV_DIM + 2 * GMLP_DIM
N_HEADS = 8
HEAD_DIM = 128
ROT_DIM = HEAD_DIM // 4
ROPE_THETA = 500000.0
MOBA_BLOCK = 256
MOBA_TOPK = 3
N_EXPERTS = 8
EPS = 1e-6

V7X_LANES = 128
ROW_TILE = 8
VMEM_LIMIT = 56 * 1024 * 1024

NEG_INF = float("-inf")
QK_SCALE = HEAD_DIM ** -0.5 * math.log2(math.e)
MASK_MARGIN = 256.0
M_INIT = -1e30
ONES_ROWS = 16


def _rms(x, g):
    return x * lax.rsqrt(jnp.mean(x * x, axis=-1, keepdims=True) + EPS) * g


def _dot(a, b):
    return jnp.dot(a, b, preferred_element_type=F32)


def _dot_nt(a, b):
    return lax.dot_general(a, b, (((1,), (1,)), ((), ())), preferred_element_type=F32)


def _split_bf16(x):
    hi = x.astype(BF16)
    lo = (x - hi.astype(F32)).astype(BF16)
    return hi, lo


def _mixer0_kernel(x_ref, g_ref, win_ref, convw_ref, lng_ref, lnb_ref, ws_ref, bias_ref,
                   wout_ref, g2_ref, h_ref, hn_ref, carry_ref, y_ref, *, tm, tiles_per_seq):
    i = pl.program_id(0)
    x = x_ref[...]
    hn = _rms(x, g_ref[...]).astype(BF16)
    proj = _dot(hn, win_ref[...])
    a_h = proj[:, 0:CONV_DIM]
    a_c = proj[:, CONV_DIM:2 * CONV_DIM]
    a_b = proj[:, 2 * CONV_DIM:3 * CONV_DIM]
    g_u = proj[:, 3 * CONV_DIM:3 * CONV_DIM + GMLP_DIM]
    g_v = proj[:, 3 * CONV_DIM + GMLP_DIM:]

    z = a_c * a_h

    @pl.when(i % tiles_per_seq == 0)
    def _():
        carry_ref[...] = jnp.zeros_like(carry_ref)

    prev = carry_ref[...]
    row = lax.broadcasted_iota(jnp.int32, (tm, CONV_DIM), 0)
    z1 = jnp.where(row == 0, prev[7:8, :], pltpu.roll(z, 1, 0))
    z2 = jnp.where(row == 0, prev[6:7, :], jnp.where(row == 1, prev[7:8, :], pltpu.roll(z, 2, 0)))
    carry_ref[...] = z[tm - 8:tm, :]
    cw = convw_ref[...]
    y_a = a_b * (cw[0:1, :] * z2 + cw[1:2, :] * z1 + cw[2:3, :] * z)
    y_ref[:, 0:CONV_DIM] = y_a.astype(BF16)

    mu = jnp.mean(g_v, axis=-1, keepdims=True)
    cen = g_v - mu
    var = jnp.mean(cen * cen, axis=-1, keepdims=True)
    v = cen * lax.rsqrt(var + EPS) * lng_ref[...] + lnb_ref[...]
    tr = lax.broadcasted_iota(jnp.int32, (GMLP_CHUNK, GMLP_CHUNK), 0)
    tc = lax.broadcasted_iota(jnp.int32, (GMLP_CHUNK, GMLP_CHUNK), 1)
    causal = tc <= tr
    lo_half = tc < GMLP_GROUP_DIM
    bias = bias_ref[...]
    for c in range(tm // GMLP_CHUNK):
        rows = slice(c * GMLP_CHUNK, (c + 1) * GMLP_CHUNK)
        pieces = []
        for j in range(GMLP_DIM // V7X_LANES):
            vj = v[rows, j * V7X_LANES:(j + 1) * V7X_LANES]
            v_lo = jnp.where(lo_half, vj, 0.0).astype(BF16)
            v_hi = jnp.where(lo_half, 0.0, vj).astype(BF16)
            w_lo = jnp.where(causal, ws_ref[2 * j], 0.0).astype(BF16)
            w_hi = jnp.where(causal, ws_ref[2 * j + 1], 0.0).astype(BF16)
            pieces.append(_dot(w_lo, v_lo) + _dot(w_hi, v_hi))
        mixed = jnp.concatenate(pieces, axis=1) + bias
        y_ref[rows, CONV_DIM:] = (g_u[rows, :] * mixed).astype(BF16)

    h = x + _dot(y_ref[...], wout_ref[...])
    h_ref[...] = h
    hn_ref[...] = _rms(h, g2_ref[...]).astype(BF16)


def _mixer0(x, g, w_in, conv_w, ln_g, ln_b, w_s, bias_full, w_out, g2, *, seq, tm=512):
    t, d = x.shape
    const = lambda *shape: pl.BlockSpec(shape, lambda i: (0,) * len(shape))
    return pl.pallas_call(
        functools.partial(_mixer0_kernel, tm=tm, tiles_per_seq=seq // tm),
        grid=(t // tm,),
        in_specs=[
            pl.BlockSpec((tm, d), lambda i: (i, 0)),
            const(1, d),
            const(d, MIX_IN),
            const(CONV_KERNEL, CONV_DIM),
            const(1, GMLP_DIM),
            const(1, GMLP_DIM),
            const(GMLP_GROUPS, GMLP_CHUNK, GMLP_CHUNK),
            const(GMLP_CHUNK, GMLP_DIM),
            const(CONV_DIM + GMLP_DIM, d),
            const(1, d),
        ],
        out_specs=[pl.BlockSpec((tm, d), lambda i: (i, 0)), pl.BlockSpec((tm, d), lambda i: (i, 0))],
        out_shape=[jax.ShapeDtypeStruct((t, d), F32), jax.ShapeDtypeStruct((t, d), BF16)],
        scratch_shapes=[pltpu.VMEM((8, CONV_DIM), F32), pltpu.VMEM((tm, CONV_DIM + GMLP_DIM), BF16)],
        compiler_params=pltpu.CompilerParams(dimension_semantics=("arbitrary",),
                                             vmem_limit_bytes=VMEM_LIMIT),
        name="mixer0",
    )(x, g, w_in, conv_w, ln_g, ln_b, w_s, bias_full, w_out, g2)


def _swiglu(x, wg, wu, wd):
    a = _dot(x, wg.astype(BF16))
    u = _dot(x, wu.astype(BF16))
    hact = (a * jax.nn.sigmoid(a) * u).astype(BF16)
    return _dot(hact, wd.astype(BF16))


def _ffn_kernel(x_ref, res_ref, wg_ref, wu_ref, wd_ref, o_ref):
    @pl.when(pl.program_id(1) == 0)
    def _():
        o_ref[...] = res_ref[...]

    o_ref[...] += _swiglu(x_ref[...], wg_ref[...], wu_ref[...], wd_ref[...])


def _ffn(xn, res, w_gate, w_up, w_down, *, tm, tf):
    t, d = xn.shape
    f_dim = w_gate.shape[1]
    row = pl.BlockSpec((tm, d), lambda i, f: (i, 0))
    return pl.pallas_call(
        _ffn_kernel,
        grid=(t // tm, f_dim // tf),
        in_specs=[row, row,
                  pl.BlockSpec((d, tf), lambda i, f: (0, f)),
                  pl.BlockSpec((d, tf), lambda i, f: (0, f)),
                  pl.BlockSpec((tf, d), lambda i, f: (f, 0))],
        out_specs=row,
        out_shape=jax.ShapeDtypeStruct((t, d), F32),
        compiler_params=pltpu.CompilerParams(dimension_semantics=("parallel", "arbitrary"),
                                             vmem_limit_bytes=VMEM_LIMIT),
        name="dense_ffn",
    )(xn, res, w_gate, w_up, w_down)


def _qkv_kernel(h_ref, g_ref, w_ref, qg_ref, kg_ref, cos_ref, sin_ref,
                q_ref, k_ref, v_ref, km_ref, *, tm):
    hn = _rms(h_ref[...], g_ref[...]).astype(BF16)
    cos = cos_ref[...]
    sin = sin_ref[...]
    lane = lax.broadcasted_iota(jnp.int32, (tm, HEAD_DIM), 1)
    first = lane < ROT_DIM // 2

    def rope(x):
        partner = jnp.where(first, pltpu.roll(x, HEAD_DIM - ROT_DIM // 2, 1),
                            pltpu.roll(x, ROT_DIM // 2, 1))
        return x * cos + partner * sin

    hd = N_HEADS * HEAD_DIM
    qkv = _dot(hn, w_ref[...])
    for c in range(tm // MOBA_BLOCK):
        v_ref[c] = jnp.transpose(qkv[c * MOBA_BLOCK:(c + 1) * MOBA_BLOCK, 2 * hd:]).astype(BF16)
    for h in range(N_HEADS):
        cols = slice(h * HEAD_DIM, (h + 1) * HEAD_DIM)
        q = rope(_rms(qkv[:, cols], qg_ref[...]))
        k = rope(_rms(qkv[:, hd + h * HEAD_DIM:hd + (h + 1) * HEAD_DIM], kg_ref[...]))
        q_ref[:, cols] = (q * QK_SCALE).astype(BF16)
        k_ref[:, cols] = k.astype(BF16)
        for c in range(tm // MOBA_BLOCK):
            km_ref[c, :, cols] = jnp.mean(
                k[c * MOBA_BLOCK:(c + 1) * MOBA_BLOCK, :], axis=0, keepdims=True)


def _qkv(h, g, w, qg, kg, cos, sin, *, seq, tm=512):
    t, d = h.shape
    ns = seq // tm
    hd = N_HEADS * HEAD_DIM
    const = lambda *shape: pl.BlockSpec(shape, lambda i: (0,) * len(shape))
    row_spec = pl.BlockSpec((tm, hd), lambda i: (i, 0))
    row_shape = jax.ShapeDtypeStruct((t, hd), BF16)
    return pl.pallas_call(
        functools.partial(_qkv_kernel, tm=tm),
        grid=(t // tm,),
        in_specs=[
            pl.BlockSpec((tm, d), lambda i: (i, 0)),
            const(1, d),
            const(d, 3 * hd),
            const(1, HEAD_DIM),
            const(1, HEAD_DIM),
            pl.BlockSpec((tm, HEAD_DIM), lambda i: (i % ns, 0)),
            pl.BlockSpec((tm, HEAD_DIM), lambda i: (i % ns, 0)),
        ],
        out_specs=[row_spec, row_spec,
                   pl.BlockSpec((tm // MOBA_BLOCK, hd, MOBA_BLOCK), lambda i: (i, 0, 0)),
                   pl.BlockSpec((tm // MOBA_BLOCK, 1, hd), lambda i: (i, 0, 0))],
        out_shape=[row_shape, row_shape,
                   jax.ShapeDtypeStruct((t // MOBA_BLOCK, hd, MOBA_BLOCK), BF16),
                   jax.ShapeDtypeStruct((t // MOBA_BLOCK, 1, hd), F32)],
        compiler_params=pltpu.CompilerParams(dimension_semantics=("parallel",),
                                             vmem_limit_bytes=VMEM_LIMIT),
        name="qkv",
    )(h, g, w, qg, kg, cos, sin)


def _attn_kernel(q_ref, k_ref, v_ref, km_ref, qg_ref, kg_ref, o_ref, qx_ref, acc_ref, m_ref,
                 s0_ref, s1_ref, *, n_blocks):
    i = pl.program_id(1)
    bs = MOBA_BLOCK
    hd = HEAD_DIM
    bound = QK_SCALE * hd * jnp.max(jnp.abs(qg_ref[...]), axis=-1, keepdims=True) * jnp.max(
        jnp.abs(kg_ref[...]), axis=-1, keepdims=True)
    big = 2.1 * bound + MASK_MARGIN
    blk = lax.broadcasted_iota(jnp.int32, (n_blocks, bs), 0)
    past = blk < i
    key = lax.broadcasted_iota(jnp.int32, (bs, bs), 0)
    qry = lax.broadcasted_iota(jnp.int32, (bs, bs), 1)
    causal = key <= qry
    ones_rows = jnp.ones((ONES_ROWS, bs), BF16)
    own = pl.multiple_of(i * bs, bs)

    for h in range(N_HEADS):
        cols = slice(h * hd, (h + 1) * hd)
        q = q_ref[0, :, cols]
        km_hi, km_lo = _split_bf16(km_ref[0, :, cols])
        gate_t = _dot_nt(km_hi, q) + _dot_nt(km_lo, q)
        cur = jnp.where(past, gate_t, NEG_INF)
        sel = jnp.zeros((n_blocks, bs), F32)
        for _ in range(min(MOBA_TOPK, n_blocks)):
            top = jnp.max(cur, axis=0, keepdims=True)
            idx = jnp.min(jnp.where(cur == top, blk, n_blocks), axis=0, keepdims=True)
            pick = blk == idx
            sel = jnp.where(pick & past, 1.0, sel)
            cur = jnp.where(pick, NEG_INF, cur)
        bias_t = jnp.where(sel > 0.5, 0.0, -big)
        bias_t = jnp.concatenate([bias_t, jnp.zeros((hd - n_blocks, bs), F32)], axis=0)
        qx_ref[h] = jnp.concatenate([q, jnp.transpose(bias_t).astype(BF16)], axis=1)

        s = _dot_nt(k_ref[0, pl.ds(own, bs), cols], q)
        s0_ref[h] = jnp.where(causal, s, NEG_INF)
        acc_ref[h] = jnp.zeros(acc_ref.shape[1:], F32)
        m_ref[h] = jnp.full(m_ref.shape[1:], M_INIT, F32)

    def consume(h, s_ref, v_blk):
        cols = slice(h * hd, (h + 1) * hd)
        s = s_ref[h]
        m_old = m_ref[h]
        m_new = jnp.maximum(m_old, jnp.max(s, axis=0, keepdims=True))
        p = jnp.exp2(s - m_new)
        vx = jnp.concatenate([v_ref[0, v_blk, cols, :], ones_rows], axis=0)
        acc_ref[h] = jnp.exp2(m_old - m_new) * acc_ref[h] + _dot(vx, p.astype(BF16))
        m_ref[h] = m_new

    def step(j, s_in, s_out):
        start = pl.multiple_of(j * bs, bs)
        onehot = (lax.broadcasted_iota(jnp.int32, (bs, hd), 1) == j).astype(BF16)
        v_blk = jnp.where(j == 0, i, j - 1)
        for h in range(N_HEADS):
            cols = slice(h * hd, (h + 1) * hd)
            kx = jnp.concatenate([k_ref[0, pl.ds(start, bs), cols], onehot], axis=1)
            s_out[h] = _dot_nt(kx, qx_ref[h])
            consume(h, s_in, v_blk)

    def body(j, carry):
        @pl.when(j % 2 == 0)
        def _():
            step(j, s0_ref, s1_ref)

        @pl.when(j % 2 == 1)
        def _():
            step(j, s1_ref, s0_ref)

        return carry

    lax.fori_loop(0, i, body, 0)

    def finish(s_ref):
        v_blk = jnp.where(i == 0, i, i - 1)
        for h in range(N_HEADS):
            consume(h, s_ref, v_blk)
            a = acc_ref[h]
            o_t = a[:hd, :] / a[hd:hd + 1, :]
            o_ref[0, :, h * hd:(h + 1) * hd] = jnp.transpose(o_t).astype(BF16)

    @pl.when(i % 2 == 0)
    def _():
        finish(s0_ref)

    @pl.when(i % 2 == 1)
    def _():
        finish(s1_ref)


def _attn(q, k, vt, km, qg, kg):
    batch, seq, d = q.shape
    nb = seq // MOBA_BLOCK
    tile = pl.BlockSpec((1, MOBA_BLOCK, d), lambda b, i: (b, i, 0))
    gain = pl.BlockSpec((1, HEAD_DIM), lambda b, i: (0, 0))
    return pl.pallas_call(
        functools.partial(_attn_kernel, n_blocks=nb),
        grid=(batch, nb),
        in_specs=[tile,
                  pl.BlockSpec((1, seq, d), lambda b, i: (b, 0, 0)),
                  pl.BlockSpec((1, nb, d, MOBA_BLOCK), lambda b, i: (b, 0, 0, 0)),
                  pl.BlockSpec((1, nb, d), lambda b, i: (b, 0, 0)), gain, gain],
        out_specs=tile,
        out_shape=jax.ShapeDtypeStruct((batch, seq, d), BF16),
        scratch_shapes=[
            pltpu.VMEM((N_HEADS, MOBA_BLOCK, 2 * HEAD_DIM), BF16),
            pltpu.VMEM((N_HEADS, HEAD_DIM + ONES_ROWS, MOBA_BLOCK), F32),
            pltpu.VMEM((N_HEADS, 1, MOBA_BLOCK), F32),
            pltpu.VMEM((N_HEADS, MOBA_BLOCK, MOBA_BLOCK), F32),
            pltpu.VMEM((N_HEADS, MOBA_BLOCK, MOBA_BLOCK), F32),
        ],
        compiler_params=pltpu.CompilerParams(dimension_semantics=("parallel", "arbitrary"),
                                             vmem_limit_bytes=VMEM_LIMIT),
        name="moba_attn",
    )(q, k, vt, km, qg, kg)


def _wo_router_kernel(o_ref, res_ref, wo_ref, g_ref, wr_ref, h_ref, hn_ref, route_ref, counts_ref,
                      cnt_ref, *, tm):
    h = res_ref[...] + _dot(o_ref[...], wo_ref[...])
    h_ref[...] = h
    hn = _rms(h, g_ref[...])
    _store_rows_as_tiles(hn_ref, hn)

    @pl.when(pl.program_id(0) == 0)
    def _():
        cnt_ref[...] = jnp.zeros_like(cnt_ref)

    x_hi, x_lo = _split_bf16(hn)
    w_hi, w_lo = _split_bf16(wr_ref[...])
    logits = _dot(x_hi, w_hi) + _dot(x_hi, w_lo) + _dot(x_lo, w_hi)
    lane = lax.broadcasted_iota(jnp.int32, (tm, N_EXPERTS), 1)
    m1 = jnp.max(logits, axis=-1, keepdims=True)
    i1 = jnp.min(jnp.where(logits == m1, lane, N_EXPERTS), axis=-1, keepdims=True)
    pick1 = lane == i1
    rest = jnp.where(pick1, NEG_INF, logits)
    m2 = jnp.max(rest, axis=-1, keepdims=True)
    i2 = jnp.min(jnp.where(rest == m2, lane, N_EXPERTS), axis=-1, keepdims=True)
    pick2 = lane == i2
    e2 = jnp.exp(m2 - m1)
    denom = 1.0 + e2

    chosen = jnp.where(pick1 | pick2, 1.0, 0.0)
    tr = lax.broadcasted_iota(jnp.int32, (tm, tm), 0)
    tc = lax.broadcasted_iota(jnp.int32, (tm, tm), 1)
    earlier = jnp.where(tc < tr, 1.0, 0.0).astype(BF16)
    rank = _dot(earlier, chosen.astype(BF16)) + cnt_ref[...]
    cnt_ref[...] += jnp.sum(chosen, axis=0, keepdims=True)
    counts_ref[...] = cnt_ref[...]
    rank1 = jnp.sum(jnp.where(pick1, rank, 0.0), axis=-1, keepdims=True)
    rank2 = jnp.sum(jnp.where(pick2, rank, 0.0), axis=-1, keepdims=True)
    fields = [i1.astype(F32), i2.astype(F32), 1.0 / denom, e2 / denom, rank1, rank2]
    route = jnp.zeros((tm, N_EXPERTS), F32)
    for slot, val in enumerate(fields):
        route = jnp.where(lane == slot, val, route)
    route_ref[...] = route


def _wo_router(o, res, w_o, g, w_r, *, tm=512):
    t, d = res.shape
    const = lambda *shape: pl.BlockSpec(shape, lambda i: (0,) * len(shape))
    row = lambda width: pl.BlockSpec((tm, width), lambda i: (i, 0))
    return pl.pallas_call(
        functools.partial(_wo_router_kernel, tm=tm),
        grid=(t // tm,),
        in_specs=[row(d), row(d), const(d, d), const(1, d), const(d, N_EXPERTS)],
        out_specs=[row(d), pl.BlockSpec((tm * ROW_TILE, V7X_LANES), lambda i: (i, 0)),
                   row(N_EXPERTS), const(1, N_EXPERTS)],
        out_shape=[jax.ShapeDtypeStruct((t, d), F32),
                   jax.ShapeDtypeStruct((t * ROW_TILE, V7X_LANES), F32),
                   jax.ShapeDtypeStruct((t, N_EXPERTS), F32),
                   jax.ShapeDtypeStruct((1, N_EXPERTS), F32)],
        scratch_shapes=[pltpu.VMEM((1, N_EXPERTS), F32)],
        compiler_params=pltpu.CompilerParams(dimension_semantics=("arbitrary",),
                                             vmem_limit_bytes=VMEM_LIMIT),
        name="wo_router",
    )(o, res, w_o, g, w_r)


def _store_rows_as_tiles(dst_ref, x):
    n = x.shape[0]
    for s in range(ROW_TILE):
        dst_ref[pl.ds(s, n, stride=ROW_TILE), :] = x[:, s * V7X_LANES:(s + 1) * V7X_LANES]


def _load_lane_tile(src_ref, n, s):
    return src_ref[pl.ds(s, n, stride=ROW_TILE), :]


def _row_copy(src_ref, src_row, dst_ref, dst_row, sem):
    src = src_ref.at[pl.ds(pl.multiple_of(src_row * ROW_TILE, ROW_TILE), ROW_TILE), :]
    dst = dst_ref.at[pl.ds(pl.multiple_of(dst_row * ROW_TILE, ROW_TILE), ROW_TILE), :]
    return pltpu.make_async_copy(src, dst, sem)


def _moe_scatter_kernel(meta_ref, pos_ref, x_ref, xs_ref, zero_ref, sem, zero_sem, *, tm, tg):
    zr = zero_ref.shape[0] // ROW_TILE
    n_rows = xs_ref.shape[0] // ROW_TILE
    total = meta_ref[N_EXPERTS - 1]

    def zero_tile(base):
        base = pl.multiple_of(base * ROW_TILE, tg * ROW_TILE)
        return [pltpu.make_async_copy(
            zero_ref, xs_ref.at[pl.ds(base + c * zr * ROW_TILE, zr * ROW_TILE), :], zero_sem)
            for c in range(tg // zr)]

    @pl.when(pl.program_id(0) == 0)
    def _():
        zero_ref[...] = jnp.zeros_like(zero_ref)
        for wait in (False, True):
            for e in range(N_EXPERTS):
                @pl.when(meta_ref[N_EXPERTS + e] > 0)
                def _():
                    for cp in zero_tile(meta_ref[e] - tg):
                        cp.wait() if wait else cp.start()
            for k in range(N_EXPERTS - 1):
                @pl.when(total + k * tg < n_rows)
                def _():
                    for cp in zero_tile(total + k * tg):
                        cp.wait() if wait else cp.start()

    def start(r, carry):
        _row_copy(x_ref, r, xs_ref, pos_ref[0, 0, 2 * r], sem).start(priority=0)
        _row_copy(x_ref, r, xs_ref, pos_ref[0, 0, 2 * r + 1], sem).start(priority=1)
        return carry

    lax.fori_loop(0, tm, start, 0, unroll=8)

    def wait(r, carry):
        _row_copy(x_ref, 0, xs_ref, 0, sem).wait()
        _row_copy(x_ref, 0, xs_ref, 0, sem).wait()
        return carry

    lax.fori_loop(0, tm, wait, 0, unroll=8)


def _moe_scatter(meta, pos, x, n_rows, *, tg, tm=256, zero_rows=256):
    t = x.shape[0] // ROW_TILE
    return pl.pallas_call(
        functools.partial(_moe_scatter_kernel, tm=tm, tg=tg),
        grid_spec=pltpu.PrefetchScalarGridSpec(
            num_scalar_prefetch=1,
            grid=(t // tm,),
            in_specs=[pl.BlockSpec((1, 1, 2 * tm), lambda i, meta: (i, 0, 0), memory_space=pltpu.SMEM),
                      pl.BlockSpec((tm * ROW_TILE, V7X_LANES), lambda i, meta: (i, 0))],
            out_specs=pl.BlockSpec(memory_space=pl.ANY),
            scratch_shapes=[pltpu.VMEM((zero_rows * ROW_TILE, V7X_LANES), F32),
                            pltpu.SemaphoreType.DMA(()), pltpu.SemaphoreType.DMA(())]),
        out_shape=jax.ShapeDtypeStruct((n_rows * ROW_TILE, V7X_LANES), F32),
        compiler_params=pltpu.CompilerParams(dimension_semantics=("arbitrary",),
                                             vmem_limit_bytes=VMEM_LIMIT),
        name="moe_scatter",
    )(meta, pos.reshape(t // tm, 1, 2 * tm), x)


def _moe_gemm_kernel(te_ref, used_ref, x_ref, wg_ref, wu_ref, wd_ref, y_ref, xb_ref, acc_ref):
    del te_ref
    f = pl.program_id(1)
    tg, d = acc_ref.shape
    live = pl.program_id(0) < used_ref[0]

    @pl.when(jnp.logical_not(live) & (f == 0))
    def _():
        y_ref[...] = jnp.zeros_like(y_ref)

    @pl.when(live & (f == 0))
    def _():
        for s in range(d // V7X_LANES):
            xb_ref[:, s * V7X_LANES:(s + 1) * V7X_LANES] = _load_lane_tile(x_ref, tg, s).astype(BF16)
        acc_ref[...] = jnp.zeros_like(acc_ref)

    @pl.when(live)
    def _():
        acc_ref[...] += _swiglu(xb_ref[...], wg_ref[0], wu_ref[0], wd_ref[0])

    @pl.when(live & (f == pl.num_programs(1) - 1))
    def _():
        _store_rows_as_tiles(y_ref, acc_ref[...])


def _moe_gemm(tile_expert, n_used, xs, w_gate, w_up, w_down, *, tg, tf):
    n_rows = xs.shape[0] // ROW_TILE
    d, f_dim = w_gate.shape[-2:]
    row_block = (tg * ROW_TILE, V7X_LANES)
    x_row = pl.BlockSpec(row_block, lambda i, f, te, used: (jnp.minimum(i, used[0] - 1), 0))
    y_row = pl.BlockSpec(row_block, lambda i, f, te, used: (i, 0))
    w_tile = lambda i, f, used: jnp.where(i < used[0], f, f_dim // tf - 1)
    return pl.pallas_call(
        _moe_gemm_kernel,
        grid_spec=pltpu.PrefetchScalarGridSpec(
            num_scalar_prefetch=2,
            grid=(n_rows // tg, f_dim // tf),
            in_specs=[x_row,
                      pl.BlockSpec((1, d, tf), lambda i, f, te, used: (te[i], 0, w_tile(i, f, used))),
                      pl.BlockSpec((1, d, tf), lambda i, f, te, used: (te[i], 0, w_tile(i, f, used))),
                      pl.BlockSpec((1, tf, d), lambda i, f, te, used: (te[i], w_tile(i, f, used), 0))],
            out_specs=y_row,
            scratch_shapes=[pltpu.VMEM((tg, d), BF16), pltpu.VMEM((tg, d), F32)]),
        out_shape=jax.ShapeDtypeStruct((n_rows * ROW_TILE, V7X_LANES), F32),
        compiler_params=pltpu.CompilerParams(dimension_semantics=("arbitrary", "arbitrary"),
                                             vmem_limit_bytes=VMEM_LIMIT),
        name="moe_gemm",
    )(tile_expert, n_used, xs, w_gate, w_up, w_down)


def _moe_combine_kernel(pos_ref, h_ref, route_ref, ys_ref, o_ref, buf_ref, sem, *, tm):
    def start(r, carry):
        _row_copy(ys_ref, pos_ref[0, 0, 2 * r], buf_ref.at[0], r, sem).start(priority=0)
        _row_copy(ys_ref, pos_ref[0, 0, 2 * r + 1], buf_ref.at[1], r, sem).start(priority=1)
        return carry

    lax.fori_loop(0, tm, start, 0, unroll=8)

    def wait(r, carry):
        _row_copy(ys_ref, 0, buf_ref.at[0], 0, sem).wait()
        _row_copy(ys_ref, 0, buf_ref.at[1], 0, sem).wait()
        return carry

    lax.fori_loop(0, tm, wait, 0, unroll=8)
    route = route_ref[...]
    p1 = route[:, 2:3]
    p2 = route[:, 3:4]
    for s in range(o_ref.shape[1] // V7X_LANES):
        cols = slice(s * V7X_LANES, (s + 1) * V7X_LANES)
        o_ref[:, cols] = (h_ref[:, cols] + p1 * _load_lane_tile(buf_ref.at[0], tm, s)
                          + p2 * _load_lane_tile(buf_ref.at[1], tm, s))


def _moe_combine(pos, h, route, ys, *, tm=256):
    t, d = h.shape
    return pl.pallas_call(
        functools.partial(_moe_combine_kernel, tm=tm),
        grid=(t // tm,),
        in_specs=[pl.BlockSpec((1, 1, 2 * tm), lambda i: (i, 0, 0), memory_space=pltpu.SMEM),
                  pl.BlockSpec((tm, d), lambda i: (i, 0)),
                  pl.BlockSpec((tm, N_EXPERTS), lambda i: (i, 0)),
                  pl.BlockSpec(memory_space=pl.ANY)],
        out_specs=pl.BlockSpec((tm, d), lambda i: (i, 0)),
        out_shape=jax.ShapeDtypeStruct((t, d), F32),
        scratch_shapes=[pltpu.VMEM((2, tm * ROW_TILE, V7X_LANES), F32), pltpu.SemaphoreType.DMA(())],
        compiler_params=pltpu.CompilerParams(dimension_semantics=("arbitrary",),
                                             vmem_limit_bytes=VMEM_LIMIT),
        name="moe_combine",
    )(pos.reshape(t // tm, 1, 2 * tm), h, route, ys)


def _moe(h, hn, route, counts, w_gate, w_up, w_down, *, tg=1024, tf=512):
    t, d = h.shape
    n_tiles = (2 * t + N_EXPERTS * (tg - 1)) // tg
    counts = counts[0].astype(jnp.int32)
    padded = (counts + tg - 1) // tg * tg
    e_ids = jnp.arange(N_EXPERTS, dtype=jnp.int32)
    ends = jnp.sum(jnp.where(e_ids[:, None] <= e_ids[None, :], padded[:, None], 0), axis=0)
    starts = ends - padded
    experts = route[:, 0:2].astype(jnp.int32)
    start_of = jnp.sum(jnp.where(experts[:, :, None] == e_ids, starts, 0), axis=-1)
    pos = start_of + route[:, 4:6].astype(jnp.int32)
    n_used = ends[-1:] // tg
    tile_start = jnp.minimum(jnp.arange(n_tiles, dtype=jnp.int32), n_used - 1) * tg
    tile_expert = jnp.sum((ends[None, :] <= tile_start[:, None]).astype(jnp.int32), axis=1)
    xs = _moe_scatter(jnp.concatenate([ends, padded]), pos, hn, n_tiles * tg, tg=tg)
    ys = _moe_gemm(tile_expert, n_used, xs, w_gate, w_up, w_down, tg=tg, tf=tf)
    return _moe_combine(pos, h, route, ys)


def _rope_tables(seq):
    half = ROT_DIM // 2
    inv_freq = ROPE_THETA ** (-jnp.arange(half, dtype=F32) / half)
    ang = jnp.arange(seq, dtype=F32)[:, None] * inv_freq[None, :]
    cos = jnp.cos(ang)
    sin = jnp.sin(ang)
    rest = HEAD_DIM - ROT_DIM
    cos_full = jnp.concatenate([cos, cos, jnp.ones((seq, rest), F32)], axis=-1)
    sin_signed = jnp.concatenate([-sin, sin, jnp.zeros((seq, rest), F32)], axis=-1)
    return cos_full, sin_signed


def kernel(x, e_mix_norm, e_w_in, e_conv_w, e_gmlp_ln_g, e_gmlp_ln_b, e_w_spatial, e_b_spatial, e_w_out, e_ffn_norm, e_w_gate, e_w_up, e_w_down, o_mix_norm, o_w_qkv, o_q_norm, o_k_norm, o_w_o, o_ffn_norm, o_w_router, o_w_gate, o_w_up, o_w_down):
    batch, seq, d = x.shape
    t = batch * seq
    xf = x.reshape(t, d)

    bias_full = jnp.repeat(jnp.transpose(e_b_spatial[0]), GMLP_GROUP_DIM, axis=1)
    h1, hn1 = _mixer0(
        xf, e_mix_norm[0][None, :], e_w_in[0].astype(BF16), e_conv_w[0],
        e_gmlp_ln_g[0][None, :], e_gmlp_ln_b[0][None, :], e_w_spatial[0], bias_full,
        e_w_out[0].astype(BF16), e_ffn_norm[0][None, :], seq=seq)
    h2 = _ffn(hn1, h1, e_w_gate[0].astype(BF16), e_w_up[0].astype(BF16), e_w_down[0].astype(BF16),
              tm=512, tf=1408)

    cos, sin = _rope_tables(seq)
    qg = o_q_norm[0][None, :]
    kg = o_k_norm[0][None, :]
    q, k, vt, km = _qkv(h2, o_mix_norm[0][None, :], o_w_qkv[0].astype(BF16), qg, kg, cos, sin, seq=seq)
    as_seq = lambda a: a.reshape(batch, -1, d)
    o = _attn(as_seq(q), as_seq(k), vt.reshape(batch, seq // MOBA_BLOCK, d, MOBA_BLOCK),
              as_seq(km), qg, kg)
    h3, hn3, route, counts = _wo_router(o.reshape(t, d), h2, o_w_o[0].astype(BF16),
                                        o_ffn_norm[0][None, :], o_w_router[0])
    out = _moe(h3, hn3, route, counts, o_w_gate[0], o_w_up[0], o_w_down[0])
    return out.reshape(batch, seq, d)
```

```python
import functools
import math

import jax
import jax.numpy as jnp
from jax import lax
from jax.experimental import pallas as pl
from jax.experimental.pallas import tpu as pltpu

F32 = jnp.float32
BF16 = jnp.bfloat16

D_MODEL = 1024
CONV_DIM = 512
CONV_KERNEL = 3
GMLP_DIM = 512
GMLP_GROUPS = 8
GMLP_GROUP_DIM = GMLP_DIM // GMLP_GROUPS
GMLP_CHUNK = 128
MIX_IN = 3 * CONV_DIM + 2 * GMLP_DIM
N_HEADS = 8
HEAD_DIM = 128
ROT_DIM = HEAD_DIM // 4
ROPE_THETA = 500000.0
MOBA_BLOCK = 256
MOBA_TOPK = 3
N_EXPERTS = 8
EPS = 1e-6

V7X_LANES = 128
ROW_TILE = 8
VMEM_LIMIT = 56 * 1024 * 1024

NEG_INF = float("-inf")
QK_SCALE = HEAD_DIM ** -0.5 * math.log2(math.e)
MASK_MARGIN = 256.0
M_INIT = -1e30
ONES_ROWS = 16


def _rms(x, g):
    return x * lax.rsqrt(jnp.mean(x * x, axis=-1, keepdims=True) + EPS) * g


def _dot(a, b):
    return jnp.dot(a, b, preferred_element_type=F32)


def _dot_nt(a, b):
    return lax.dot_general(a, b, (((1,), (1,)), ((), ())), preferred_element_type=F32)


def _split_bf16(x):
    hi = x.astype(BF16)
    lo = (x - hi.astype(F32)).astype(BF16)
    return hi, lo


def _mixer0_kernel(x_ref, xres_ref, g_ref, win_ref, convw_ref, lng_ref, lnb_ref, ws_ref, bias_ref,
                   wout_ref, g2_ref, h_ref, hn_ref, carry_ref, y_ref, buf0_ref, buf1_ref,
                   *, tm, tiles_per_seq, n_tiles):
    def produce(proj_ref):
        hn = _rms(x_ref[...], g_ref[...]).astype(BF16)
        proj_ref[...] = _dot(hn, win_ref[...])

    def consume(proj_ref):
        tile = pl.program_id(0) - 1
        a_h = proj_ref[:, 0:CONV_DIM]
        a_c = proj_ref[:, CONV_DIM:2 * CONV_DIM]
        a_b = proj_ref[:, 2 * CONV_DIM:3 * CONV_DIM]
        g_v = proj_ref[:, 3 * CONV_DIM + GMLP_DIM:]

        z = a_c * a_h
        prev = jnp.where(tile % tiles_per_seq == 0, 0.0, carry_ref[...])
        row = lax.broadcasted_iota(jnp.int32, (tm, CONV_DIM), 0)
        z1 = jnp.where(row == 0, prev[7:8, :], pltpu.roll(z, 1, 0))
        z2 = jnp.where(row == 0, prev[6:7, :],
                       jnp.where(row == 1, prev[7:8, :], pltpu.roll(z, 2, 0)))
        carry_ref[...] = z[tm - 8:tm, :]
        cw = convw_ref[...]
        y_a = a_b * (cw[0:1, :] * z2 + cw[1:2, :] * z1 + cw[2:3, :] * z)
        y_ref[:, 0:CONV_DIM] = y_a.astype(BF16)

        mu = jnp.mean(g_v, axis=-1, keepdims=True)
        cen = g_v - mu
        var = jnp.mean(cen * cen, axis=-1, keepdims=True)
        v = cen * lax.rsqrt(var + EPS) * lng_ref[...] + lnb_ref[...]
        tr = lax.broadcasted_iota(jnp.int32, (GMLP_CHUNK, GMLP_CHUNK), 0)
        tc = lax.broadcasted_iota(jnp.int32, (GMLP_CHUNK, GMLP_CHUNK), 1)
        causal = tc <= tr
        lo_half = tc < GMLP_GROUP_DIM
        n_chunks = tm // GMLP_CHUNK
        for j in range(GMLP_DIM // V7X_LANES):
            cols = slice(j * V7X_LANES, (j + 1) * V7X_LANES)
            w_pair = jnp.concatenate(
                [jnp.where(causal, ws_ref[2 * j], 0.0), jnp.where(causal, ws_ref[2 * j + 1], 0.0)],
                axis=1).astype(BF16)
            stacked = []
            for c in range(n_chunks):
                vj = v[c * GMLP_CHUNK:(c + 1) * GMLP_CHUNK, cols]
                stacked.append(jnp.concatenate(
                    [jnp.where(lo_half, vj, 0.0), jnp.where(lo_half, 0.0, vj)], axis=0))
            mixed = _dot(w_pair, jnp.concatenate(stacked, axis=1).astype(BF16))
            for c in range(n_chunks):
                rows = slice(c * GMLP_CHUNK, (c + 1) * GMLP_CHUNK)
                g_u = proj_ref[rows, 3 * CONV_DIM + j * V7X_LANES:3 * CONV_DIM + (j + 1) * V7X_LANES]
                m_c = mixed[:, c * V7X_LANES:(c + 1) * V7X_LANES] + bias_ref[:, cols]
                y_ref[rows, CONV_DIM + j * V7X_LANES:CONV_DIM + (j + 1) * V7X_LANES] = (
                    g_u * m_c).astype(BF16)

        h = xres_ref[...] + _dot(y_ref[...], wout_ref[...])
        h_ref[...] = h
        hn_ref[...] = _rms(h, g2_ref[...]).astype(BF16)

    @pl.when(pl.program_id(0) == 0)
    def _():
        carry_ref[...] = jnp.zeros_like(carry_ref)

    _skewed(n_tiles, produce, consume, (buf0_ref, buf1_ref))


def _mixer0(x, g, w_in, conv_w, ln_g, ln_b, w_s, bias_full, w_out, g2, *, seq, tm=512):
    t, d = x.shape
    n_tiles = t // tm
    const = lambda *shape: pl.BlockSpec(shape, lambda i: (0,) * len(shape))
    done = pl.BlockSpec((tm, d), lambda i: (_consumed(i), 0))
    return pl.pallas_call(
        functools.partial(_mixer0_kernel, tm=tm, tiles_per_seq=seq // tm, n_tiles=n_tiles),
        grid=(n_tiles + 1,),
        in_specs=[
            pl.BlockSpec((tm, d), lambda i: (_produced(i, n_tiles), 0)),
            done,
            const(1, d),
            const(d, MIX_IN),
            const(CONV_KERNEL, CONV_DIM),
            const(1, GMLP_DIM),
            const(1, GMLP_DIM),
            const(GMLP_GROUPS, GMLP_CHUNK, GMLP_CHUNK),
            const(GMLP_CHUNK, GMLP_DIM),
            const(CONV_DIM + GMLP_DIM, d),
            const(1, d),
        ],
        out_specs=[done, done],
        out_shape=[jax.ShapeDtypeStruct((t, d), F32), jax.ShapeDtypeStruct((t, d), BF16)],
        scratch_shapes=[pltpu.VMEM((8, CONV_DIM), F32), pltpu.VMEM((tm, CONV_DIM + GMLP_DIM), BF16),
                        pltpu.VMEM((tm, MIX_IN), F32), pltpu.VMEM((tm, MIX_IN), F32)],
        compiler_params=pltpu.CompilerParams(dimension_semantics=("arbitrary",),
                                             vmem_limit_bytes=VMEM_LIMIT),
        name="mixer0",
    )(x, x, g, w_in, conv_w, ln_g, ln_b, w_s, bias_full, w_out, g2)


def _swiglu(x, wg, wu, wd):
    a = _dot(x, wg.astype(BF16))
    u = _dot(x, wu.astype(BF16))
    hact = (a * jax.nn.sigmoid(a) * u).astype(BF16)
    return _dot(hact, wd.astype(BF16))


def _ffn_kernel(x_ref, res_ref, wg_ref, wu_ref, wd_ref, o_ref):
    @pl.when(pl.program_id(1) == 0)
    def _():
        o_ref[...] = res_ref[...]

    o_ref[...] += _swiglu(x_ref[...], wg_ref[...], wu_ref[...], wd_ref[...])


def _ffn(xn, res, w_gate, w_up, w_down, *, tm, tf):
    t, d = xn.shape
    f_dim = w_gate.shape[1]
    row = pl.BlockSpec((tm, d), lambda i, f: (i, 0))
    return pl.pallas_call(
        _ffn_kernel,
        grid=(t // tm, f_dim // tf),
        in_specs=[row, row,
                  pl.BlockSpec((d, tf), lambda i, f: (0, f)),
                  pl.BlockSpec((d, tf), lambda i, f: (0, f)),
                  pl.BlockSpec((tf, d), lambda i, f: (f, 0))],
        out_specs=row,
        out_shape=jax.ShapeDtypeStruct((t, d), F32),
        compiler_params=pltpu.CompilerParams(dimension_semantics=("parallel", "arbitrary"),
                                             vmem_limit_bytes=VMEM_LIMIT),
        name="dense_ffn",
    )(xn, res, w_gate, w_up, w_down)


def _skewed(n_tiles, produce, consume, bufs):
    i = pl.program_id(0)
    b0, b1 = bufs

    @pl.when(i == 0)
    def _():
        produce(b0)

    for parity, (dst, src) in enumerate(((b0, b1), (b1, b0))):
        @pl.when((i > 0) & (i < n_tiles) & (i % 2 == parity))
        def _():
            produce(dst)
            consume(src)

    @pl.when(i == n_tiles)
    def _():
        consume(bufs[(n_tiles - 1) % 2])


def _produced(i, n_tiles):
    return jnp.minimum(i, n_tiles - 1)


def _consumed(i):
    return jnp.maximum(i - 1, 0)


def _qkv_kernel(h_ref, g_ref, w_ref, qg_ref, kg_ref, cos_ref, sin_ref,
                q_ref, k_ref, v_ref, km_ref, buf0_ref, buf1_ref, *, tm, n_tiles):
    hd = N_HEADS * HEAD_DIM

    def produce(buf_ref):
        hn = _rms(h_ref[...], g_ref[...]).astype(BF16)
        buf_ref[...] = _dot(hn, w_ref[...])

    def consume(buf_ref):
        cos = cos_ref[...]
        sin = sin_ref[...]
        lane = lax.broadcasted_iota(jnp.int32, (tm, HEAD_DIM), 1)
        first = lane < ROT_DIM // 2

        def rope(x):
            partner = jnp.where(first, pltpu.roll(x, HEAD_DIM - ROT_DIM // 2, 1),
                                pltpu.roll(x, ROT_DIM // 2, 1))
            return x * cos + partner * sin

        for c in range(tm // MOBA_BLOCK):
            v_ref[c] = jnp.transpose(
                buf_ref[c * MOBA_BLOCK:(c + 1) * MOBA_BLOCK, 2 * hd:]).astype(BF16)
        for h in range(N_HEADS):
            cols = slice(h * HEAD_DIM, (h + 1) * HEAD_DIM)
            q = rope(_rms(buf_ref[:, cols], qg_ref[...]))
            k = rope(_rms(buf_ref[:, hd + h * HEAD_DIM:hd + (h + 1) * HEAD_DIM], kg_ref[...]))
            q_ref[:, cols] = (q * QK_SCALE).astype(BF16)
            k_ref[:, cols] = k.astype(BF16)
            for c in range(tm // MOBA_BLOCK):
                km_ref[c, :, cols] = jnp.mean(
                    k[c * MOBA_BLOCK:(c + 1) * MOBA_BLOCK, :], axis=0, keepdims=True)

    _skewed(n_tiles, produce, consume, (buf0_ref, buf1_ref))


def _qkv(h, g, w, qg, kg, cos, sin, *, seq, tm=512):
    t, d = h.shape
    ns = seq // tm
    n_tiles = t // tm
    hd = N_HEADS * HEAD_DIM
    const = lambda *shape: pl.BlockSpec(shape, lambda i: (0,) * len(shape))
    row_spec = pl.BlockSpec((tm, hd), lambda i: (_consumed(i), 0))
    row_shape = jax.ShapeDtypeStruct((t, hd), BF16)
    table = pl.BlockSpec((tm, HEAD_DIM), lambda i: (_consumed(i) % ns, 0))
    return pl.pallas_call(
        functools.partial(_qkv_kernel, tm=tm, n_tiles=n_tiles),
        grid=(n_tiles + 1,),
        in_specs=[
            pl.BlockSpec((tm, d), lambda i: (_produced(i, n_tiles), 0)),
            const(1, d),
            const(d, 3 * hd),
            const(1, HEAD_DIM),
            const(1, HEAD_DIM),
            table,
            table,
        ],
        out_specs=[row_spec, row_spec,
                   pl.BlockSpec((tm // MOBA_BLOCK, hd, MOBA_BLOCK), lambda i: (_consumed(i), 0, 0)),
                   pl.BlockSpec((tm // MOBA_BLOCK, 1, hd), lambda i: (_consumed(i), 0, 0))],
        out_shape=[row_shape, row_shape,
                   jax.ShapeDtypeStruct((t // MOBA_BLOCK, hd, MOBA_BLOCK), BF16),
                   jax.ShapeDtypeStruct((t // MOBA_BLOCK, 1, hd), F32)],
        scratch_shapes=[pltpu.VMEM((tm, 3 * hd), F32), pltpu.VMEM((tm, 3 * hd), F32)],
        compiler_params=pltpu.CompilerParams(dimension_semantics=("arbitrary",),
                                             vmem_limit_bytes=VMEM_LIMIT),
        name="qkv",
    )(h, g, w, qg, kg, cos, sin)


def _attn_kernel(q_ref, k_ref, v_ref, km_ref, qg_ref, kg_ref, o_ref, qx_ref, acc_ref, m_ref,
                 s0_ref, s1_ref, *, n_blocks):
    i = pl.program_id(1)
    bs = MOBA_BLOCK
    hd = HEAD_DIM
    bound = QK_SCALE * hd * jnp.max(jnp.abs(qg_ref[...]), axis=-1, keepdims=True) * jnp.max(
        jnp.abs(kg_ref[...]), axis=-1, keepdims=True)
    big = 2.1 * bound + MASK_MARGIN
    blk = lax.broadcasted_iota(jnp.int32, (n_blocks, bs), 0)
    past = blk < i
    key = lax.broadcasted_iota(jnp.int32, (bs, bs), 0)
    qry = lax.broadcasted_iota(jnp.int32, (bs, bs), 1)
    causal = key <= qry
    ones_rows = jnp.ones((ONES_ROWS, bs), BF16)
    own = pl.multiple_of(i * bs, bs)

    for h in range(N_HEADS):
        cols = slice(h * hd, (h + 1) * hd)
        q = q_ref[0, :, cols]
        km_hi, km_lo = _split_bf16(km_ref[0, :, cols])
        gate_t = _dot_nt(km_hi, q) + _dot_nt(km_lo, q)
        cur = jnp.where(past, gate_t, NEG_INF)
        sel = jnp.zeros((n_blocks, bs), F32)
        for _ in range(min(MOBA_TOPK, n_blocks)):
            top = jnp.max(cur, axis=0, keepdims=True)
            idx = jnp.min(jnp.where(cur == top, blk, n_blocks), axis=0, keepdims=True)
            pick = blk == idx
            sel = jnp.where(pick & past, 1.0, sel)
            cur = jnp.where(pick, NEG_INF, cur)
        bias_t = jnp.where(sel > 0.5, 0.0, -big)
        bias_t = jnp.concatenate([bias_t, jnp.zeros((hd - n_blocks, bs), F32)], axis=0)
        qx_ref[h] = jnp.concatenate([q, jnp.transpose(bias_t).astype(BF16)], axis=1)

        s = _dot_nt(k_ref[0, pl.ds(own, bs), cols], q)
        s0_ref[h] = jnp.where(causal, s, NEG_INF)
        acc_ref[h] = jnp.zeros(acc_ref.shape[1:], F32)
        m_ref[h] = jnp.full(m_ref.shape[1:], M_INIT, F32)

    def consume(h, s_ref, v_blk):
        cols = slice(h * hd, (h + 1) * hd)
        s = s_ref[h]
        m_old = m_ref[h]
        m_new = jnp.maximum(m_old, jnp.max(s, axis=0, keepdims=True))
        p = jnp.exp2(s - m_new)
        vx = jnp.concatenate([v_ref[0, v_blk, cols, :], ones_rows], axis=0)
        acc_ref[h] = jnp.exp2(m_old - m_new) * acc_ref[h] + _dot(vx, p.astype(BF16))
        m_ref[h] = m_new

    def step(j, s_in, s_out):
        start = pl.multiple_of(j * bs, bs)
        onehot = (lax.broadcasted_iota(jnp.int32, (bs, hd), 1) == j).astype(BF16)
        v_blk = jnp.where(j == 0, i, j - 1)
        for h in range(N_HEADS):
            cols = slice(h * hd, (h + 1) * hd)
            kx = jnp.concatenate([k_ref[0, pl.ds(start, bs), cols], onehot], axis=1)
            s_out[h] = _dot_nt(kx, qx_ref[h])
            consume(h, s_in, v_blk)

    def body(j, carry):
        @pl.when(j % 2 == 0)
        def _():
            step(j, s0_ref, s1_ref)

        @pl.when(j % 2 == 1)
        def _():
            step(j, s1_ref, s0_ref)

        return carry

    lax.fori_loop(0, i, body, 0)

    def finish(s_ref):
        v_blk = jnp.where(i == 0, i, i - 1)
        for h in range(N_HEADS):
            consume(h, s_ref, v_blk)
            a = acc_ref[h]
            o_t = a[:hd, :] / a[hd:hd + 1, :]
            o_ref[0, :, h * hd:(h + 1) * hd] = jnp.transpose(o_t).astype(BF16)

    @pl.when(i % 2 == 0)
    def _():
        finish(s0_ref)

    @pl.when(i % 2 == 1)
    def _():
        finish(s1_ref)


def _attn(q, k, vt, km, qg, kg):
    batch, seq, d = q.shape
    nb = seq // MOBA_BLOCK
    tile = pl.BlockSpec((1, MOBA_BLOCK, d), lambda b, i: (b, i, 0))
    gain = pl.BlockSpec((1, HEAD_DIM), lambda b, i: (0, 0))
    return pl.pallas_call(
        functools.partial(_attn_kernel, n_blocks=nb),
        grid=(batch, nb),
        in_specs=[tile,
                  pl.BlockSpec((1, seq, d), lambda b, i: (b, 0, 0)),
                  pl.BlockSpec((1, nb, d, MOBA_BLOCK), lambda b, i: (b, 0, 0, 0)),
                  pl.BlockSpec((1, nb, d), lambda b, i: (b, 0, 0)), gain, gain],
        out_specs=tile,
        out_shape=jax.ShapeDtypeStruct((batch, seq, d), BF16),
        scratch_shapes=[
            pltpu.VMEM((N_HEADS, MOBA_BLOCK, 2 * HEAD_DIM), BF16),
            pltpu.VMEM((N_HEADS, HEAD_DIM + ONES_ROWS, MOBA_BLOCK), F32),
            pltpu.VMEM((N_HEADS, 1, MOBA_BLOCK), F32),
            pltpu.VMEM((N_HEADS, MOBA_BLOCK, MOBA_BLOCK), F32),
            pltpu.VMEM((N_HEADS, MOBA_BLOCK, MOBA_BLOCK), F32),
        ],
        compiler_params=pltpu.CompilerParams(dimension_semantics=("parallel", "arbitrary"),
                                             vmem_limit_bytes=VMEM_LIMIT),
        name="moba_attn",
    )(q, k, vt, km, qg, kg)


def _wo_router_kernel(o_ref, res_ref, wo_ref, g_ref, wrt_ref, h_ref, hn_ref, route_ref, counts_ref,
                      cnt_ref, buf0_ref, buf1_ref, *, tm, n_tiles):
    def produce(buf_ref):
        h = res_ref[...] + _dot(o_ref[...], wo_ref[...])
        h_ref[...] = h
        buf_ref[...] = h

    def consume(buf_ref):
        hn = _rms(buf_ref[...], g_ref[...])
        _store_rows_as_tiles(hn_ref, hn)

        x_hi, x_lo = _split_bf16(hn)
        w_hi, w_lo = _split_bf16(wrt_ref[...])
        part = _dot_nt(jnp.concatenate([w_hi, w_lo], axis=0), x_hi)
        logits = part[:N_EXPERTS] + part[N_EXPERTS:] + _dot_nt(w_hi, x_lo)
        e_id = lax.broadcasted_iota(jnp.int32, (N_EXPERTS, tm), 0)
        m1 = jnp.max(logits, axis=0, keepdims=True)
        i1 = jnp.min(jnp.where(logits == m1, e_id, N_EXPERTS), axis=0, keepdims=True)
        pick1 = e_id == i1
        rest = jnp.where(pick1, NEG_INF, logits)
        m2 = jnp.max(rest, axis=0, keepdims=True)
        i2 = jnp.min(jnp.where(rest == m2, e_id, N_EXPERTS), axis=0, keepdims=True)
        pick2 = e_id == i2
        e2 = jnp.exp(m2 - m1)
        denom = 1.0 + e2

        chosen = jnp.where(pick1 | pick2, 1.0, 0.0)
        tr = lax.broadcasted_iota(jnp.int32, (tm, tm), 0)
        tc = lax.broadcasted_iota(jnp.int32, (tm, tm), 1)
        earlier = jnp.where(tr < tc, 1.0, 0.0).astype(BF16)
        count = cnt_ref[...]
        rank = _dot(chosen.astype(BF16), earlier) + count
        count = count + jnp.sum(chosen, axis=1, keepdims=True)
        cnt_ref[...] = count
        counts_ref[...] = count
        rank1 = jnp.sum(jnp.where(pick1, rank, 0.0), axis=0, keepdims=True)
        rank2 = jnp.sum(jnp.where(pick2, rank, 0.0), axis=0, keepdims=True)
        fields = [i1.astype(F32), i2.astype(F32), 1.0 / denom, e2 / denom, rank1, rank2]
        route = jnp.zeros((N_EXPERTS, tm), F32)
        for slot, val in enumerate(fields):
            route = jnp.where(e_id == slot, val, route)
        route_ref[...] = route

    @pl.when(pl.program_id(0) == 0)
    def _():
        cnt_ref[...] = jnp.zeros_like(cnt_ref)

    _skewed(n_tiles, produce, consume, (buf0_ref, buf1_ref))


def _wo_router(o, res, w_o, g, w_r_t, *, tm=512):
    t, d = res.shape
    n_tiles = t // tm
    const = lambda *shape: pl.BlockSpec(shape, lambda i: (0,) * len(shape))
    new = pl.BlockSpec((tm, d), lambda i: (_produced(i, n_tiles), 0))
    return pl.pallas_call(
        functools.partial(_wo_router_kernel, tm=tm, n_tiles=n_tiles),
        grid=(n_tiles + 1,),
        in_specs=[new, new, const(d, d), const(1, d), const(N_EXPERTS, d)],
        out_specs=[new,
                   pl.BlockSpec((tm * ROW_TILE, V7X_LANES), lambda i: (_consumed(i), 0)),
                   pl.BlockSpec((N_EXPERTS, tm), lambda i: (0, _consumed(i))),
                   const(N_EXPERTS, 1)],
        out_shape=[jax.ShapeDtypeStruct((t, d), F32),
                   jax.ShapeDtypeStruct((t * ROW_TILE, V7X_LANES), F32),
                   jax.ShapeDtypeStruct((N_EXPERTS, t), F32),
                   jax.ShapeDtypeStruct((N_EXPERTS, 1), F32)],
        scratch_shapes=[pltpu.VMEM((N_EXPERTS, 1), F32), pltpu.VMEM((tm, d), F32),
                        pltpu.VMEM((tm, d), F32)],
        compiler_params=pltpu.CompilerParams(dimension_semantics=("arbitrary",),
                                             vmem_limit_bytes=VMEM_LIMIT),
        name="wo_router",
    )(o, res, w_o, g, w_r_t)


def _store_rows_as_tiles(dst_ref, x):
    n = x.shape[0]
    for s in range(ROW_TILE):
        dst_ref[pl.ds(s, n, stride=ROW_TILE), :] = x[:, s * V7X_LANES:(s + 1) * V7X_LANES]


def _load_lane_tile(src_ref, n, s):
    return src_ref[pl.ds(s, n, stride=ROW_TILE), :]


def _row_copy(src_ref, src_row, dst_ref, dst_row, sem):
    src = src_ref.at[pl.ds(pl.multiple_of(src_row * ROW_TILE, ROW_TILE), ROW_TILE), :]
    dst = dst_ref.at[pl.ds(pl.multiple_of(dst_row * ROW_TILE, ROW_TILE), ROW_TILE), :]
    return pltpu.make_async_copy(src, dst, sem)


def _moe_scatter_kernel(meta_ref, pos_ref, x_ref, xs_ref, zero_ref, sem, zero_sem, *, tm, tg):
    zr = zero_ref.shape[0] // ROW_TILE
    n_rows = xs_ref.shape[0] // ROW_TILE
    total = meta_ref[N_EXPERTS - 1]

    def zero_tile(base):
        base = pl.multiple_of(base * ROW_TILE, tg * ROW_TILE)
        return [pltpu.make_async_copy(
            zero_ref, xs_ref.at[pl.ds(base + c * zr * ROW_TILE, zr * ROW_TILE), :], zero_sem)
            for c in range(tg // zr)]

    @pl.when(pl.program_id(0) == 0)
    def _():
        zero_ref[...] = jnp.zeros_like(zero_ref)
        for wait in (False, True):
            for e in range(N_EXPERTS):
                @pl.when(meta_ref[N_EXPERTS + e] > 0)
                def _():
                    for cp in zero_tile(meta_ref[e] - tg):
                        cp.wait() if wait else cp.start()
            for k in range(N_EXPERTS - 1):
                @pl.when(total + k * tg < n_rows)
                def _():
                    for cp in zero_tile(total + k * tg):
                        cp.wait() if wait else cp.start()

    def start(r, carry):
        _row_copy(x_ref, r, xs_ref, pos_ref[0, 0, r], sem).start(priority=0)
        _row_copy(x_ref, r, xs_ref, pos_ref[0, 0, tm + r], sem).start(priority=1)
        return carry

    lax.fori_loop(0, tm, start, 0, unroll=8)

    def wait(r, carry):
        _row_copy(x_ref, 0, xs_ref, 0, sem).wait()
        _row_copy(x_ref, 0, xs_ref, 0, sem).wait()
        return carry

    lax.fori_loop(0, tm, wait, 0, unroll=8)


def _moe_scatter(meta, pos, x, n_rows, *, tg, tm=256, zero_rows=256):
    t = x.shape[0] // ROW_TILE
    return pl.pallas_call(
        functools.partial(_moe_scatter_kernel, tm=tm, tg=tg),
        grid_spec=pltpu.PrefetchScalarGridSpec(
            num_scalar_prefetch=1,
            grid=(t // tm,),
            in_specs=[pl.BlockSpec((1, 1, 2 * tm), lambda i, meta: (i, 0, 0), memory_space=pltpu.SMEM),
                      pl.BlockSpec((tm * ROW_TILE, V7X_LANES), lambda i, meta: (i, 0))],
            out_specs=pl.BlockSpec(memory_space=pl.ANY),
            scratch_shapes=[pltpu.VMEM((zero_rows * ROW_TILE, V7X_LANES), F32),
                            pltpu.SemaphoreType.DMA(()), pltpu.SemaphoreType.DMA(())]),
        out_shape=jax.ShapeDtypeStruct((n_rows * ROW_TILE, V7X_LANES), F32),
        compiler_params=pltpu.CompilerParams(dimension_semantics=("arbitrary",),
                                             vmem_limit_bytes=VMEM_LIMIT),
        name="moe_scatter",
    )(meta, pos, x)


def _moe_gemm_kernel(te_ref, used_ref, x_ref, wg_ref, wu_ref, wd_ref, y_ref, xb_ref, acc_ref):
    del te_ref
    f = pl.program_id(1)
    tg, d = acc_ref.shape
    live = pl.program_id(0) < used_ref[0]

    @pl.when(jnp.logical_not(live) & (f == 0))
    def _():
        y_ref[...] = jnp.zeros_like(y_ref)

    @pl.when(live & (f == 0))
    def _():
        for s in range(d // V7X_LANES):
            xb_ref[:, s * V7X_LANES:(s + 1) * V7X_LANES] = _load_lane_tile(x_ref, tg, s).astype(BF16)
        acc_ref[...] = jnp.zeros_like(acc_ref)

    @pl.when(live)
    def _():
        acc_ref[...] += _swiglu(xb_ref[...], wg_ref[0], wu_ref[0], wd_ref[0])

    @pl.when(live & (f == pl.num_programs(1) - 1))
    def _():
        _store_rows_as_tiles(y_ref, acc_ref[...])


def _moe_gemm(tile_expert, n_used, xs, w_gate, w_up, w_down, *, tg, tf):
    n_rows = xs.shape[0] // ROW_TILE
    d, f_dim = w_gate.shape[-2:]
    row_block = (tg * ROW_TILE, V7X_LANES)
    x_row = pl.BlockSpec(row_block, lambda i, f, te, used: (jnp.minimum(i, used[0] - 1), 0))
    y_row = pl.BlockSpec(row_block, lambda i, f, te, used: (i, 0))
    w_tile = lambda i, f, used: jnp.where(i < used[0], f, f_dim // tf - 1)
    return pl.pallas_call(
        _moe_gemm_kernel,
        grid_spec=pltpu.PrefetchScalarGridSpec(
            num_scalar_prefetch=2,
            grid=(n_rows // tg, f_dim // tf),
            in_specs=[x_row,
                      pl.BlockSpec((1, d, tf), lambda i, f, te, used: (te[i], 0, w_tile(i, f, used))),
                      pl.BlockSpec((1, d, tf), lambda i, f, te, used: (te[i], 0, w_tile(i, f, used))),
                      pl.BlockSpec((1, tf, d), lambda i, f, te, used: (te[i], w_tile(i, f, used), 0))],
            out_specs=y_row,
            scratch_shapes=[pltpu.VMEM((tg, d), BF16), pltpu.VMEM((tg, d), F32)]),
        out_shape=jax.ShapeDtypeStruct((n_rows * ROW_TILE, V7X_LANES), F32),
        compiler_params=pltpu.CompilerParams(dimension_semantics=("arbitrary", "arbitrary"),
                                             vmem_limit_bytes=VMEM_LIMIT),
        name="moe_gemm",
    )(tile_expert, n_used, xs, w_gate, w_up, w_down)


def _moe_combine_kernel(pos_ref, next_pos_ref, h_ref, prob_ref, ys_ref, o_ref, buf_ref, sem, *, tm):
    step = pl.program_id(0)
    slot = step % 2

    def fetch(table_ref, into):
        def start(r, carry):
            _row_copy(ys_ref, table_ref[0, 0, r], buf_ref.at[into, 0], r, sem.at[into]).start(
                priority=0)
            _row_copy(ys_ref, table_ref[0, 0, tm + r], buf_ref.at[into, 1], r, sem.at[into]).start(
                priority=1)
            return carry

        lax.fori_loop(0, tm, start, 0, unroll=8)

    @pl.when(step == 0)
    def _():
        fetch(pos_ref, 0)

    @pl.when(step + 1 < pl.num_programs(0))
    def _():
        fetch(next_pos_ref, 1 - slot)

    def wait(r, carry):
        _row_copy(ys_ref, 0, buf_ref.at[slot, 0], 0, sem.at[slot]).wait()
        _row_copy(ys_ref, 0, buf_ref.at[slot, 1], 0, sem.at[slot]).wait()
        return carry

    lax.fori_loop(0, tm, wait, 0, unroll=8)
    p1 = prob_ref[:, 0:1]
    p2 = prob_ref[:, 1:2]
    for s in range(o_ref.shape[1] // V7X_LANES):
        cols = slice(s * V7X_LANES, (s + 1) * V7X_LANES)
        o_ref[:, cols] = (h_ref[:, cols] + p1 * _load_lane_tile(buf_ref.at[slot, 0], tm, s)
                          + p2 * _load_lane_tile(buf_ref.at[slot, 1], tm, s))


def _moe_combine(pos, h, probs, ys, *, tm):
    t, d = h.shape
    n_steps = t // tm
    table = lambda index: pl.BlockSpec((1, 1, 2 * tm), index, memory_space=pltpu.SMEM)
    return pl.pallas_call(
        functools.partial(_moe_combine_kernel, tm=tm),
        grid=(n_steps,),
        in_specs=[table(lambda i: (i, 0, 0)),
                  table(lambda i: (jnp.minimum(i + 1, n_steps - 1), 0, 0)),
                  pl.BlockSpec((tm, d), lambda i: (i, 0)),
                  pl.BlockSpec((tm, 2), lambda i: (i, 0)),
                  pl.BlockSpec(memory_space=pl.ANY)],
        out_specs=pl.BlockSpec((tm, d), lambda i: (i, 0)),
        out_shape=jax.ShapeDtypeStruct((t, d), F32),
        scratch_shapes=[pltpu.VMEM((2, 2, tm * ROW_TILE, V7X_LANES), F32),
                        pltpu.SemaphoreType.DMA((2,))],
        compiler_params=pltpu.CompilerParams(dimension_semantics=("arbitrary",),
                                             vmem_limit_bytes=VMEM_LIMIT),
        name="moe_combine",
    )(pos, pos, h, probs, ys)


def _moe(h, hn, route, counts, w_gate, w_up, w_down, *, tg=1024, tf=512, tm=256):
    t, d = h.shape
    n_tiles = (2 * t + N_EXPERTS * (tg - 1)) // tg
    counts = counts[:, 0].astype(jnp.int32)
    padded = (counts + tg - 1) // tg * tg
    e_ids = jnp.arange(N_EXPERTS, dtype=jnp.int32)
    ends = jnp.sum(jnp.where(e_ids[:, None] <= e_ids[None, :], padded[:, None], 0), axis=0)
    starts = ends - padded
    experts = route[0:2].astype(jnp.int32)
    start_of = jnp.sum(jnp.where(experts[:, :, None] == e_ids, starts, 0), axis=-1)
    pos = start_of + route[4:6].astype(jnp.int32)
    pos = jnp.transpose(pos.reshape(2, t // tm, tm), (1, 0, 2)).reshape(t // tm, 1, 2 * tm)
    n_used = ends[-1:] // tg
    tile_start = jnp.minimum(jnp.arange(n_tiles, dtype=jnp.int32), n_used - 1) * tg
    tile_expert = jnp.sum((ends[None, :] <= tile_start[:, None]).astype(jnp.int32), axis=1)
    xs = _moe_scatter(jnp.concatenate([ends, padded]), pos, hn, n_tiles * tg, tg=tg, tm=tm)
    ys = _moe_gemm(tile_expert, n_used, xs, w_gate, w_up, w_down, tg=tg, tf=tf)
    return _moe_combine(pos, h, jnp.transpose(route[2:4]), ys, tm=tm)


def _rope_tables(seq):
    half = ROT_DIM // 2
    inv_freq = ROPE_THETA ** (-jnp.arange(half, dtype=F32) / half)
    ang = jnp.arange(seq, dtype=F32)[:, None] * inv_freq[None, :]
    cos = jnp.cos(ang)
    sin = jnp.sin(ang)
    rest = HEAD_DIM - ROT_DIM
    cos_full = jnp.concatenate([cos, cos, jnp.ones((seq, rest), F32)], axis=-1)
    sin_signed = jnp.concatenate([-sin, sin, jnp.zeros((seq, rest), F32)], axis=-1)
    return cos_full, sin_signed


def kernel(x, e_mix_norm, e_w_in, e_conv_w, e_gmlp_ln_g, e_gmlp_ln_b, e_w_spatial, e_b_spatial, e_w_out, e_ffn_norm, e_w_gate, e_w_up, e_w_down, o_mix_norm, o_w_qkv, o_q_norm, o_k_norm, o_w_o, o_ffn_norm, o_w_router, o_w_gate, o_w_up, o_w_down):
    batch, seq, d = x.shape
    t = batch * seq
    xf = x.reshape(t, d)

    bias_full = jnp.repeat(jnp.transpose(e_b_spatial[0]), GMLP_GROUP_DIM, axis=1)
    h1, hn1 = _mixer0(
        xf, e_mix_norm[0][None, :], e_w_in[0].astype(BF16), e_conv_w[0],
        e_gmlp_ln_g[0][None, :], e_gmlp_ln_b[0][None, :], e_w_spatial[0], bias_full,
        e_w_out[0].astype(BF16), e_ffn_norm[0][None, :], seq=seq)
    h2 = _ffn(hn1, h1, e_w_gate[0].astype(BF16), e_w_up[0].astype(BF16), e_w_down[0].astype(BF16),
              tm=512, tf=1408)

    cos, sin = _rope_tables(seq)
    qg = o_q_norm[0][None, :]
    kg = o_k_norm[0][None, :]
    q, k, vt, km = _qkv(h2, o_mix_norm[0][None, :], o_w_qkv[0].astype(BF16), qg, kg, cos, sin, seq=seq)
    as_seq = lambda a: a.reshape(batch, -1, d)
    o = _attn(as_seq(q), as_seq(k), vt.reshape(batch, seq // MOBA_BLOCK, d, MOBA_BLOCK),
              as_seq(km), qg, kg)
    h3, hn3, route, counts = _wo_router(o.reshape(t, d), h2, o_w_o[0].astype(BF16),
                                        o_ffn_norm[0][None, :], jnp.transpose(o_w_router[0]))
    out = _moe(h3, hn3, route, counts, o_w_gate[0], o_w_up[0], o_w_down[0])
    return out.reshape(batch, seq, d)
```

```python
import functools
import math

import jax
import jax.numpy as jnp
from jax import lax
from jax.experimental import pallas as pl
from jax.experimental.pallas import tpu as pltpu

F32 = jnp.float32
BF16 = jnp.bfloat16

D_MODEL = 1024
CONV_DIM = 512
CONV_KERNEL = 3
GMLP_DIM = 512
GMLP_GROUPS = 8
GMLP_GROUP_DIM = GMLP_DIM // GMLP_GROUPS
GMLP_CHUNK = 128
MIX_IN = 3 * CONV_DIM + 2 * GMLP_DIM
N_HEADS = 8
HEAD_DIM = 128
ROT_DIM = HEAD_DIM // 4
ROPE_THETA = 500000.0
MOBA_BLOCK = 256
MOBA_TOPK = 3
N_EXPERTS = 8
EPS = 1e-6

V7X_LANES = 128
ROW_TILE = 8
VMEM_LIMIT = 56 * 1024 * 1024

NEG_INF = float("-inf")
QK_SCALE = HEAD_DIM ** -0.5 * math.log2(math.e)
MASK_MARGIN = 256.0
M_INIT = -1e30
ONES_ROWS = 16


def _rms(x, g):
    return x * lax.rsqrt(jnp.mean(x * x, axis=-1, keepdims=True) + EPS) * g


def _dot(a, b):
    return jnp.dot(a, b, preferred_element_type=F32)


def _dot_nt(a, b):
    return lax.dot_general(a, b, (((1,), (1,)), ((), ())), preferred_element_type=F32)


def _split_bf16(x):
    hi = x.astype(BF16)
    lo = (x - hi.astype(F32)).astype(BF16)
    return hi, lo


def _mixer0_kernel(x_ref, xres_ref, g_ref, win_ref, convw_ref, lng_ref, lnb_ref, ws_ref, bias_ref,
                   wout_ref, g2_ref, h_ref, hn_ref, carry_ref, y_ref, buf0_ref, buf1_ref,
                   *, tm, tiles_per_seq, n_tiles):
    def produce(proj_ref):
        hn = _rms(x_ref[...], g_ref[...]).astype(BF16)
        proj_ref[...] = _dot(hn, win_ref[...])

    def consume(proj_ref):
        tile = pl.program_id(0) - 1
        a_h = proj_ref[:, 0:CONV_DIM]
        a_c = proj_ref[:, CONV_DIM:2 * CONV_DIM]
        a_b = proj_ref[:, 2 * CONV_DIM:3 * CONV_DIM]
        g_v = proj_ref[:, 3 * CONV_DIM + GMLP_DIM:]

        z = a_c * a_h
        prev = jnp.where(tile % tiles_per_seq == 0, 0.0, carry_ref[...])
        row = lax.broadcasted_iota(jnp.int32, (tm, CONV_DIM), 0)
        z1 = jnp.where(row == 0, prev[7:8, :], pltpu.roll(z, 1, 0))
        z2 = jnp.where(row == 0, prev[6:7, :],
                       jnp.where(row == 1, prev[7:8, :], pltpu.roll(z, 2, 0)))
        carry_ref[...] = z[tm - 8:tm, :]
        cw = convw_ref[...]
        y_a = a_b * (cw[0:1, :] * z2 + cw[1:2, :] * z1 + cw[2:3, :] * z)
        y_ref[:, 0:CONV_DIM] = y_a.astype(BF16)

        mu = jnp.mean(g_v, axis=-1, keepdims=True)
        cen = g_v - mu
        var = jnp.mean(cen * cen, axis=-1, keepdims=True)
        v = cen * lax.rsqrt(var + EPS) * lng_ref[...] + lnb_ref[...]
        tr = lax.broadcasted_iota(jnp.int32, (GMLP_CHUNK, GMLP_CHUNK), 0)
        tc = lax.broadcasted_iota(jnp.int32, (GMLP_CHUNK, GMLP_CHUNK), 1)
        causal = tc <= tr
        lo_half = tc < GMLP_GROUP_DIM
        n_chunks = tm // GMLP_CHUNK
        for j in range(GMLP_DIM // V7X_LANES):
            cols = slice(j * V7X_LANES, (j + 1) * V7X_LANES)
            w_pair = jnp.concatenate(
                [jnp.where(causal, ws_ref[2 * j], 0.0), jnp.where(causal, ws_ref[2 * j + 1], 0.0)],
                axis=1).astype(BF16)
            stacked = []
            for c in range(n_chunks):
                vj = v[c * GMLP_CHUNK:(c + 1) * GMLP_CHUNK, cols]
                stacked.append(jnp.concatenate(
                    [jnp.where(lo_half, vj, 0.0), jnp.where(lo_half, 0.0, vj)], axis=0))
            mixed = _dot(w_pair, jnp.concatenate(stacked, axis=1).astype(BF16))
            for c in range(n_chunks):
                rows = slice(c * GMLP_CHUNK, (c + 1) * GMLP_CHUNK)
                g_u = proj_ref[rows, 3 * CONV_DIM + j * V7X_LANES:3 * CONV_DIM + (j + 1) * V7X_LANES]
                m_c = mixed[:, c * V7X_LANES:(c + 1) * V7X_LANES] + bias_ref[:, cols]
                y_ref[rows, CONV_DIM + j * V7X_LANES:CONV_DIM + (j + 1) * V7X_LANES] = (
                    g_u * m_c).astype(BF16)

        h = xres_ref[...] + _dot(y_ref[...], wout_ref[...])
        h_ref[...] = h
        hn_ref[...] = _rms(h, g2_ref[...]).astype(BF16)

    @pl.when(pl.program_id(0) == 0)
    def _():
        carry_ref[...] = jnp.zeros_like(carry_ref)

    _skewed(n_tiles, produce, consume, (buf0_ref, buf1_ref))


def _mixer0(x, g, w_in, conv_w, ln_g, ln_b, w_s, bias_full, w_out, g2, *, seq, tm=512):
    t, d = x.shape
    n_tiles = t // tm
    const = lambda *shape: pl.BlockSpec(shape, lambda i: (0,) * len(shape))
    done = pl.BlockSpec((tm, d), lambda i: (_consumed(i), 0))
    return pl.pallas_call(
        functools.partial(_mixer0_kernel, tm=tm, tiles_per_seq=seq // tm, n_tiles=n_tiles),
        grid=(n_tiles + 1,),
        in_specs=[
            pl.BlockSpec((tm, d), lambda i: (_produced(i, n_tiles), 0)),
            done,
            const(1, d),
            const(d, MIX_IN),
            const(CONV_KERNEL, CONV_DIM),
            const(1, GMLP_DIM),
            const(1, GMLP_DIM),
            const(GMLP_GROUPS, GMLP_CHUNK, GMLP_CHUNK),
            const(GMLP_CHUNK, GMLP_DIM),
            const(CONV_DIM + GMLP_DIM, d),
            const(1, d),
        ],
        out_specs=[done, done],
        out_shape=[jax.ShapeDtypeStruct((t, d), F32), jax.ShapeDtypeStruct((t, d), BF16)],
        scratch_shapes=[pltpu.VMEM((8, CONV_DIM), F32), pltpu.VMEM((tm, CONV_DIM + GMLP_DIM), BF16),
                        pltpu.VMEM((tm, MIX_IN), F32), pltpu.VMEM((tm, MIX_IN), F32)],
        compiler_params=pltpu.CompilerParams(dimension_semantics=("arbitrary",),
                                             vmem_limit_bytes=VMEM_LIMIT),
        name="mixer0",
    )(x, x, g, w_in, conv_w, ln_g, ln_b, w_s, bias_full, w_out, g2)


def _swiglu(x, wg, wu, wd):
    a = _dot(x, wg.astype(BF16))
    u = _dot(x, wu.astype(BF16))
    hact = (a * jax.nn.sigmoid(a) * u).astype(BF16)
    return _dot(hact, wd.astype(BF16))


def _ffn_kernel(x_ref, res_ref, wg_ref, wu_ref, wd_ref, o_ref):
    @pl.when(pl.program_id(1) == 0)
    def _():
        o_ref[...] = res_ref[...]

    o_ref[...] += _swiglu(x_ref[...], wg_ref[...], wu_ref[...], wd_ref[...])


def _ffn(xn, res, w_gate, w_up, w_down, *, tm, tf):
    t, d = xn.shape
    f_dim = w_gate.shape[1]
    row = pl.BlockSpec((tm, d), lambda i, f: (i, 0))
    return pl.pallas_call(
        _ffn_kernel,
        grid=(t // tm, f_dim // tf),
        in_specs=[row, row,
                  pl.BlockSpec((d, tf), lambda i, f: (0, f)),
                  pl.BlockSpec((d, tf), lambda i, f: (0, f)),
                  pl.BlockSpec((tf, d), lambda i, f: (f, 0))],
        out_specs=row,
        out_shape=jax.ShapeDtypeStruct((t, d), F32),
        compiler_params=pltpu.CompilerParams(dimension_semantics=("parallel", "arbitrary"),
                                             vmem_limit_bytes=VMEM_LIMIT),
        name="dense_ffn",
    )(xn, res, w_gate, w_up, w_down)


def _skewed(n_tiles, produce, consume, bufs):
    i = pl.program_id(0)
    b0, b1 = bufs

    @pl.when(i == 0)
    def _():
        produce(b0)

    for parity, (dst, src) in enumerate(((b0, b1), (b1, b0))):
        @pl.when((i > 0) & (i < n_tiles) & (i % 2 == parity))
        def _():
            produce(dst)
            consume(src)

    @pl.when(i == n_tiles)
    def _():
        consume(bufs[(n_tiles - 1) % 2])


def _produced(i, n_tiles):
    return jnp.minimum(i, n_tiles - 1)


def _consumed(i):
    return jnp.maximum(i - 1, 0)


def _qkv_kernel(h_ref, g_ref, w_ref, qg_ref, kg_ref, cos_ref, sin_ref,
                q_ref, k_ref, v_ref, km_ref, buf0_ref, buf1_ref, *, tm, n_tiles):
    hd = N_HEADS * HEAD_DIM

    def produce(buf_ref):
        hn = _rms(h_ref[...], g_ref[...]).astype(BF16)
        buf_ref[...] = _dot(hn, w_ref[...])

    def consume(buf_ref):
        cos = cos_ref[...]
        sin = sin_ref[...]

        def rope(x):
            return x * cos + pltpu.roll(x, HEAD_DIM // 2, 1) * sin

        for c in range(tm // MOBA_BLOCK):
            v_ref[c] = jnp.transpose(
                buf_ref[c * MOBA_BLOCK:(c + 1) * MOBA_BLOCK, 2 * hd:]).astype(BF16)
        for h in range(N_HEADS):
            cols = slice(h * HEAD_DIM, (h + 1) * HEAD_DIM)
            q = rope(_rms(buf_ref[:, cols], qg_ref[...]))
            k = rope(_rms(buf_ref[:, hd + h * HEAD_DIM:hd + (h + 1) * HEAD_DIM], kg_ref[...]))
            q_ref[:, cols] = (q * QK_SCALE).astype(BF16)
            k_ref[:, cols] = k.astype(BF16)
            for c in range(tm // MOBA_BLOCK):
                km_ref[c, :, cols] = jnp.mean(
                    k[c * MOBA_BLOCK:(c + 1) * MOBA_BLOCK, :], axis=0, keepdims=True)

    _skewed(n_tiles, produce, consume, (buf0_ref, buf1_ref))


def _qkv(h, g, w, qg, kg, cos, sin, *, seq, tm=512):
    t, d = h.shape
    ns = seq // tm
    n_tiles = t // tm
    hd = N_HEADS * HEAD_DIM
    const = lambda *shape: pl.BlockSpec(shape, lambda i: (0,) * len(shape))
    row_spec = pl.BlockSpec((tm, hd), lambda i: (_consumed(i), 0))
    row_shape = jax.ShapeDtypeStruct((t, hd), BF16)
    table = pl.BlockSpec((tm, HEAD_DIM), lambda i: (_consumed(i) % ns, 0))
    return pl.pallas_call(
        functools.partial(_qkv_kernel, tm=tm, n_tiles=n_tiles),
        grid=(n_tiles + 1,),
        in_specs=[
            pl.BlockSpec((tm, d), lambda i: (_produced(i, n_tiles), 0)),
            const(1, d),
            const(d, 3 * hd),
            const(1, HEAD_DIM),
            const(1, HEAD_DIM),
            table,
            table,
        ],
        out_specs=[row_spec, row_spec,
                   pl.BlockSpec((tm // MOBA_BLOCK, hd, MOBA_BLOCK), lambda i: (_consumed(i), 0, 0)),
                   pl.BlockSpec((tm // MOBA_BLOCK, 1, hd), lambda i: (_consumed(i), 0, 0))],
        out_shape=[row_shape, row_shape,
                   jax.ShapeDtypeStruct((t // MOBA_BLOCK, hd, MOBA_BLOCK), BF16),
                   jax.ShapeDtypeStruct((t // MOBA_BLOCK, 1, hd), F32)],
        scratch_shapes=[pltpu.VMEM((tm, 3 * hd), F32), pltpu.VMEM((tm, 3 * hd), F32)],
        compiler_params=pltpu.CompilerParams(dimension_semantics=("arbitrary",),
                                             vmem_limit_bytes=VMEM_LIMIT),
        name="qkv",
    )(h, g, w, qg, kg, cos, sin)


def _attn_kernel(q_ref, k_ref, v_ref, km_ref, qg_ref, kg_ref, o_ref, qx_ref, acc_ref, m_ref,
                 s0_ref, s1_ref, *, n_blocks):
    i = pl.program_id(1)
    bs = MOBA_BLOCK
    hd = HEAD_DIM
    bound = QK_SCALE * hd * jnp.max(jnp.abs(qg_ref[...]), axis=-1, keepdims=True) * jnp.max(
        jnp.abs(kg_ref[...]), axis=-1, keepdims=True)
    big = 2.1 * bound + MASK_MARGIN
    blk = lax.broadcasted_iota(jnp.int32, (n_blocks, bs), 0)
    past = blk < i
    key = lax.broadcasted_iota(jnp.int32, (bs, bs), 0)
    qry = lax.broadcasted_iota(jnp.int32, (bs, bs), 1)
    causal = key <= qry
    ones_rows = jnp.ones((ONES_ROWS, bs), BF16)
    own = pl.multiple_of(i * bs, bs)

    for h in range(N_HEADS):
        cols = slice(h * hd, (h + 1) * hd)
        q = q_ref[0, :, cols]
        km_hi, km_lo = _split_bf16(km_ref[0, :, cols])
        gate_t = _dot_nt(km_hi, q) + _dot_nt(km_lo, q)
        cur = jnp.where(past, gate_t, NEG_INF)
        sel = jnp.zeros((n_blocks, bs), F32)
        for _ in range(min(MOBA_TOPK, n_blocks)):
            top = jnp.max(cur, axis=0, keepdims=True)
            idx = jnp.min(jnp.where(cur == top, blk, n_blocks), axis=0, keepdims=True)
            pick = blk == idx
            sel = jnp.where(pick & past, 1.0, sel)
            cur = jnp.where(pick, NEG_INF, cur)
        bias_t = jnp.where(sel > 0.5, 0.0, -big)
        bias_t = jnp.concatenate([bias_t, jnp.zeros((hd - n_blocks, bs), F32)], axis=0)
        qx_ref[h] = jnp.concatenate([q, jnp.transpose(bias_t).astype(BF16)], axis=1)

        s = _dot_nt(k_ref[0, pl.ds(own, bs), cols], q)
        s0_ref[h] = jnp.where(causal, s, NEG_INF)
        acc_ref[h] = jnp.zeros(acc_ref.shape[1:], F32)
        m_ref[h] = jnp.full(m_ref.shape[1:], M_INIT, F32)

    def consume(h, s_ref, v_blk):
        cols = slice(h * hd, (h + 1) * hd)
        s = s_ref[h]
        m_old = m_ref[h]
        m_new = jnp.maximum(m_old, jnp.max(s, axis=0, keepdims=True))
        p = jnp.exp2(s - m_new)
        vx = jnp.concatenate([v_ref[0, v_blk, cols, :], ones_rows], axis=0)
        acc_ref[h] = jnp.exp2(m_old - m_new) * acc_ref[h] + _dot(vx, p.astype(BF16))
        m_ref[h] = m_new

    def step(j, s_in, s_out):
        start = pl.multiple_of(j * bs, bs)
        onehot = (lax.broadcasted_iota(jnp.int32, (bs, hd), 1) == j).astype(BF16)
        v_blk = jnp.where(j == 0, i, j - 1)
        for h in range(N_HEADS):
            cols = slice(h * hd, (h + 1) * hd)
            kx = jnp.concatenate([k_ref[0, pl.ds(start, bs), cols], onehot], axis=1)
            s_out[h] = _dot_nt(kx, qx_ref[h])
            consume(h, s_in, v_blk)

    def body(j, carry):
        @pl.when(j % 2 == 0)
        def _():
            step(j, s0_ref, s1_ref)

        @pl.when(j % 2 == 1)
        def _():
            step(j, s1_ref, s0_ref)

        return carry

    lax.fori_loop(0, i, body, 0)

    def finish(s_ref):
        v_blk = jnp.where(i == 0, i, i - 1)
        for h in range(N_HEADS):
            consume(h, s_ref, v_blk)
            a = acc_ref[h]
            o_t = a[:hd, :] / a[hd:hd + 1, :]
            o_ref[0, :, h * hd:(h + 1) * hd] = jnp.transpose(o_t).astype(BF16)

    @pl.when(i % 2 == 0)
    def _():
        finish(s0_ref)

    @pl.when(i % 2 == 1)
    def _():
        finish(s1_ref)


def _attn(q, k, vt, km, qg, kg):
    batch, seq, d = q.shape
    nb = seq // MOBA_BLOCK
    tile = pl.BlockSpec((1, MOBA_BLOCK, d), lambda b, i: (b, i, 0))
    gain = pl.BlockSpec((1, HEAD_DIM), lambda b, i: (0, 0))
    return pl.pallas_call(
        functools.partial(_attn_kernel, n_blocks=nb),
        grid=(batch, nb),
        in_specs=[tile,
                  pl.BlockSpec((1, seq, d), lambda b, i: (b, 0, 0)),
                  pl.BlockSpec((1, nb, d, MOBA_BLOCK), lambda b, i: (b, 0, 0, 0)),
                  pl.BlockSpec((1, nb, d), lambda b, i: (b, 0, 0)), gain, gain],
        out_specs=tile,
        out_shape=jax.ShapeDtypeStruct((batch, seq, d), BF16),
        scratch_shapes=[
            pltpu.VMEM((N_HEADS, MOBA_BLOCK, 2 * HEAD_DIM), BF16),
            pltpu.VMEM((N_HEADS, HEAD_DIM + ONES_ROWS, MOBA_BLOCK), F32),
            pltpu.VMEM((N_HEADS, 1, MOBA_BLOCK), F32),
            pltpu.VMEM((N_HEADS, MOBA_BLOCK, MOBA_BLOCK), F32),
            pltpu.VMEM((N_HEADS, MOBA_BLOCK, MOBA_BLOCK), F32),
        ],
        compiler_params=pltpu.CompilerParams(dimension_semantics=("parallel", "arbitrary"),
                                             vmem_limit_bytes=VMEM_LIMIT),
        name="moba_attn",
    )(q, k, vt, km, qg, kg)


def _wo_router_kernel(o_ref, res_ref, wo_ref, g_ref, wrt_ref, h_ref, hn_ref, route_ref, counts_ref,
                      cnt_ref, buf0_ref, buf1_ref, *, tm, n_tiles):
    def produce(buf_ref):
        h = res_ref[...] + _dot(o_ref[...], wo_ref[...])
        h_ref[...] = h
        buf_ref[...] = h

    def consume(buf_ref):
        hn = _rms(buf_ref[...], g_ref[...])
        _store_rows_as_tiles(hn_ref, hn)

        x_hi, x_lo = _split_bf16(hn)
        w_hi, w_lo = _split_bf16(wrt_ref[...])
        part = _dot_nt(jnp.concatenate([w_hi, w_lo], axis=0), x_hi)
        logits = part[:N_EXPERTS] + part[N_EXPERTS:] + _dot_nt(w_hi, x_lo)
        e_id = lax.broadcasted_iota(jnp.int32, (N_EXPERTS, tm), 0)
        m1 = jnp.max(logits, axis=0, keepdims=True)
        i1 = jnp.min(jnp.where(logits == m1, e_id, N_EXPERTS), axis=0, keepdims=True)
        pick1 = e_id == i1
        rest = jnp.where(pick1, NEG_INF, logits)
        m2 = jnp.max(rest, axis=0, keepdims=True)
        i2 = jnp.min(jnp.where(rest == m2, e_id, N_EXPERTS), axis=0, keepdims=True)
        pick2 = e_id == i2
        e2 = jnp.exp(m2 - m1)
        denom = 1.0 + e2

        chosen = jnp.where(pick1 | pick2, 1.0, 0.0)
        tr = lax.broadcasted_iota(jnp.int32, (tm, tm), 0)
        tc = lax.broadcasted_iota(jnp.int32, (tm, tm), 1)
        earlier = jnp.where(tr < tc, 1.0, 0.0).astype(BF16)
        count = cnt_ref[...]
        rank = _dot(chosen.astype(BF16), earlier) + count
        count = count + jnp.sum(chosen, axis=1, keepdims=True)
        cnt_ref[...] = count
        counts_ref[...] = count
        rank1 = jnp.sum(jnp.where(pick1, rank, 0.0), axis=0, keepdims=True)
        rank2 = jnp.sum(jnp.where(pick2, rank, 0.0), axis=0, keepdims=True)
        fields = [i1.astype(F32), i2.astype(F32), 1.0 / denom, e2 / denom, rank1, rank2]
        route = jnp.zeros((N_EXPERTS, tm), F32)
        for slot, val in enumerate(fields):
            route = jnp.where(e_id == slot, val, route)
        route_ref[...] = route

    @pl.when(pl.program_id(0) == 0)
    def _():
        cnt_ref[...] = jnp.zeros_like(cnt_ref)

    _skewed(n_tiles, produce, consume, (buf0_ref, buf1_ref))


def _wo_router(o, res, w_o, g, w_r_t, *, tm=512):
    t, d = res.shape
    n_tiles = t // tm
    const = lambda *shape: pl.BlockSpec(shape, lambda i: (0,) * len(shape))
    new = pl.BlockSpec((tm, d), lambda i: (_produced(i, n_tiles), 0))
    return pl.pallas_call(
        functools.partial(_wo_router_kernel, tm=tm, n_tiles=n_tiles),
        grid=(n_tiles + 1,),
        in_specs=[new, new, const(d, d), const(1, d), const(N_EXPERTS, d)],
        out_specs=[new,
                   pl.BlockSpec((tm * ROW_TILE, V7X_LANES), lambda i: (_consumed(i), 0)),
                   pl.BlockSpec((N_EXPERTS, tm), lambda i: (0, _consumed(i))),
                   const(N_EXPERTS, 1)],
        out_shape=[jax.ShapeDtypeStruct((t, d), F32),
                   jax.ShapeDtypeStruct((t * ROW_TILE, V7X_LANES), F32),
                   jax.ShapeDtypeStruct((N_EXPERTS, t), F32),
                   jax.ShapeDtypeStruct((N_EXPERTS, 1), F32)],
        scratch_shapes=[pltpu.VMEM((N_EXPERTS, 1), F32), pltpu.VMEM((tm, d), F32),
                        pltpu.VMEM((tm, d), F32)],
        compiler_params=pltpu.CompilerParams(dimension_semantics=("arbitrary",),
                                             vmem_limit_bytes=VMEM_LIMIT),
        name="wo_router",
    )(o, res, w_o, g, w_r_t)


def _store_rows_as_tiles(dst_ref, x):
    n = x.shape[0]
    for s in range(ROW_TILE):
        dst_ref[pl.ds(s, n, stride=ROW_TILE), :] = x[:, s * V7X_LANES:(s + 1) * V7X_LANES]


def _load_lane_tile(src_ref, n, s):
    return src_ref[pl.ds(s, n, stride=ROW_TILE), :]


def _row_copy(src_ref, src_row, dst_ref, dst_row, sem):
    src = src_ref.at[pl.ds(pl.multiple_of(src_row * ROW_TILE, ROW_TILE), ROW_TILE), :]
    dst = dst_ref.at[pl.ds(pl.multiple_of(dst_row * ROW_TILE, ROW_TILE), ROW_TILE), :]
    return pltpu.make_async_copy(src, dst, sem)


def _moe_scatter_kernel(meta_ref, pos_ref, x_ref, xs_ref, zero_ref, sem, zero_sem, *, tm, tg):
    zr = zero_ref.shape[0] // ROW_TILE
    n_rows = xs_ref.shape[0] // ROW_TILE
    total = meta_ref[N_EXPERTS - 1]

    def zero_tile(base):
        base = pl.multiple_of(base * ROW_TILE, tg * ROW_TILE)
        return [pltpu.make_async_copy(
            zero_ref, xs_ref.at[pl.ds(base + c * zr * ROW_TILE, zr * ROW_TILE), :], zero_sem)
            for c in range(tg // zr)]

    @pl.when(pl.program_id(0) == 0)
    def _():
        zero_ref[...] = jnp.zeros_like(zero_ref)
        for wait in (False, True):
            for e in range(N_EXPERTS):
                @pl.when(meta_ref[N_EXPERTS + e] > 0)
                def _():
                    for cp in zero_tile(meta_ref[e] - tg):
                        cp.wait() if wait else cp.start()
            for k in range(N_EXPERTS - 1):
                @pl.when(total + k * tg < n_rows)
                def _():
                    for cp in zero_tile(total + k * tg):
                        cp.wait() if wait else cp.start()

    def start(r, carry):
        _row_copy(x_ref, r, xs_ref, pos_ref[0, 0, r], sem).start(priority=0)
        _row_copy(x_ref, r, xs_ref, pos_ref[0, 0, tm + r], sem).start(priority=1)
        return carry

    lax.fori_loop(0, tm, start, 0, unroll=8)

    def wait(r, carry):
        _row_copy(x_ref, 0, xs_ref, 0, sem).wait()
        _row_copy(x_ref, 0, xs_ref, 0, sem).wait()
        return carry

    lax.fori_loop(0, tm, wait, 0, unroll=8)


def _moe_scatter(meta, pos, x, n_rows, *, tg, tm=256, zero_rows=256):
    t = x.shape[0] // ROW_TILE
    return pl.pallas_call(
        functools.partial(_moe_scatter_kernel, tm=tm, tg=tg),
        grid_spec=pltpu.PrefetchScalarGridSpec(
            num_scalar_prefetch=1,
            grid=(t // tm,),
            in_specs=[pl.BlockSpec((1, 1, 2 * tm), lambda i, meta: (i, 0, 0), memory_space=pltpu.SMEM),
                      pl.BlockSpec((tm * ROW_TILE, V7X_LANES), lambda i, meta: (i, 0))],
            out_specs=pl.BlockSpec(memory_space=pl.ANY),
            scratch_shapes=[pltpu.VMEM((zero_rows * ROW_TILE, V7X_LANES), F32),
                            pltpu.SemaphoreType.DMA(()), pltpu.SemaphoreType.DMA(())]),
        out_shape=jax.ShapeDtypeStruct((n_rows * ROW_TILE, V7X_LANES), F32),
        compiler_params=pltpu.CompilerParams(dimension_semantics=("arbitrary",),
                                             vmem_limit_bytes=VMEM_LIMIT),
        name="moe_scatter",
    )(meta, pos, x)


def _moe_gemm_kernel(te_ref, used_ref, x_ref, wg_ref, wu_ref, wd_ref, y_ref, xb_ref, acc_ref):
    del te_ref
    f = pl.program_id(1)
    tg, d = acc_ref.shape
    live = pl.program_id(0) < used_ref[0]

    @pl.when(jnp.logical_not(live) & (f == 0))
    def _():
        y_ref[...] = jnp.zeros_like(y_ref)

    first = f == 0
    last = f == pl.num_programs(1) - 1

    @pl.when(live & first)
    def _():
        x = jnp.concatenate(
            [_load_lane_tile(x_ref, tg, s).astype(BF16) for s in range(d // V7X_LANES)], axis=1)
        xb_ref[...] = x
        acc_ref[...] = _swiglu(x, wg_ref[0], wu_ref[0], wd_ref[0])

    @pl.when(live & jnp.logical_not(first | last))
    def _():
        acc_ref[...] += _swiglu(xb_ref[...], wg_ref[0], wu_ref[0], wd_ref[0])

    @pl.when(live & last)
    def _():
        _store_rows_as_tiles(y_ref, acc_ref[...] + _swiglu(xb_ref[...], wg_ref[0], wu_ref[0], wd_ref[0]))


def _moe_gemm(tile_expert, n_used, xs, w_gate, w_up, w_down, *, tg, tf):
    n_rows = xs.shape[0] // ROW_TILE
    d, f_dim = w_gate.shape[-2:]
    row_block = (tg * ROW_TILE, V7X_LANES)
    x_row = pl.BlockSpec(row_block, lambda i, f, te, used: (jnp.minimum(i, used[0] - 1), 0))
    y_row = pl.BlockSpec(row_block, lambda i, f, te, used: (i, 0))
    w_tile = lambda i, f, used: jnp.where(i < used[0], f, f_dim // tf - 1)
    return pl.pallas_call(
        _moe_gemm_kernel,
        grid_spec=pltpu.PrefetchScalarGridSpec(
            num_scalar_prefetch=2,
            grid=(n_rows // tg, f_dim // tf),
            in_specs=[x_row,
                      pl.BlockSpec((1, d, tf), lambda i, f, te, used: (te[i], 0, w_tile(i, f, used))),
                      pl.BlockSpec((1, d, tf), lambda i, f, te, used: (te[i], 0, w_tile(i, f, used))),
                      pl.BlockSpec((1, tf, d), lambda i, f, te, used: (te[i], w_tile(i, f, used), 0))],
            out_specs=y_row,
            scratch_shapes=[pltpu.VMEM((tg, d), BF16), pltpu.VMEM((tg, d), F32)]),
        out_shape=jax.ShapeDtypeStruct((n_rows * ROW_TILE, V7X_LANES), F32),
        compiler_params=pltpu.CompilerParams(dimension_semantics=("arbitrary", "arbitrary"),
                                             vmem_limit_bytes=VMEM_LIMIT),
        name="moe_gemm",
    )(tile_expert, n_used, xs, w_gate, w_up, w_down)


def _moe_combine_kernel(pos_ref, next_pos_ref, h_ref, prob_ref, ys_ref, o_ref, buf_ref, sem, *, tm):
    step = pl.program_id(0)
    slot = step % 2

    def fetch(table_ref, into):
        def start(r, carry):
            _row_copy(ys_ref, table_ref[0, 0, r], buf_ref.at[into, 0], r, sem.at[into]).start(
                priority=0)
            _row_copy(ys_ref, table_ref[0, 0, tm + r], buf_ref.at[into, 1], r, sem.at[into]).start(
                priority=1)
            return carry

        lax.fori_loop(0, tm, start, 0, unroll=8)

    @pl.when(step == 0)
    def _():
        fetch(pos_ref, 0)

    @pl.when(step + 1 < pl.num_programs(0))
    def _():
        fetch(next_pos_ref, 1 - slot)

    def wait(r, carry):
        _row_copy(ys_ref, 0, buf_ref.at[slot, 0], 0, sem.at[slot]).wait()
        _row_copy(ys_ref, 0, buf_ref.at[slot, 1], 0, sem.at[slot]).wait()
        return carry

    lax.fori_loop(0, tm, wait, 0, unroll=8)
    p1 = prob_ref[:, 0:1]
    p2 = prob_ref[:, 1:2]
    for s in range(o_ref.shape[1] // V7X_LANES):
        cols = slice(s * V7X_LANES, (s + 1) * V7X_LANES)
        o_ref[:, cols] = (h_ref[:, cols] + p1 * _load_lane_tile(buf_ref.at[slot, 0], tm, s)
                          + p2 * _load_lane_tile(buf_ref.at[slot, 1], tm, s))


def _moe_combine(pos, h, probs, ys, *, tm):
    t, d = h.shape
    n_steps = t // tm
    table = lambda index: pl.BlockSpec((1, 1, 2 * tm), index, memory_space=pltpu.SMEM)
    return pl.pallas_call(
        functools.partial(_moe_combine_kernel, tm=tm),
        grid=(n_steps,),
        in_specs=[table(lambda i: (i, 0, 0)),
                  table(lambda i: (jnp.minimum(i + 1, n_steps - 1), 0, 0)),
                  pl.BlockSpec((tm, d), lambda i: (i, 0)),
                  pl.BlockSpec((tm, 2), lambda i: (i, 0)),
                  pl.BlockSpec(memory_space=pl.ANY)],
        out_specs=pl.BlockSpec((tm, d), lambda i: (i, 0)),
        out_shape=jax.ShapeDtypeStruct((t, d), F32),
        scratch_shapes=[pltpu.VMEM((2, 2, tm * ROW_TILE, V7X_LANES), F32),
                        pltpu.SemaphoreType.DMA((2,))],
        compiler_params=pltpu.CompilerParams(dimension_semantics=("arbitrary",),
                                             vmem_limit_bytes=VMEM_LIMIT),
        name="moe_combine",
    )(pos, pos, h, probs, ys)


def _moe(h, hn, route, counts, w_gate, w_up, w_down, *, tg=1024, tf=512, tm=512):
    t, d = h.shape
    n_tiles = (2 * t + N_EXPERTS * (tg - 1)) // tg
    counts = counts[:, 0].astype(jnp.int32)
    padded = (counts + tg - 1) // tg * tg
    e_ids = jnp.arange(N_EXPERTS, dtype=jnp.int32)
    ends = jnp.sum(jnp.where(e_ids[:, None] <= e_ids[None, :], padded[:, None], 0), axis=0)
    starts = ends - padded
    experts = route[0:2].astype(jnp.int32)
    start_of = jnp.sum(jnp.where(experts[:, :, None] == e_ids, starts, 0), axis=-1)
    pos = start_of + route[4:6].astype(jnp.int32)
    pos = jnp.transpose(pos.reshape(2, t // tm, tm), (1, 0, 2)).reshape(t // tm, 1, 2 * tm)
    n_used = ends[-1:] // tg
    tile_start = jnp.minimum(jnp.arange(n_tiles, dtype=jnp.int32), n_used - 1) * tg
    tile_expert = jnp.sum((ends[None, :] <= tile_start[:, None]).astype(jnp.int32), axis=1)
    xs = _moe_scatter(jnp.concatenate([ends, padded]), pos, hn, n_tiles * tg, tg=tg, tm=tm)
    ys = _moe_gemm(tile_expert, n_used, xs, w_gate, w_up, w_down, tg=tg, tf=tf)
    return _moe_combine(pos, h, jnp.transpose(route[2:4]), ys, tm=tm)


def _rotary_layout(a):
    half = ROT_DIM // 2
    split = HEAD_DIM // 2 - half
    x1, x2, rest = a[..., :half], a[..., half:ROT_DIM], a[..., ROT_DIM:]
    return jnp.concatenate([x1, rest[..., :split], x2, rest[..., split:]], axis=-1)


def _rope_tables(seq):
    half = ROT_DIM // 2
    inv_freq = ROPE_THETA ** (-jnp.arange(half, dtype=F32) / half)
    ang = jnp.arange(seq, dtype=F32)[:, None] * inv_freq[None, :]
    cos = jnp.cos(ang)
    sin = jnp.sin(ang)
    rest = HEAD_DIM - ROT_DIM
    cos_full = jnp.concatenate([cos, cos, jnp.ones((seq, rest), F32)], axis=-1)
    sin_signed = jnp.concatenate([-sin, sin, jnp.zeros((seq, rest), F32)], axis=-1)
    return _rotary_layout(cos_full), _rotary_layout(sin_signed)


def _qkv_weights(w_qkv):
    d = w_qkv.shape[0]
    hd = N_HEADS * HEAD_DIM
    qk = _rotary_layout(w_qkv[:, :2 * hd].reshape(d, 2 * N_HEADS, HEAD_DIM)).reshape(d, 2 * hd)
    return jnp.concatenate([qk, w_qkv[:, 2 * hd:]], axis=1).astype(BF16)


def kernel(x, e_mix_norm, e_w_in, e_conv_w, e_gmlp_ln_g, e_gmlp_ln_b, e_w_spatial, e_b_spatial, e_w_out, e_ffn_norm, e_w_gate, e_w_up, e_w_down, o_mix_norm, o_w_qkv, o_q_norm, o_k_norm, o_w_o, o_ffn_norm, o_w_router, o_w_gate, o_w_up, o_w_down):
    batch, seq, d = x.shape
    t = batch * seq
    xf = x.reshape(t, d)

    bias_full = jnp.repeat(jnp.transpose(e_b_spatial[0]), GMLP_GROUP_DIM, axis=1)
    h1, hn1 = _mixer0(
        xf, e_mix_norm[0][None, :], e_w_in[0].astype(BF16), e_conv_w[0],
        e_gmlp_ln_g[0][None, :], e_gmlp_ln_b[0][None, :], e_w_spatial[0], bias_full,
        e_w_out[0].astype(BF16), e_ffn_norm[0][None, :], seq=seq)
    h2 = _ffn(hn1, h1, e_w_gate[0].astype(BF16), e_w_up[0].astype(BF16), e_w_down[0].astype(BF16),
              tm=1024, tf=1408)

    cos, sin = _rope_tables(seq)
    qg = _rotary_layout(o_q_norm[0][None, :])
    kg = _rotary_layout(o_k_norm[0][None, :])
    q, k, vt, km = _qkv(h2, o_mix_norm[0][None, :], _qkv_weights(o_w_qkv[0]), qg, kg, cos, sin, seq=seq)
    as_seq = lambda a: a.reshape(batch, -1, d)
    o = _attn(as_seq(q), as_seq(k), vt.reshape(batch, seq // MOBA_BLOCK, d, MOBA_BLOCK),
              as_seq(km), qg, kg)
    h3, hn3, route, counts = _wo_router(o.reshape(t, d), h2, o_w_o[0].astype(BF16),
                                        o_ffn_norm[0][None, :], jnp.transpose(o_w_router[0]))
    out = _moe(h3, hn3, route, counts, o_w_gate[0], o_w_up[0], o_w_down[0])
    return out.reshape(batch, seq, d)
```

```python
import functools
import math

import jax
import jax.numpy as jnp
from jax import lax
from jax.experimental import pallas as pl
from jax.experimental.pallas import tpu as pltpu

F32 = jnp.float32
BF16 = jnp.bfloat16

D_MODEL = 1024
CONV_DIM = 512
CONV_KERNEL = 3
GMLP_DIM = 512
GMLP_GROUPS = 8
GMLP_GROUP_DIM = GMLP_DIM // GMLP_GROUPS
GMLP_CHUNK = 128
MIX_IN = 3 * CONV_DIM + 2 * GMLP_DIM
N_HEADS = 8
HEAD_DIM = 128
ROT_DIM = HEAD_DIM // 4
ROPE_THETA = 500000.0
MOBA_BLOCK = 256
MOBA_TOPK = 3
N_EXPERTS = 8
EPS = 1e-6

V7X_LANES = 128
ROW_TILE = 8
VMEM_LIMIT = 56 * 1024 * 1024

NEG_INF = float("-inf")
QK_SCALE = HEAD_DIM ** -0.5 * math.log2(math.e)
MASK_MARGIN = 256.0
M_INIT = -1e30
ONES_ROWS = 16


def _rms(x, g):
    return x * lax.rsqrt(jnp.mean(x * x, axis=-1, keepdims=True) + EPS) * g


def _dot(a, b):
    return jnp.dot(a, b, preferred_element_type=F32)


def _dot_nt(a, b):
    return lax.dot_general(a, b, (((1,), (1,)), ((), ())), preferred_element_type=F32)


def _split_bf16(x):
    hi = x.astype(BF16)
    lo = (x - hi.astype(F32)).astype(BF16)
    return hi, lo


def _mixer0_kernel(x_ref, xres_ref, g_ref, win_ref, convw_ref, lng_ref, lnb_ref, ws_ref, bias_ref,
                   wout_ref, g2_ref, h_ref, hn_ref, carry_ref, y_ref, buf0_ref, buf1_ref,
                   *, tm, tiles_per_seq, n_tiles):
    def produce(proj_ref):
        hn = _rms(x_ref[...], g_ref[...]).astype(BF16)
        proj_ref[...] = _dot(hn, win_ref[...])

    def consume(proj_ref):
        tile = pl.program_id(0) - 1
        a_h = proj_ref[:, 0:CONV_DIM]
        a_c = proj_ref[:, CONV_DIM:2 * CONV_DIM]
        a_b = proj_ref[:, 2 * CONV_DIM:3 * CONV_DIM]
        g_v = proj_ref[:, 3 * CONV_DIM + GMLP_DIM:]

        z = a_c * a_h
        prev = jnp.where(tile % tiles_per_seq == 0, 0.0, carry_ref[...])
        row = lax.broadcasted_iota(jnp.int32, (tm, CONV_DIM), 0)
        z1 = jnp.where(row == 0, prev[7:8, :], pltpu.roll(z, 1, 0))
        z2 = jnp.where(row == 0, prev[6:7, :],
                       jnp.where(row == 1, prev[7:8, :], pltpu.roll(z, 2, 0)))
        carry_ref[...] = z[tm - 8:tm, :]
        cw = convw_ref[...]
        y_a = a_b * (cw[0:1, :] * z2 + cw[1:2, :] * z1 + cw[2:3, :] * z)
        y_ref[:, 0:CONV_DIM] = y_a.astype(BF16)

        mu = jnp.mean(g_v, axis=-1, keepdims=True)
        cen = g_v - mu
        var = jnp.mean(cen * cen, axis=-1, keepdims=True)
        v = cen * lax.rsqrt(var + EPS) * lng_ref[...] + lnb_ref[...]
        tr = lax.broadcasted_iota(jnp.int32, (GMLP_CHUNK, GMLP_CHUNK), 0)
        tc = lax.broadcasted_iota(jnp.int32, (GMLP_CHUNK, GMLP_CHUNK), 1)
        causal = tc <= tr
        lo_half = tc < GMLP_GROUP_DIM
        n_chunks = tm // GMLP_CHUNK
        for j in range(GMLP_DIM // V7X_LANES):
            cols = slice(j * V7X_LANES, (j + 1) * V7X_LANES)
            w_pair = jnp.concatenate(
                [jnp.where(causal, ws_ref[2 * j], 0.0), jnp.where(causal, ws_ref[2 * j + 1], 0.0)],
                axis=1).astype(BF16)
            stacked = []
            for c in range(n_chunks):
                vj = v[c * GMLP_CHUNK:(c + 1) * GMLP_CHUNK, cols]
                stacked.append(jnp.concatenate(
                    [jnp.where(lo_half, vj, 0.0), jnp.where(lo_half, 0.0, vj)], axis=0))
            mixed = _dot(w_pair, jnp.concatenate(stacked, axis=1).astype(BF16))
            for c in range(n_chunks):
                rows = slice(c * GMLP_CHUNK, (c + 1) * GMLP_CHUNK)
                g_u = proj_ref[rows, 3 * CONV_DIM + j * V7X_LANES:3 * CONV_DIM + (j + 1) * V7X_LANES]
                m_c = mixed[:, c * V7X_LANES:(c + 1) * V7X_LANES] + bias_ref[:, cols]
                y_ref[rows, CONV_DIM + j * V7X_LANES:CONV_DIM + (j + 1) * V7X_LANES] = (
                    g_u * m_c).astype(BF16)

        h = xres_ref[...] + _dot(y_ref[...], wout_ref[...])
        h_ref[...] = h
        hn_ref[...] = _rms(h, g2_ref[...]).astype(BF16)

    @pl.when(pl.program_id(0) == 0)
    def _():
        carry_ref[...] = jnp.zeros_like(carry_ref)

    _skewed(n_tiles, produce, consume, (buf0_ref, buf1_ref))


def _mixer0(x, g, w_in, conv_w, ln_g, ln_b, w_s, bias_full, w_out, g2, *, seq, tm=512):
    t, d = x.shape
    n_tiles = t // tm
    const = lambda *shape: pl.BlockSpec(shape, lambda i: (0,) * len(shape))
    done = pl.BlockSpec((tm, d), lambda i: (_consumed(i), 0))
    return pl.pallas_call(
        functools.partial(_mixer0_kernel, tm=tm, tiles_per_seq=seq // tm, n_tiles=n_tiles),
        grid=(n_tiles + 1,),
        in_specs=[
            pl.BlockSpec((tm, d), lambda i: (_produced(i, n_tiles), 0)),
            done,
            const(1, d),
            const(d, MIX_IN),
            const(CONV_KERNEL, CONV_DIM),
            const(1, GMLP_DIM),
            const(1, GMLP_DIM),
            const(GMLP_GROUPS, GMLP_CHUNK, GMLP_CHUNK),
            const(GMLP_CHUNK, GMLP_DIM),
            const(CONV_DIM + GMLP_DIM, d),
            const(1, d),
        ],
        out_specs=[done, done],
        out_shape=[jax.ShapeDtypeStruct((t, d), F32), jax.ShapeDtypeStruct((t, d), BF16)],
        scratch_shapes=[pltpu.VMEM((8, CONV_DIM), F32), pltpu.VMEM((tm, CONV_DIM + GMLP_DIM), BF16),
                        pltpu.VMEM((tm, MIX_IN), F32), pltpu.VMEM((tm, MIX_IN), F32)],
        compiler_params=pltpu.CompilerParams(dimension_semantics=("arbitrary",),
                                             vmem_limit_bytes=VMEM_LIMIT),
        name="mixer0",
    )(x, x, g, w_in, conv_w, ln_g, ln_b, w_s, bias_full, w_out, g2)


def _swiglu(x, wg, wu, wd):
    a = _dot(x, wg.astype(BF16))
    u = _dot(x, wu.astype(BF16))
    hact = (a * jax.nn.sigmoid(a) * u).astype(BF16)
    return _dot(hact, wd.astype(BF16))


def _ffn_kernel(x_ref, res_ref, wg_ref, wu_ref, wd_ref, o_ref):
    o_ref[...] = res_ref[...] + _swiglu(x_ref[...], wg_ref[...], wu_ref[...], wd_ref[...])


def _ffn(xn, res, w_gate, w_up, w_down, *, tm):
    t, d = xn.shape
    f_dim = w_gate.shape[1]
    row = pl.BlockSpec((tm, d), lambda i: (i, 0))
    resident = lambda shape: pl.BlockSpec(shape, lambda i: (0, 0), pipeline_mode=pl.Buffered(1))
    return pl.pallas_call(
        _ffn_kernel,
        grid=(t // tm,),
        in_specs=[row, row, resident((d, f_dim)), resident((d, f_dim)), resident((f_dim, d))],
        out_specs=row,
        out_shape=jax.ShapeDtypeStruct((t, d), F32),
        compiler_params=pltpu.CompilerParams(dimension_semantics=("parallel",),
                                             vmem_limit_bytes=VMEM_LIMIT),
        name="dense_ffn",
    )(xn, res, w_gate, w_up, w_down)


def _skewed(n_tiles, produce, consume, bufs):
    i = pl.program_id(0)
    b0, b1 = bufs

    @pl.when(i == 0)
    def _():
        produce(b0)

    for parity, (dst, src) in enumerate(((b0, b1), (b1, b0))):
        @pl.when((i > 0) & (i < n_tiles) & (i % 2 == parity))
        def _():
            produce(dst)
            consume(src)

    @pl.when(i == n_tiles)
    def _():
        consume(bufs[(n_tiles - 1) % 2])


def _produced(i, n_tiles):
    return jnp.minimum(i, n_tiles - 1)


def _consumed(i):
    return jnp.maximum(i - 1, 0)


def _qkv_kernel(h_ref, g_ref, w_ref, qg_ref, kg_ref, cos_ref, sin_ref,
                q_ref, k_ref, v_ref, km_ref, buf0_ref, buf1_ref, *, tm, n_tiles):
    hd = N_HEADS * HEAD_DIM

    def produce(buf_ref):
        hn = _rms(h_ref[...], g_ref[...]).astype(BF16)
        buf_ref[...] = _dot(hn, w_ref[...])

    def consume(buf_ref):
        cos = cos_ref[...]
        sin = sin_ref[...]

        def rope(x):
            return x * cos + pltpu.roll(x, HEAD_DIM // 2, 1) * sin

        for c in range(tm // MOBA_BLOCK):
            v_ref[c] = jnp.transpose(
                buf_ref[c * MOBA_BLOCK:(c + 1) * MOBA_BLOCK, 2 * hd:]).astype(BF16)
        for h in range(N_HEADS):
            cols = slice(h * HEAD_DIM, (h + 1) * HEAD_DIM)
            q = rope(_rms(buf_ref[:, cols], qg_ref[...]))
            k = rope(_rms(buf_ref[:, hd + h * HEAD_DIM:hd + (h + 1) * HEAD_DIM], kg_ref[...]))
            q_ref[:, cols] = (q * QK_SCALE).astype(BF16)
            k_ref[:, cols] = k.astype(BF16)
            for c in range(tm // MOBA_BLOCK):
                km_ref[c, :, cols] = jnp.mean(
                    k[c * MOBA_BLOCK:(c + 1) * MOBA_BLOCK, :], axis=0, keepdims=True)

    _skewed(n_tiles, produce, consume, (buf0_ref, buf1_ref))


def _qkv(h, g, w, qg, kg, cos, sin, *, seq, tm=512):
    t, d = h.shape
    ns = seq // tm
    n_tiles = t // tm
    hd = N_HEADS * HEAD_DIM
    const = lambda *shape: pl.BlockSpec(shape, lambda i: (0,) * len(shape))
    row_spec = pl.BlockSpec((tm, hd), lambda i: (_consumed(i), 0))
    row_shape = jax.ShapeDtypeStruct((t, hd), BF16)
    table = pl.BlockSpec((tm, HEAD_DIM), lambda i: (_consumed(i) % ns, 0))
    return pl.pallas_call(
        functools.partial(_qkv_kernel, tm=tm, n_tiles=n_tiles),
        grid=(n_tiles + 1,),
        in_specs=[
            pl.BlockSpec((tm, d), lambda i: (_produced(i, n_tiles), 0)),
            const(1, d),
            const(d, 3 * hd),
            const(1, HEAD_DIM),
            const(1, HEAD_DIM),
            table,
            table,
        ],
        out_specs=[row_spec, row_spec,
                   pl.BlockSpec((tm // MOBA_BLOCK, hd, MOBA_BLOCK), lambda i: (_consumed(i), 0, 0)),
                   pl.BlockSpec((tm // MOBA_BLOCK, 1, hd), lambda i: (_consumed(i), 0, 0))],
        out_shape=[row_shape, row_shape,
                   jax.ShapeDtypeStruct((t // MOBA_BLOCK, hd, MOBA_BLOCK), BF16),
                   jax.ShapeDtypeStruct((t // MOBA_BLOCK, 1, hd), F32)],
        scratch_shapes=[pltpu.VMEM((tm, 3 * hd), F32), pltpu.VMEM((tm, 3 * hd), F32)],
        compiler_params=pltpu.CompilerParams(dimension_semantics=("arbitrary",),
                                             vmem_limit_bytes=VMEM_LIMIT),
        name="qkv",
    )(h, g, w, qg, kg, cos, sin)


def _attn_kernel(q_ref, k_ref, v_ref, km_ref, qg_ref, kg_ref, o_ref, qx_ref, acc_ref, m_ref,
                 s0_ref, s1_ref, *, n_blocks):
    i = pl.program_id(1)
    bs = MOBA_BLOCK
    hd = HEAD_DIM
    bound = QK_SCALE * hd * jnp.max(jnp.abs(qg_ref[...]), axis=-1, keepdims=True) * jnp.max(
        jnp.abs(kg_ref[...]), axis=-1, keepdims=True)
    big = 2.1 * bound + MASK_MARGIN
    blk = lax.broadcasted_iota(jnp.int32, (n_blocks, bs), 0)
    past = blk < i
    key = lax.broadcasted_iota(jnp.int32, (bs, bs), 0)
    qry = lax.broadcasted_iota(jnp.int32, (bs, bs), 1)
    causal = key <= qry
    ones_rows = jnp.ones((ONES_ROWS, bs), BF16)
    own = pl.multiple_of(i * bs, bs)

    for h in range(N_HEADS):
        cols = slice(h * hd, (h + 1) * hd)
        q = q_ref[0, :, cols]
        km_hi, km_lo = _split_bf16(km_ref[0, :, cols])
        stacked = jnp.concatenate([km_hi, km_lo, k_ref[0, pl.ds(own, bs), cols]], axis=0)
        prod = _dot_nt(stacked, q)
        gate_t = prod[:n_blocks] + prod[n_blocks:2 * n_blocks]
        cur = jnp.where(past, gate_t, NEG_INF)
        sel = jnp.zeros((n_blocks, bs), F32)
        for _ in range(min(MOBA_TOPK, n_blocks)):
            top = jnp.max(cur, axis=0, keepdims=True)
            idx = jnp.min(jnp.where(cur == top, blk, n_blocks), axis=0, keepdims=True)
            pick = blk == idx
            sel = jnp.where(pick & past, 1.0, sel)
            cur = jnp.where(pick, NEG_INF, cur)
        bias_t = jnp.where(sel > 0.5, 0.0, -big)
        bias_t = jnp.concatenate([bias_t, jnp.zeros((hd - n_blocks, bs), F32)], axis=0)
        qx_ref[h] = jnp.concatenate([q, jnp.transpose(bias_t).astype(BF16)], axis=1)

        s0_ref[h] = jnp.where(causal, prod[2 * n_blocks:], NEG_INF)
        acc_ref[h] = jnp.zeros(acc_ref.shape[1:], F32)
        m_ref[h] = jnp.full(m_ref.shape[1:], M_INIT, F32)

    def consume(h, s_ref, v_blk):
        cols = slice(h * hd, (h + 1) * hd)
        s = s_ref[h]
        m_old = m_ref[h]
        m_new = jnp.maximum(m_old, jnp.max(s, axis=0, keepdims=True))
        p = jnp.exp2(s - m_new)
        vx = jnp.concatenate([v_ref[0, v_blk, cols, :], ones_rows], axis=0)
        acc_ref[h] = jnp.exp2(m_old - m_new) * acc_ref[h] + _dot(vx, p.astype(BF16))
        m_ref[h] = m_new

    def step(j, s_in, s_out):
        start = pl.multiple_of(j * bs, bs)
        onehot = (lax.broadcasted_iota(jnp.int32, (bs, hd), 1) == j).astype(BF16)
        v_blk = jnp.where(j == 0, i, j - 1)
        for h in range(N_HEADS):
            cols = slice(h * hd, (h + 1) * hd)
            kx = jnp.concatenate([k_ref[0, pl.ds(start, bs), cols], onehot], axis=1)
            s_out[h] = _dot_nt(kx, qx_ref[h])
            consume(h, s_in, v_blk)

    def body(j, carry):
        @pl.when(j % 2 == 0)
        def _():
            step(j, s0_ref, s1_ref)

        @pl.when(j % 2 == 1)
        def _():
            step(j, s1_ref, s0_ref)

        return carry

    lax.fori_loop(0, i, body, 0)

    def finish(s_ref):
        v_blk = jnp.where(i == 0, i, i - 1)
        for h in range(N_HEADS):
            consume(h, s_ref, v_blk)
            a = acc_ref[h]
            o_t = a[:hd, :] / a[hd:hd + 1, :]
            o_ref[0, :, h * hd:(h + 1) * hd] = jnp.transpose(o_t).astype(BF16)

    @pl.when(i % 2 == 0)
    def _():
        finish(s0_ref)

    @pl.when(i % 2 == 1)
    def _():
        finish(s1_ref)


def _attn(q, k, vt, km, qg, kg):
    batch, seq, d = q.shape
    nb = seq // MOBA_BLOCK
    tile = pl.BlockSpec((1, MOBA_BLOCK, d), lambda b, i: (b, i, 0))
    gain = pl.BlockSpec((1, HEAD_DIM), lambda b, i: (0, 0))
    return pl.pallas_call(
        functools.partial(_attn_kernel, n_blocks=nb),
        grid=(batch, nb),
        in_specs=[tile,
                  pl.BlockSpec((1, seq, d), lambda b, i: (b, 0, 0)),
                  pl.BlockSpec((1, nb, d, MOBA_BLOCK), lambda b, i: (b, 0, 0, 0)),
                  pl.BlockSpec((1, nb, d), lambda b, i: (b, 0, 0)), gain, gain],
        out_specs=tile,
        out_shape=jax.ShapeDtypeStruct((batch, seq, d), BF16),
        scratch_shapes=[
            pltpu.VMEM((N_HEADS, MOBA_BLOCK, 2 * HEAD_DIM), BF16),
            pltpu.VMEM((N_HEADS, HEAD_DIM + ONES_ROWS, MOBA_BLOCK), F32),
            pltpu.VMEM((N_HEADS, 1, MOBA_BLOCK), F32),
            pltpu.VMEM((N_HEADS, MOBA_BLOCK, MOBA_BLOCK), F32),
            pltpu.VMEM((N_HEADS, MOBA_BLOCK, MOBA_BLOCK), F32),
        ],
        compiler_params=pltpu.CompilerParams(dimension_semantics=("parallel", "arbitrary"),
                                             vmem_limit_bytes=VMEM_LIMIT),
        name="moba_attn",
    )(q, k, vt, km, qg, kg)


def _wo_router_kernel(o_ref, res_ref, wo_ref, g_ref, wrt_ref, h_ref, hn_ref, route_ref, counts_ref,
                      cnt_ref, buf0_ref, buf1_ref, *, tm, n_tiles):
    def produce(buf_ref):
        h = res_ref[...] + _dot(o_ref[...], wo_ref[...])
        h_ref[...] = h
        buf_ref[...] = h

    def consume(buf_ref):
        hn = _rms(buf_ref[...], g_ref[...])
        _store_rows_as_tiles(hn_ref, hn)

        x_hi, x_lo = _split_bf16(hn)
        w_hi, w_lo = _split_bf16(wrt_ref[...])
        part = _dot_nt(jnp.concatenate([w_hi, w_lo], axis=0), x_hi)
        logits = part[:N_EXPERTS] + part[N_EXPERTS:] + _dot_nt(w_hi, x_lo)
        e_id = lax.broadcasted_iota(jnp.int32, (N_EXPERTS, tm), 0)
        m1 = jnp.max(logits, axis=0, keepdims=True)
        i1 = jnp.min(jnp.where(logits == m1, e_id, N_EXPERTS), axis=0, keepdims=True)
        pick1 = e_id == i1
        rest = jnp.where(pick1, NEG_INF, logits)
        m2 = jnp.max(rest, axis=0, keepdims=True)
        i2 = jnp.min(jnp.where(rest == m2, e_id, N_EXPERTS), axis=0, keepdims=True)
        pick2 = e_id == i2
        e2 = jnp.exp(m2 - m1)
        denom = 1.0 + e2

        chosen = jnp.where(pick1 | pick2, 1.0, 0.0)
        tr = lax.broadcasted_iota(jnp.int32, (tm, tm), 0)
        tc = lax.broadcasted_iota(jnp.int32, (tm, tm), 1)
        earlier = jnp.where(tr < tc, 1.0, 0.0).astype(BF16)
        count = cnt_ref[...]
        rank = _dot(chosen.astype(BF16), earlier) + count
        count = count + jnp.sum(chosen, axis=1, keepdims=True)
        cnt_ref[...] = count
        counts_ref[...] = count
        rank1 = jnp.sum(jnp.where(pick1, rank, 0.0), axis=0, keepdims=True)
        rank2 = jnp.sum(jnp.where(pick2, rank, 0.0), axis=0, keepdims=True)
        fields = [i1.astype(F32), i2.astype(F32), 1.0 / denom, e2 / denom, rank1, rank2]
        route = jnp.zeros((N_EXPERTS, tm), F32)
        for slot, val in enumerate(fields):
            route = jnp.where(e_id == slot, val, route)
        route_ref[...] = route

    @pl.when(pl.program_id(0) == 0)
    def _():
        cnt_ref[...] = jnp.zeros_like(cnt_ref)

    _skewed(n_tiles, produce, consume, (buf0_ref, buf1_ref))


def _wo_router(o, res, w_o, g, w_r_t, *, tm=512):
    t, d = res.shape
    n_tiles = t // tm
    const = lambda *shape: pl.BlockSpec(shape, lambda i: (0,) * len(shape))
    new = pl.BlockSpec((tm, d), lambda i: (_produced(i, n_tiles), 0))
    return pl.pallas_call(
        functools.partial(_wo_router_kernel, tm=tm, n_tiles=n_tiles),
        grid=(n_tiles + 1,),
        in_specs=[new, new, const(d, d), const(1, d), const(N_EXPERTS, d)],
        out_specs=[new,
                   pl.BlockSpec((tm * ROW_TILE, V7X_LANES), lambda i: (_consumed(i), 0)),
                   pl.BlockSpec((N_EXPERTS, tm), lambda i: (0, _consumed(i))),
                   const(N_EXPERTS, 1)],
        out_shape=[jax.ShapeDtypeStruct((t, d), F32),
                   jax.ShapeDtypeStruct((t * ROW_TILE, V7X_LANES), F32),
                   jax.ShapeDtypeStruct((N_EXPERTS, t), F32),
                   jax.ShapeDtypeStruct((N_EXPERTS, 1), F32)],
        scratch_shapes=[pltpu.VMEM((N_EXPERTS, 1), F32), pltpu.VMEM((tm, d), F32),
                        pltpu.VMEM((tm, d), F32)],
        compiler_params=pltpu.CompilerParams(dimension_semantics=("arbitrary",),
                                             vmem_limit_bytes=VMEM_LIMIT),
        name="wo_router",
    )(o, res, w_o, g, w_r_t)


def _store_rows_as_tiles(dst_ref, x):
    n = x.shape[0]
    for s in range(ROW_TILE):
        dst_ref[pl.ds(s, n, stride=ROW_TILE), :] = x[:, s * V7X_LANES:(s + 1) * V7X_LANES]


def _load_lane_tile(src_ref, n, s):
    return src_ref[pl.ds(s, n, stride=ROW_TILE), :]


def _row_copy(src_ref, src_row, dst_ref, dst_row, sem):
    src = src_ref.at[pl.ds(pl.multiple_of(src_row * ROW_TILE, ROW_TILE), ROW_TILE), :]
    dst = dst_ref.at[pl.ds(pl.multiple_of(dst_row * ROW_TILE, ROW_TILE), ROW_TILE), :]
    return pltpu.make_async_copy(src, dst, sem)


def _moe_scatter_kernel(meta_ref, pos_ref, x_ref, xs_ref, zero_ref, sem, zero_sem, *, tm, tg):
    zr = zero_ref.shape[0] // ROW_TILE
    n_rows = xs_ref.shape[0] // ROW_TILE
    total = meta_ref[N_EXPERTS - 1]

    def zero_tile(base):
        base = pl.multiple_of(base * ROW_TILE, tg * ROW_TILE)
        return [pltpu.make_async_copy(
            zero_ref, xs_ref.at[pl.ds(base + c * zr * ROW_TILE, zr * ROW_TILE), :], zero_sem)
            for c in range(tg // zr)]

    @pl.when(pl.program_id(0) == 0)
    def _():
        zero_ref[...] = jnp.zeros_like(zero_ref)
        for wait in (False, True):
            for e in range(N_EXPERTS):
                @pl.when(meta_ref[N_EXPERTS + e] > 0)
                def _():
                    for cp in zero_tile(meta_ref[e] - tg):
                        cp.wait() if wait else cp.start()
            for k in range(N_EXPERTS - 1):
                @pl.when(total + k * tg < n_rows)
                def _():
                    for cp in zero_tile(total + k * tg):
                        cp.wait() if wait else cp.start()

    def start(r, carry):
        _row_copy(x_ref, r, xs_ref, pos_ref[0, 0, r], sem).start(priority=0)
        _row_copy(x_ref, r, xs_ref, pos_ref[0, 0, tm + r], sem).start(priority=1)
        return carry

    lax.fori_loop(0, tm, start, 0, unroll=8)

    def wait(r, carry):
        _row_copy(x_ref, 0, xs_ref, 0, sem).wait()
        _row_copy(x_ref, 0, xs_ref, 0, sem).wait()
        return carry

    lax.fori_loop(0, tm, wait, 0, unroll=8)


def _moe_scatter(meta, pos, x, n_rows, *, tg, tm=256, zero_rows=256):
    t = x.shape[0] // ROW_TILE
    return pl.pallas_call(
        functools.partial(_moe_scatter_kernel, tm=tm, tg=tg),
        grid_spec=pltpu.PrefetchScalarGridSpec(
            num_scalar_prefetch=1,
            grid=(t // tm,),
            in_specs=[pl.BlockSpec((1, 1, 2 * tm), lambda i, meta: (i, 0, 0), memory_space=pltpu.SMEM),
                      pl.BlockSpec((tm * ROW_TILE, V7X_LANES), lambda i, meta: (i, 0))],
            out_specs=pl.BlockSpec(memory_space=pl.ANY),
            scratch_shapes=[pltpu.VMEM((zero_rows * ROW_TILE, V7X_LANES), F32),
                            pltpu.SemaphoreType.DMA(()), pltpu.SemaphoreType.DMA(())]),
        out_shape=jax.ShapeDtypeStruct((n_rows * ROW_TILE, V7X_LANES), F32),
        compiler_params=pltpu.CompilerParams(dimension_semantics=("arbitrary",),
                                             vmem_limit_bytes=VMEM_LIMIT),
        name="moe_scatter",
    )(meta, pos, x)


def _moe_gemm_kernel(te_ref, used_ref, x_ref, wg_ref, wu_ref, wd_ref, y_ref, xb_ref, acc_ref):
    del te_ref
    f = pl.program_id(1)
    tg, d = acc_ref.shape
    live = pl.program_id(0) < used_ref[0]

    @pl.when(jnp.logical_not(live) & (f == 0))
    def _():
        y_ref[...] = jnp.zeros_like(y_ref)

    first = f == 0
    last = f == pl.num_programs(1) - 1

    @pl.when(live & first)
    def _():
        x = jnp.concatenate(
            [_load_lane_tile(x_ref, tg, s).astype(BF16) for s in range(d // V7X_LANES)], axis=1)
        xb_ref[...] = x
        acc_ref[...] = _swiglu(x, wg_ref[0], wu_ref[0], wd_ref[0])

    @pl.when(live & jnp.logical_not(first | last))
    def _():
        acc_ref[...] += _swiglu(xb_ref[...], wg_ref[0], wu_ref[0], wd_ref[0])

    @pl.when(live & last)
    def _():
        _store_rows_as_tiles(y_ref, acc_ref[...] + _swiglu(xb_ref[...], wg_ref[0], wu_ref[0], wd_ref[0]))


def _moe_gemm(tile_expert, n_used, xs, w_gate, w_up, w_down, *, tg, tf):
    n_rows = xs.shape[0] // ROW_TILE
    d, f_dim = w_gate.shape[-2:]
    row_block = (tg * ROW_TILE, V7X_LANES)
    x_row = pl.BlockSpec(row_block, lambda i, f, te, used: (jnp.minimum(i, used[0] - 1), 0))
    y_row = pl.BlockSpec(row_block, lambda i, f, te, used: (i, 0))
    w_tile = lambda i, f, used: jnp.where(i < used[0], f, f_dim // tf - 1)
    return pl.pallas_call(
        _moe_gemm_kernel,
        grid_spec=pltpu.PrefetchScalarGridSpec(
            num_scalar_prefetch=2,
            grid=(n_rows // tg, f_dim // tf),
            in_specs=[x_row,
                      pl.BlockSpec((1, d, tf), lambda i, f, te, used: (te[i], 0, w_tile(i, f, used))),
                      pl.BlockSpec((1, d, tf), lambda i, f, te, used: (te[i], 0, w_tile(i, f, used))),
                      pl.BlockSpec((1, tf, d), lambda i, f, te, used: (te[i], w_tile(i, f, used), 0))],
            out_specs=y_row,
            scratch_shapes=[pltpu.VMEM((tg, d), BF16), pltpu.VMEM((tg, d), F32)]),
        out_shape=jax.ShapeDtypeStruct((n_rows * ROW_TILE, V7X_LANES), F32),
        compiler_params=pltpu.CompilerParams(dimension_semantics=("arbitrary", "arbitrary"),
                                             vmem_limit_bytes=VMEM_LIMIT),
        name="moe_gemm",
    )(tile_expert, n_used, xs, w_gate, w_up, w_down)


def _moe_combine_kernel(pos_ref, next_pos_ref, h_ref, prob_ref, ys_ref, o_ref, buf_ref, sem, *, tm):
    step = pl.program_id(0)
    slot = step % 2

    def fetch(table_ref, into):
        def start(r, carry):
            _row_copy(ys_ref, table_ref[0, 0, r], buf_ref.at[into, 0], r, sem.at[into]).start(
                priority=0)
            _row_copy(ys_ref, table_ref[0, 0, tm + r], buf_ref.at[into, 1], r, sem.at[into]).start(
                priority=1)
            return carry

        lax.fori_loop(0, tm, start, 0, unroll=8)

    @pl.when(step == 0)
    def _():
        fetch(pos_ref, 0)

    @pl.when(step + 1 < pl.num_programs(0))
    def _():
        fetch(next_pos_ref, 1 - slot)

    def wait(r, carry):
        _row_copy(ys_ref, 0, buf_ref.at[slot, 0], 0, sem.at[slot]).wait()
        _row_copy(ys_ref, 0, buf_ref.at[slot, 1], 0, sem.at[slot]).wait()
        return carry

    lax.fori_loop(0, tm, wait, 0, unroll=8)
    p1 = prob_ref[:, 0:1]
    p2 = prob_ref[:, 1:2]
    for s in range(o_ref.shape[1] // V7X_LANES):
        cols = slice(s * V7X_LANES, (s + 1) * V7X_LANES)
        o_ref[:, cols] = (h_ref[:, cols] + p1 * _load_lane_tile(buf_ref.at[slot, 0], tm, s)
                          + p2 * _load_lane_tile(buf_ref.at[slot, 1], tm, s))


def _moe_combine(pos, h, probs, ys, *, tm):
    t, d = h.shape
    n_steps = t // tm
    table = lambda index: pl.BlockSpec((1, 1, 2 * tm), index, memory_space=pltpu.SMEM)
    return pl.pallas_call(
        functools.partial(_moe_combine_kernel, tm=tm),
        grid=(n_steps,),
        in_specs=[table(lambda i: (i, 0, 0)),
                  table(lambda i: (jnp.minimum(i + 1, n_steps - 1), 0, 0)),
                  pl.BlockSpec((tm, d), lambda i: (i, 0)),
                  pl.BlockSpec((tm, 2), lambda i: (i, 0)),
                  pl.BlockSpec(memory_space=pl.ANY)],
        out_specs=pl.BlockSpec((tm, d), lambda i: (i, 0)),
        out_shape=jax.ShapeDtypeStruct((t, d), F32),
        scratch_shapes=[pltpu.VMEM((2, 2, tm * ROW_TILE, V7X_LANES), F32),
                        pltpu.SemaphoreType.DMA((2,))],
        compiler_params=pltpu.CompilerParams(dimension_semantics=("arbitrary",),
                                             vmem_limit_bytes=VMEM_LIMIT),
        name="moe_combine",
    )(pos, pos, h, probs, ys)


def _moe(h, hn, route, counts, w_gate, w_up, w_down, *, tg=1024, tf=512, tm_scatter=1024,
         tm_combine=256):
    t, d = h.shape
    n_tiles = (2 * t + N_EXPERTS * (tg - 1)) // tg
    counts = counts[:, 0].astype(jnp.int32)
    padded = (counts + tg - 1) // tg * tg
    e_ids = jnp.arange(N_EXPERTS, dtype=jnp.int32)
    ends = jnp.sum(jnp.where(e_ids[:, None] <= e_ids[None, :], padded[:, None], 0), axis=0)
    starts = ends - padded
    experts = route[0:2].astype(jnp.int32)
    start_of = jnp.sum(jnp.where(experts[:, :, None] == e_ids, starts, 0), axis=-1)
    pos = start_of + route[4:6].astype(jnp.int32)
    def per_tile(tm):
        return jnp.transpose(pos.reshape(2, t // tm, tm), (1, 0, 2)).reshape(t // tm, 1, 2 * tm)

    n_used = ends[-1:] // tg
    tile_start = jnp.minimum(jnp.arange(n_tiles, dtype=jnp.int32), n_used - 1) * tg
    tile_expert = jnp.sum((ends[None, :] <= tile_start[:, None]).astype(jnp.int32), axis=1)
    xs = _moe_scatter(jnp.concatenate([ends, padded]), per_tile(tm_scatter), hn, n_tiles * tg,
                      tg=tg, tm=tm_scatter)
    ys = _moe_gemm(tile_expert, n_used, xs, w_gate, w_up, w_down, tg=tg, tf=tf)
    return _moe_combine(per_tile(tm_combine), h, jnp.transpose(route[2:4]), ys, tm=tm_combine)


def _rotary_layout(a):
    half = ROT_DIM // 2
    split = HEAD_DIM // 2 - half
    x1, x2, rest = a[..., :half], a[..., half:ROT_DIM], a[..., ROT_DIM:]
    return jnp.concatenate([x1, rest[..., :split], x2, rest[..., split:]], axis=-1)


def _rope_tables(seq):
    half = ROT_DIM // 2
    inv_freq = ROPE_THETA ** (-jnp.arange(half, dtype=F32) / half)
    ang = jnp.arange(seq, dtype=F32)[:, None] * inv_freq[None, :]
    cos = jnp.cos(ang)
    sin = jnp.sin(ang)
    rest = HEAD_DIM - ROT_DIM
    cos_full = jnp.concatenate([cos, cos, jnp.ones((seq, rest), F32)], axis=-1)
    sin_signed = jnp.concatenate([-sin, sin, jnp.zeros((seq, rest), F32)], axis=-1)
    return _rotary_layout(cos_full), _rotary_layout(sin_signed)


def _qkv_weights(w_qkv):
    d = w_qkv.shape[0]
    hd = N_HEADS * HEAD_DIM
    qk = _rotary_layout(w_qkv[:, :2 * hd].reshape(d, 2 * N_HEADS, HEAD_DIM)).reshape(d, 2 * hd)
    return jnp.concatenate([qk, w_qkv[:, 2 * hd:]], axis=1).astype(BF16)


def kernel(x, e_mix_norm, e_w_in, e_conv_w, e_gmlp_ln_g, e_gmlp_ln_b, e_w_spatial, e_b_spatial, e_w_out, e_ffn_norm, e_w_gate, e_w_up, e_w_down, o_mix_norm, o_w_qkv, o_q_norm, o_k_norm, o_w_o, o_ffn_norm, o_w_router, o_w_gate, o_w_up, o_w_down):
    batch, seq, d = x.shape
    t = batch * seq
    xf = x.reshape(t, d)

    bias_full = jnp.repeat(jnp.transpose(e_b_spatial[0]), GMLP_GROUP_DIM, axis=1)
    h1, hn1 = _mixer0(
        xf, e_mix_norm[0][None, :], e_w_in[0].astype(BF16), e_conv_w[0],
        e_gmlp_ln_g[0][None, :], e_gmlp_ln_b[0][None, :], e_w_spatial[0], bias_full,
        e_w_out[0].astype(BF16), e_ffn_norm[0][None, :], seq=seq)
    h2 = _ffn(hn1, h1, e_w_gate[0].astype(BF16), e_w_up[0].astype(BF16), e_w_down[0].astype(BF16),
              tm=512)

    cos, sin = _rope_tables(seq)
    qg = _rotary_layout(o_q_norm[0][None, :])
    kg = _rotary_layout(o_k_norm[0][None, :])
    q, k, vt, km = _qkv(h2, o_mix_norm[0][None, :], _qkv_weights(o_w_qkv[0]), qg, kg, cos, sin, seq=seq)
    as_seq = lambda a: a.reshape(batch, -1, d)
    o = _attn(as_seq(q), as_seq(k), vt.reshape(batch, seq // MOBA_BLOCK, d, MOBA_BLOCK),
              as_seq(km), qg, kg)
    h3, hn3, route, counts = _wo_router(o.reshape(t, d), h2, o_w_o[0].astype(BF16),
                                        o_ffn_norm[0][None, :], jnp.transpose(o_w_router[0]))
    out = _moe(h3, hn3, route, counts, o_w_gate[0], o_w_up[0], o_w_down[0])
    return out.reshape(batch, seq, d)
```

```python
import functools
import math

import jax
import jax.numpy as jnp
from jax import lax
from jax.experimental import pallas as pl
from jax.experimental.pallas import tpu as pltpu

F32 = jnp.float32
BF16 = jnp.bfloat16

D_MODEL = 1024
CONV_DIM = 512
CONV_KERNEL = 3
GMLP_DIM = 512
GMLP_GROUPS = 8
GMLP_GROUP_DIM = GMLP_DIM // GMLP_GROUPS
GMLP_CHUNK = 128
MIX_IN = 3 * CONV_DIM + 2 * GMLP_DIM
N_HEADS = 8
HEAD_DIM = 128
ROT_DIM = HEAD_DIM // 4
ROPE_THETA = 500000.0
MOBA_BLOCK = 256
MOBA_TOPK = 3
N_EXPERTS = 8
EPS = 1e-6

V7X_LANES = 128
ROW_TILE = 8
VMEM_LIMIT = 56 * 1024 * 1024

NEG_INF = float("-inf")
QK_SCALE = HEAD_DIM ** -0.5 * math.log2(math.e)
MASK_MARGIN = 256.0
M_INIT = -1e30
ONES_ROWS = 16


def _rms(x, g):
    return x * lax.rsqrt(jnp.mean(x * x, axis=-1, keepdims=True) + EPS) * g


def _dot(a, b):
    return jnp.dot(a, b, preferred_element_type=F32)


def _dot_nt(a, b):
    return lax.dot_general(a, b, (((1,), (1,)), ((), ())), preferred_element_type=F32)


def _split_bf16(x):
    hi = x.astype(BF16)
    lo = (x - hi.astype(F32)).astype(BF16)
    return hi, lo


def _mixer0_kernel(x_ref, xres_ref, g_ref, win_ref, convw_ref, lng_ref, lnb_ref, ws_ref, bias_ref,
                   wout_ref, g2_ref, h_ref, hn_ref, carry_ref, y_ref, buf0_ref, buf1_ref,
                   *, tm, tiles_per_seq, n_tiles):
    def produce(proj_ref):
        hn = _rms(x_ref[...], g_ref[...]).astype(BF16)
        proj_ref[...] = _dot(hn, win_ref[...])

    def consume(proj_ref):
        tile = pl.program_id(0) - 1
        a_h = proj_ref[:, 0:CONV_DIM]
        a_c = proj_ref[:, CONV_DIM:2 * CONV_DIM]
        a_b = proj_ref[:, 2 * CONV_DIM:3 * CONV_DIM]
        g_v = proj_ref[:, 3 * CONV_DIM + GMLP_DIM:]

        z = a_c * a_h
        prev = jnp.where(tile % tiles_per_seq == 0, 0.0, carry_ref[...])
        row = lax.broadcasted_iota(jnp.int32, (tm, CONV_DIM), 0)
        z1 = jnp.where(row == 0, prev[7:8, :], pltpu.roll(z, 1, 0))
        z2 = jnp.where(row == 0, prev[6:7, :],
                       jnp.where(row == 1, prev[7:8, :], pltpu.roll(z, 2, 0)))
        carry_ref[...] = z[tm - 8:tm, :]
        cw = convw_ref[...]
        y_a = a_b * (cw[0:1, :] * z2 + cw[1:2, :] * z1 + cw[2:3, :] * z)
        y_ref[:, 0:CONV_DIM] = y_a.astype(BF16)

        mu = jnp.mean(g_v, axis=-1, keepdims=True)
        cen = g_v - mu
        var = jnp.mean(cen * cen, axis=-1, keepdims=True)
        v = cen * lax.rsqrt(var + EPS) * lng_ref[...] + lnb_ref[...]
        tr = lax.broadcasted_iota(jnp.int32, (GMLP_CHUNK, GMLP_CHUNK), 0)
        tc = lax.broadcasted_iota(jnp.int32, (GMLP_CHUNK, GMLP_CHUNK), 1)
        causal = tc <= tr
        lo_half = tc < GMLP_GROUP_DIM
        n_chunks = tm // GMLP_CHUNK
        for j in range(GMLP_DIM // V7X_LANES):
            cols = slice(j * V7X_LANES, (j + 1) * V7X_LANES)
            w_pair = jnp.concatenate(
                [jnp.where(causal, ws_ref[2 * j], 0.0), jnp.where(causal, ws_ref[2 * j + 1], 0.0)],
                axis=1).astype(BF16)
            stacked = []
            for c in range(n_chunks):
                vj = v[c * GMLP_CHUNK:(c + 1) * GMLP_CHUNK, cols]
                stacked.append(jnp.concatenate(
                    [jnp.where(lo_half, vj, 0.0), jnp.where(lo_half, 0.0, vj)], axis=0))
            mixed = _dot(w_pair, jnp.concatenate(stacked, axis=1).astype(BF16))
            for c in range(n_chunks):
                rows = slice(c * GMLP_CHUNK, (c + 1) * GMLP_CHUNK)
                g_u = proj_ref[rows, 3 * CONV_DIM + j * V7X_LANES:3 * CONV_DIM + (j + 1) * V7X_LANES]
                m_c = mixed[:, c * V7X_LANES:(c + 1) * V7X_LANES] + bias_ref[:, cols]
                y_ref[rows, CONV_DIM + j * V7X_LANES:CONV_DIM + (j + 1) * V7X_LANES] = (
                    g_u * m_c).astype(BF16)

        h = xres_ref[...] + _dot(y_ref[...], wout_ref[...])
        h_ref[...] = h
        hn_ref[...] = _rms(h, g2_ref[...]).astype(BF16)

    @pl.when(pl.program_id(0) == 0)
    def _():
        carry_ref[...] = jnp.zeros_like(carry_ref)

    _skewed(n_tiles, produce, consume, (buf0_ref, buf1_ref))


def _mixer0(x, g, w_in, conv_w, ln_g, ln_b, w_s, bias_full, w_out, g2, *, seq, tm=512):
    t, d = x.shape
    n_tiles = t // tm
    const = lambda *shape: pl.BlockSpec(shape, lambda i: (0,) * len(shape))
    done = pl.BlockSpec((tm, d), lambda i: (_consumed(i), 0))
    return pl.pallas_call(
        functools.partial(_mixer0_kernel, tm=tm, tiles_per_seq=seq // tm, n_tiles=n_tiles),
        grid=(n_tiles + 1,),
        in_specs=[
            pl.BlockSpec((tm, d), lambda i: (_produced(i, n_tiles), 0)),
            done,
            const(1, d),
            const(d, MIX_IN),
            const(CONV_KERNEL, CONV_DIM),
            const(1, GMLP_DIM),
            const(1, GMLP_DIM),
            const(GMLP_GROUPS, GMLP_CHUNK, GMLP_CHUNK),
            const(GMLP_CHUNK, GMLP_DIM),
            const(CONV_DIM + GMLP_DIM, d),
            const(1, d),
        ],
        out_specs=[done, done],
        out_shape=[jax.ShapeDtypeStruct((t, d), F32), jax.ShapeDtypeStruct((t, d), BF16)],
        scratch_shapes=[pltpu.VMEM((8, CONV_DIM), F32), pltpu.VMEM((tm, CONV_DIM + GMLP_DIM), BF16),
                        pltpu.VMEM((tm, MIX_IN), F32), pltpu.VMEM((tm, MIX_IN), F32)],
        compiler_params=pltpu.CompilerParams(dimension_semantics=("arbitrary",),
                                             vmem_limit_bytes=VMEM_LIMIT),
        name="mixer0",
    )(x, x, g, w_in, conv_w, ln_g, ln_b, w_s, bias_full, w_out, g2)


def _swiglu(x, wg, wu, wd):
    a = _dot(x, wg.astype(BF16))
    u = _dot(x, wu.astype(BF16))
    hact = (a * jax.nn.sigmoid(a) * u).astype(BF16)
    return _dot(hact, wd.astype(BF16))


def _ffn_kernel(x_ref, res_ref, wg_ref, wu_ref, wd_ref, o_ref):
    o_ref[...] = res_ref[...] + _swiglu(x_ref[...], wg_ref[...], wu_ref[...], wd_ref[...])


def _ffn(xn, res, w_gate, w_up, w_down, *, tm):
    t, d = xn.shape
    f_dim = w_gate.shape[1]
    row = pl.BlockSpec((tm, d), lambda i: (i, 0))
    resident = lambda shape: pl.BlockSpec(shape, lambda i: (0, 0), pipeline_mode=pl.Buffered(1))
    return pl.pallas_call(
        _ffn_kernel,
        grid=(t // tm,),
        in_specs=[row, row, resident((d, f_dim)), resident((d, f_dim)), resident((f_dim, d))],
        out_specs=row,
        out_shape=jax.ShapeDtypeStruct((t, d), F32),
        compiler_params=pltpu.CompilerParams(dimension_semantics=("parallel",),
                                             vmem_limit_bytes=VMEM_LIMIT),
        name="dense_ffn",
    )(xn, res, w_gate, w_up, w_down)


def _skewed(n_tiles, produce, consume, bufs):
    i = pl.program_id(0)
    b0, b1 = bufs

    @pl.when(i == 0)
    def _():
        produce(b0)

    for parity, (dst, src) in enumerate(((b0, b1), (b1, b0))):
        @pl.when((i > 0) & (i < n_tiles) & (i % 2 == parity))
        def _():
            produce(dst)
            consume(src)

    @pl.when(i == n_tiles)
    def _():
        consume(bufs[(n_tiles - 1) % 2])


def _produced(i, n_tiles):
    return jnp.minimum(i, n_tiles - 1)


def _consumed(i):
    return jnp.maximum(i - 1, 0)


def _qkv_kernel(h_ref, g_ref, w_ref, qg_ref, kg_ref, cos_ref, sin_ref,
                q_ref, k_ref, v_ref, km_ref, buf0_ref, buf1_ref, *, tm, n_tiles):
    hd = N_HEADS * HEAD_DIM

    def produce(buf_ref):
        hn = _rms(h_ref[...], g_ref[...]).astype(BF16)
        buf_ref[...] = _dot(hn, w_ref[...])

    def consume(buf_ref):
        cos = cos_ref[...]
        sin = sin_ref[...]

        def rope(x):
            return x * cos + pltpu.roll(x, HEAD_DIM // 2, 1) * sin

        for c in range(tm // MOBA_BLOCK):
            v_ref[c] = jnp.transpose(
                buf_ref[c * MOBA_BLOCK:(c + 1) * MOBA_BLOCK, 2 * hd:]).astype(BF16)
        for h in range(N_HEADS):
            cols = slice(h * HEAD_DIM, (h + 1) * HEAD_DIM)
            q = rope(_rms(buf_ref[:, cols], qg_ref[...]))
            k = rope(_rms(buf_ref[:, hd + h * HEAD_DIM:hd + (h + 1) * HEAD_DIM], kg_ref[...]))
            q_ref[:, cols] = (q * QK_SCALE).astype(BF16)
            k_ref[:, cols] = k.astype(BF16)
            for c in range(tm // MOBA_BLOCK):
                km_ref[c, :, cols] = jnp.mean(
                    k[c * MOBA_BLOCK:(c + 1) * MOBA_BLOCK, :], axis=0, keepdims=True)

    _skewed(n_tiles, produce, consume, (buf0_ref, buf1_ref))


def _qkv(h, g, w, qg, kg, cos, sin, *, seq, tm=512):
    t, d = h.shape
    ns = seq // tm
    n_tiles = t // tm
    hd = N_HEADS * HEAD_DIM
    const = lambda *shape: pl.BlockSpec(shape, lambda i: (0,) * len(shape))
    row_spec = pl.BlockSpec((tm, hd), lambda i: (_consumed(i), 0))
    row_shape = jax.ShapeDtypeStruct((t, hd), BF16)
    table = pl.BlockSpec((tm, HEAD_DIM), lambda i: (_consumed(i) % ns, 0))
    return pl.pallas_call(
        functools.partial(_qkv_kernel, tm=tm, n_tiles=n_tiles),
        grid=(n_tiles + 1,),
        in_specs=[
            pl.BlockSpec((tm, d), lambda i: (_produced(i, n_tiles), 0)),
            const(1, d),
            const(d, 3 * hd),
            const(1, HEAD_DIM),
            const(1, HEAD_DIM),
            table,
            table,
        ],
        out_specs=[row_spec, row_spec,
                   pl.BlockSpec((tm // MOBA_BLOCK, hd, MOBA_BLOCK), lambda i: (_consumed(i), 0, 0)),
                   pl.BlockSpec((tm // MOBA_BLOCK, 1, hd), lambda i: (_consumed(i), 0, 0))],
        out_shape=[row_shape, row_shape,
                   jax.ShapeDtypeStruct((t // MOBA_BLOCK, hd, MOBA_BLOCK), BF16),
                   jax.ShapeDtypeStruct((t // MOBA_BLOCK, 1, hd), F32)],
        scratch_shapes=[pltpu.VMEM((tm, 3 * hd), F32), pltpu.VMEM((tm, 3 * hd), F32)],
        compiler_params=pltpu.CompilerParams(dimension_semantics=("arbitrary",),
                                             vmem_limit_bytes=VMEM_LIMIT),
        name="qkv",
    )(h, g, w, qg, kg, cos, sin)


def _attn_kernel(q_ref, k_ref, v_ref, km_ref, qg_ref, kg_ref, o_ref, qx_ref, acc_ref, m_ref,
                 s0_ref, s1_ref, *, n_blocks):
    i = pl.program_id(1)
    bs = MOBA_BLOCK
    hd = HEAD_DIM
    bound = QK_SCALE * hd * jnp.max(jnp.abs(qg_ref[...]), axis=-1, keepdims=True) * jnp.max(
        jnp.abs(kg_ref[...]), axis=-1, keepdims=True)
    big = 2.1 * bound + MASK_MARGIN
    blk = lax.broadcasted_iota(jnp.int32, (n_blocks, bs), 0)
    past = blk < i
    key = lax.broadcasted_iota(jnp.int32, (bs, bs), 0)
    qry = lax.broadcasted_iota(jnp.int32, (bs, bs), 1)
    causal = key <= qry
    ones_rows = jnp.ones((ONES_ROWS, bs), BF16)
    own = pl.multiple_of(i * bs, bs)

    for h in range(N_HEADS):
        cols = slice(h * hd, (h + 1) * hd)
        q = q_ref[0, :, cols]
        km_hi, km_lo = _split_bf16(km_ref[0, :, cols])
        stacked = jnp.concatenate([km_hi, km_lo, k_ref[0, pl.ds(own, bs), cols]], axis=0)
        prod = _dot_nt(stacked, q)
        gate_t = prod[:n_blocks] + prod[n_blocks:2 * n_blocks]
        cur = jnp.where(past, gate_t, NEG_INF)
        sel = jnp.zeros((n_blocks, bs), F32)
        for _ in range(min(MOBA_TOPK, n_blocks)):
            top = jnp.max(cur, axis=0, keepdims=True)
            idx = jnp.min(jnp.where(cur == top, blk, n_blocks), axis=0, keepdims=True)
            pick = blk == idx
            sel = jnp.where(pick & past, 1.0, sel)
            cur = jnp.where(pick, NEG_INF, cur)
        bias_t = jnp.where(sel > 0.5, 0.0, -big)
        bias_t = jnp.concatenate([bias_t, jnp.zeros((hd - n_blocks, bs), F32)], axis=0)
        qx_ref[h] = jnp.concatenate([q, jnp.transpose(bias_t).astype(BF16)], axis=1)

        s0_ref[h] = jnp.where(causal, prod[2 * n_blocks:], NEG_INF)
        acc_ref[h] = jnp.zeros(acc_ref.shape[1:], F32)
        m_ref[h] = jnp.full(m_ref.shape[1:], M_INIT, F32)

    def consume(h, s_ref, v_blk):
        cols = slice(h * hd, (h + 1) * hd)
        s = s_ref[h]
        m_old = m_ref[h]
        m_new = jnp.maximum(m_old, jnp.max(s, axis=0, keepdims=True))
        p = jnp.exp2(s - m_new)
        vx = jnp.concatenate([v_ref[0, v_blk, cols, :], ones_rows], axis=0)
        acc_ref[h] = jnp.exp2(m_old - m_new) * acc_ref[h] + _dot(vx, p.astype(BF16))
        m_ref[h] = m_new

    def step(j, s_in, s_out):
        start = pl.multiple_of(j * bs, bs)
        onehot = (lax.broadcasted_iota(jnp.int32, (bs, hd), 1) == j).astype(BF16)
        v_blk = jnp.where(j == 0, i, j - 1)
        for h in range(N_HEADS):
            cols = slice(h * hd, (h + 1) * hd)
            kx = jnp.concatenate([k_ref[0, pl.ds(start, bs), cols], onehot], axis=1)
            s_out[h] = _dot_nt(kx, qx_ref[h])
            consume(h, s_in, v_blk)

    def body(j, carry):
        @pl.when(j % 2 == 0)
        def _():
            step(j, s0_ref, s1_ref)

        @pl.when(j % 2 == 1)
        def _():
            step(j, s1_ref, s0_ref)

        return carry

    lax.fori_loop(0, i, body, 0)

    def finish(s_ref):
        v_blk = jnp.where(i == 0, i, i - 1)
        for h in range(N_HEADS):
            consume(h, s_ref, v_blk)
            a = acc_ref[h]
            o_t = a[:hd, :] / a[hd:hd + 1, :]
            o_ref[0, :, h * hd:(h + 1) * hd] = jnp.transpose(o_t).astype(BF16)

    @pl.when(i % 2 == 0)
    def _():
        finish(s0_ref)

    @pl.when(i % 2 == 1)
    def _():
        finish(s1_ref)


def _attn(q, k, vt, km, qg, kg):
    batch, seq, d = q.shape
    nb = seq // MOBA_BLOCK
    tile = pl.BlockSpec((1, MOBA_BLOCK, d), lambda b, i: (b, i, 0))
    gain = pl.BlockSpec((1, HEAD_DIM), lambda b, i: (0, 0))
    return pl.pallas_call(
        functools.partial(_attn_kernel, n_blocks=nb),
        grid=(batch, nb),
        in_specs=[tile,
                  pl.BlockSpec((1, seq, d), lambda b, i: (b, 0, 0)),
                  pl.BlockSpec((1, nb, d, MOBA_BLOCK), lambda b, i: (b, 0, 0, 0)),
                  pl.BlockSpec((1, nb, d), lambda b, i: (b, 0, 0)), gain, gain],
        out_specs=tile,
        out_shape=jax.ShapeDtypeStruct((batch, seq, d), BF16),
        scratch_shapes=[
            pltpu.VMEM((N_HEADS, MOBA_BLOCK, 2 * HEAD_DIM), BF16),
            pltpu.VMEM((N_HEADS, HEAD_DIM + ONES_ROWS, MOBA_BLOCK), F32),
            pltpu.VMEM((N_HEADS, 1, MOBA_BLOCK), F32),
            pltpu.VMEM((N_HEADS, MOBA_BLOCK, MOBA_BLOCK), F32),
            pltpu.VMEM((N_HEADS, MOBA_BLOCK, MOBA_BLOCK), F32),
        ],
        compiler_params=pltpu.CompilerParams(dimension_semantics=("parallel", "arbitrary"),
                                             vmem_limit_bytes=VMEM_LIMIT),
        name="moba_attn",
    )(q, k, vt, km, qg, kg)


def _wo_router_kernel(o_ref, res_ref, wo_ref, g_ref, wrt_ref, h_ref, hn_ref, route_ref, counts_ref,
                      cnt_ref, buf0_ref, buf1_ref, *, tm, n_tiles):
    def produce(buf_ref):
        h = res_ref[...] + _dot(o_ref[...], wo_ref[...])
        h_ref[...] = h
        buf_ref[...] = h

    def consume(buf_ref):
        hn = _rms(buf_ref[...], g_ref[...])
        _store_rows_as_tiles(hn_ref, hn)

        x_hi, x_lo = _split_bf16(hn)
        w_hi, w_lo = _split_bf16(wrt_ref[...])
        part = _dot_nt(jnp.concatenate([w_hi, w_lo], axis=0), x_hi)
        logits = part[:N_EXPERTS] + part[N_EXPERTS:] + _dot_nt(w_hi, x_lo)
        e_id = lax.broadcasted_iota(jnp.int32, (N_EXPERTS, tm), 0)
        m1 = jnp.max(logits, axis=0, keepdims=True)
        i1 = jnp.min(jnp.where(logits == m1, e_id, N_EXPERTS), axis=0, keepdims=True)
        pick1 = e_id == i1
        rest = jnp.where(pick1, NEG_INF, logits)
        m2 = jnp.max(rest, axis=0, keepdims=True)
        i2 = jnp.min(jnp.where(rest == m2, e_id, N_EXPERTS), axis=0, keepdims=True)
        pick2 = e_id == i2
        e2 = jnp.exp(m2 - m1)
        denom = 1.0 + e2

        chosen = jnp.where(pick1 | pick2, 1.0, 0.0)
        tr = lax.broadcasted_iota(jnp.int32, (tm, tm), 0)
        tc = lax.broadcasted_iota(jnp.int32, (tm, tm), 1)
        earlier = jnp.where(tr < tc, 1.0, 0.0).astype(BF16)
        count = cnt_ref[...]
        rank = _dot(chosen.astype(BF16), earlier) + count
        count = count + jnp.sum(chosen, axis=1, keepdims=True)
        cnt_ref[...] = count
        counts_ref[...] = count
        rank1 = jnp.sum(jnp.where(pick1, rank, 0.0), axis=0, keepdims=True)
        rank2 = jnp.sum(jnp.where(pick2, rank, 0.0), axis=0, keepdims=True)
        fields = [i1.astype(F32), i2.astype(F32), 1.0 / denom, e2 / denom, rank1, rank2]
        route = jnp.zeros((N_EXPERTS, tm), F32)
        for slot, val in enumerate(fields):
            route = jnp.where(e_id == slot, val, route)
        route_ref[...] = route

    @pl.when(pl.program_id(0) == 0)
    def _():
        cnt_ref[...] = jnp.zeros_like(cnt_ref)

    _skewed(n_tiles, produce, consume, (buf0_ref, buf1_ref))


def _wo_router(o, res, w_o, g, w_r_t, *, tm=512):
    t, d = res.shape
    n_tiles = t // tm
    const = lambda *shape: pl.BlockSpec(shape, lambda i: (0,) * len(shape))
    new = pl.BlockSpec((tm, d), lambda i: (_produced(i, n_tiles), 0))
    return pl.pallas_call(
        functools.partial(_wo_router_kernel, tm=tm, n_tiles=n_tiles),
        grid=(n_tiles + 1,),
        in_specs=[new, new, const(d, d), const(1, d), const(N_EXPERTS, d)],
        out_specs=[new,
                   pl.BlockSpec((tm * ROW_TILE, V7X_LANES), lambda i: (_consumed(i), 0)),
                   pl.BlockSpec((N_EXPERTS, tm), lambda i: (0, _consumed(i))),
                   const(N_EXPERTS, 1)],
        out_shape=[jax.ShapeDtypeStruct((t, d), F32),
                   jax.ShapeDtypeStruct((t * ROW_TILE, V7X_LANES), F32),
                   jax.ShapeDtypeStruct((N_EXPERTS, t), F32),
                   jax.ShapeDtypeStruct((N_EXPERTS, 1), F32)],
        scratch_shapes=[pltpu.VMEM((N_EXPERTS, 1), F32), pltpu.VMEM((tm, d), F32),
                        pltpu.VMEM((tm, d), F32)],
        compiler_params=pltpu.CompilerParams(dimension_semantics=("arbitrary",),
                                             vmem_limit_bytes=VMEM_LIMIT),
        name="wo_router",
    )(o, res, w_o, g, w_r_t)


def _store_rows_as_tiles(dst_ref, x):
    n = x.shape[0]
    for s in range(ROW_TILE):
        dst_ref[pl.ds(s, n, stride=ROW_TILE), :] = x[:, s * V7X_LANES:(s + 1) * V7X_LANES]


def _load_lane_tile(src_ref, n, s):
    return src_ref[pl.ds(s, n, stride=ROW_TILE), :]


def _row_copy(src_ref, src_row, dst_ref, dst_row, sem):
    src = src_ref.at[pl.ds(pl.multiple_of(src_row * ROW_TILE, ROW_TILE), ROW_TILE), :]
    dst = dst_ref.at[pl.ds(pl.multiple_of(dst_row * ROW_TILE, ROW_TILE), ROW_TILE), :]
    return pltpu.make_async_copy(src, dst, sem)


def _moe_scatter_kernel(meta_ref, pos_ref, x_ref, xs_ref, zero_ref, sem, zero_sem, *, tm, tg):
    zr = zero_ref.shape[0] // ROW_TILE
    n_rows = xs_ref.shape[0] // ROW_TILE
    total = meta_ref[N_EXPERTS - 1]

    def zero_tile(base):
        base = pl.multiple_of(base * ROW_TILE, tg * ROW_TILE)
        return [pltpu.make_async_copy(
            zero_ref, xs_ref.at[pl.ds(base + c * zr * ROW_TILE, zr * ROW_TILE), :], zero_sem)
            for c in range(tg // zr)]

    @pl.when(pl.program_id(0) == 0)
    def _():
        zero_ref[...] = jnp.zeros_like(zero_ref)
        for wait in (False, True):
            for e in range(N_EXPERTS):
                @pl.when(meta_ref[N_EXPERTS + e] > 0)
                def _():
                    for cp in zero_tile(meta_ref[e] - tg):
                        cp.wait() if wait else cp.start()
            for k in range(N_EXPERTS - 1):
                @pl.when(total + k * tg < n_rows)
                def _():
                    for cp in zero_tile(total + k * tg):
                        cp.wait() if wait else cp.start()

    def start(r, carry):
        _row_copy(x_ref, r, xs_ref, pos_ref[0, 0, r], sem).start(priority=0)
        _row_copy(x_ref, r, xs_ref, pos_ref[0, 0, tm + r], sem).start(priority=1)
        return carry

    lax.fori_loop(0, tm, start, 0, unroll=8)

    def wait(r, carry):
        _row_copy(x_ref, 0, xs_ref, 0, sem).wait()
        _row_copy(x_ref, 0, xs_ref, 0, sem).wait()
        return carry

    lax.fori_loop(0, tm, wait, 0, unroll=8)


def _moe_scatter(meta, pos, x, n_rows, *, tg, tm=256, zero_rows=256):
    t = x.shape[0] // ROW_TILE
    return pl.pallas_call(
        functools.partial(_moe_scatter_kernel, tm=tm, tg=tg),
        grid_spec=pltpu.PrefetchScalarGridSpec(
            num_scalar_prefetch=1,
            grid=(t // tm,),
            in_specs=[pl.BlockSpec((1, 1, 2 * tm), lambda i, meta: (i, 0, 0), memory_space=pltpu.SMEM),
                      pl.BlockSpec((tm * ROW_TILE, V7X_LANES), lambda i, meta: (i, 0))],
            out_specs=pl.BlockSpec(memory_space=pl.ANY),
            scratch_shapes=[pltpu.VMEM((zero_rows * ROW_TILE, V7X_LANES), F32),
                            pltpu.SemaphoreType.DMA(()), pltpu.SemaphoreType.DMA(())]),
        out_shape=jax.ShapeDtypeStruct((n_rows * ROW_TILE, V7X_LANES), F32),
        compiler_params=pltpu.CompilerParams(dimension_semantics=("arbitrary",),
                                             vmem_limit_bytes=VMEM_LIMIT),
        name="moe_scatter",
    )(meta, pos, x)


def _moe_gemm_kernel(te_ref, used_ref, x_ref, wg_hbm, wu_hbm, wd_hbm, y_ref,
                     xb_ref, acc_ref, wg_buf, wu_buf, wd_buf, sem, *, tf, n_f):
    i = pl.program_id(0)
    used = used_ref[0]
    tg = x_ref.shape[0] // ROW_TILE
    d = wg_buf.shape[1]

    def f_tile_copies(tile, f, slot):
        e = te_ref[tile]
        cols = pl.ds(f * tf, tf)
        return (pltpu.make_async_copy(wg_hbm.at[e, :, cols], wg_buf.at[slot], sem.at[slot, 0]),
                pltpu.make_async_copy(wu_hbm.at[e, :, cols], wu_buf.at[slot], sem.at[slot, 1]),
                pltpu.make_async_copy(wd_hbm.at[e, cols, :], wd_buf.at[slot], sem.at[slot, 2]))

    @pl.when(i >= used)
    def _():
        y_ref[...] = jnp.zeros_like(y_ref)

    @pl.when(i == 0)
    def _():
        for cp in f_tile_copies(0, 0, 0):
            cp.start()

    @pl.when(i < used)
    def _():
        xb_ref[...] = jnp.concatenate(
            [_load_lane_tile(x_ref, tg, s).astype(BF16) for s in range(d // V7X_LANES)], axis=1)
        for f in range(n_f):
            x = xb_ref[...]
            slot = (i * n_f + f) % 2
            for cp in f_tile_copies(i, f, slot):
                cp.wait()
            if f + 1 < n_f:
                for cp in f_tile_copies(i, f + 1, 1 - slot):
                    cp.start()
            else:
                @pl.when(i + 1 < used)
                def _():
                    for cp in f_tile_copies(i + 1, 0, 1 - slot):
                        cp.start()
            y = _swiglu(x, wg_buf[slot], wu_buf[slot], wd_buf[slot])
            if f == 0:
                acc_ref[...] = y
            elif f + 1 < n_f:
                acc_ref[...] += y
            else:
                _store_rows_as_tiles(y_ref, acc_ref[...] + y)


def _moe_gemm(tile_expert, n_used, xs, w_gate, w_up, w_down, *, tg, tf):
    n_rows = xs.shape[0] // ROW_TILE
    d, f_dim = w_gate.shape[-2:]
    row_block = (tg * ROW_TILE, V7X_LANES)
    x_row = pl.BlockSpec(row_block, lambda i, te, used: (jnp.minimum(i, used[0] - 1), 0))
    y_row = pl.BlockSpec(row_block, lambda i, te, used: (i, 0))
    in_hbm = pl.BlockSpec(memory_space=pl.ANY)
    return pl.pallas_call(
        functools.partial(_moe_gemm_kernel, tf=tf, n_f=f_dim // tf),
        grid_spec=pltpu.PrefetchScalarGridSpec(
            num_scalar_prefetch=2,
            grid=(n_rows // tg,),
            in_specs=[x_row, in_hbm, in_hbm, in_hbm],
            out_specs=y_row,
            scratch_shapes=[pltpu.VMEM((tg, d), BF16), pltpu.VMEM((tg, d), F32),
                            pltpu.VMEM((2, d, tf), F32), pltpu.VMEM((2, d, tf), F32),
                            pltpu.VMEM((2, tf, d), F32), pltpu.SemaphoreType.DMA((2, 3))]),
        out_shape=jax.ShapeDtypeStruct((n_rows * ROW_TILE, V7X_LANES), F32),
        compiler_params=pltpu.CompilerParams(dimension_semantics=("arbitrary",),
                                             vmem_limit_bytes=VMEM_LIMIT),
        name="moe_gemm",
    )(tile_expert, n_used, xs, w_gate, w_up, w_down)


def _moe_combine_kernel(pos_ref, next_pos_ref, h_ref, prob_ref, ys_ref, o_ref, buf_ref, sem, *, tm):
    step = pl.program_id(0)
    slot = step % 2

    def fetch(table_ref, into):
        def start(r, carry):
            _row_copy(ys_ref, table_ref[0, 0, r], buf_ref.at[into, 0], r, sem.at[into]).start(
                priority=0)
            _row_copy(ys_ref, table_ref[0, 0, tm + r], buf_ref.at[into, 1], r, sem.at[into]).start(
                priority=1)
            return carry

        lax.fori_loop(0, tm, start, 0, unroll=8)

    @pl.when(step == 0)
    def _():
        fetch(pos_ref, 0)

    @pl.when(step + 1 < pl.num_programs(0))
    def _():
        fetch(next_pos_ref, 1 - slot)

    def wait(r, carry):
        _row_copy(ys_ref, 0, buf_ref.at[slot, 0], 0, sem.at[slot]).wait()
        _row_copy(ys_ref, 0, buf_ref.at[slot, 1], 0, sem.at[slot]).wait()
        return carry

    lax.fori_loop(0, tm, wait, 0, unroll=8)
    p1 = prob_ref[:, 0:1]
    p2 = prob_ref[:, 1:2]
    for s in range(o_ref.shape[1] // V7X_LANES):
        cols = slice(s * V7X_LANES, (s + 1) * V7X_LANES)
        o_ref[:, cols] = (h_ref[:, cols] + p1 * _load_lane_tile(buf_ref.at[slot, 0], tm, s)
                          + p2 * _load_lane_tile(buf_ref.at[slot, 1], tm, s))


def _moe_combine(pos, h, probs, ys, *, tm):
    t, d = h.shape
    n_steps = t // tm
    table = lambda index: pl.BlockSpec((1, 1, 2 * tm), index, memory_space=pltpu.SMEM)
    return pl.pallas_call(
        functools.partial(_moe_combine_kernel, tm=tm),
        grid=(n_steps,),
        in_specs=[table(lambda i: (i, 0, 0)),
                  table(lambda i: (jnp.minimum(i + 1, n_steps - 1), 0, 0)),
                  pl.BlockSpec((tm, d), lambda i: (i, 0)),
                  pl.BlockSpec((tm, 2), lambda i: (i, 0)),
                  pl.BlockSpec(memory_space=pl.ANY)],
        out_specs=pl.BlockSpec((tm, d), lambda i: (i, 0)),
        out_shape=jax.ShapeDtypeStruct((t, d), F32),
        scratch_shapes=[pltpu.VMEM((2, 2, tm * ROW_TILE, V7X_LANES), F32),
                        pltpu.SemaphoreType.DMA((2,))],
        compiler_params=pltpu.CompilerParams(dimension_semantics=("arbitrary",),
                                             vmem_limit_bytes=VMEM_LIMIT),
        name="moe_combine",
    )(pos, pos, h, probs, ys)


def _moe(h, hn, route, counts, w_gate, w_up, w_down, *, tg=1024, tf=512, tm_scatter=1024,
         tm_combine=256):
    t, d = h.shape
    n_tiles = (2 * t + N_EXPERTS * (tg - 1)) // tg
    counts = counts[:, 0].astype(jnp.int32)
    padded = (counts + tg - 1) // tg * tg
    e_ids = jnp.arange(N_EXPERTS, dtype=jnp.int32)
    ends = jnp.sum(jnp.where(e_ids[:, None] <= e_ids[None, :], padded[:, None], 0), axis=0)
    starts = ends - padded
    experts = route[0:2].astype(jnp.int32)
    start_of = jnp.sum(jnp.where(experts[:, :, None] == e_ids, starts, 0), axis=-1)
    pos = start_of + route[4:6].astype(jnp.int32)
    def per_tile(tm):
        return jnp.transpose(pos.reshape(2, t // tm, tm), (1, 0, 2)).reshape(t // tm, 1, 2 * tm)

    n_used = ends[-1:] // tg
    tile_start = jnp.minimum(jnp.arange(n_tiles, dtype=jnp.int32), n_used - 1) * tg
    tile_expert = jnp.sum((ends[None, :] <= tile_start[:, None]).astype(jnp.int32), axis=1)
    xs = _moe_scatter(jnp.concatenate([ends, padded]), per_tile(tm_scatter), hn, n_tiles * tg,
                      tg=tg, tm=tm_scatter)
    ys = _moe_gemm(tile_expert, n_used, xs, w_gate, w_up, w_down, tg=tg, tf=tf)
    return _moe_combine(per_tile(tm_combine), h, jnp.transpose(route[2:4]), ys, tm=tm_combine)


def _rotary_layout(a):
    half = ROT_DIM // 2
    split = HEAD_DIM // 2 - half
    x1, x2, rest = a[..., :half], a[..., half:ROT_DIM], a[..., ROT_DIM:]
    return jnp.concatenate([x1, rest[..., :split], x2, rest[..., split:]], axis=-1)


def _rope_tables(seq):
    half = ROT_DIM // 2
    inv_freq = ROPE_THETA ** (-jnp.arange(half, dtype=F32) / half)
    ang = jnp.arange(seq, dtype=F32)[:, None] * inv_freq[None, :]
    cos = jnp.cos(ang)
    sin = jnp.sin(ang)
    rest = HEAD_DIM - ROT_DIM
    cos_full = jnp.concatenate([cos, cos, jnp.ones((seq, rest), F32)], axis=-1)
    sin_signed = jnp.concatenate([-sin, sin, jnp.zeros((seq, rest), F32)], axis=-1)
    return _rotary_layout(cos_full), _rotary_layout(sin_signed)


def _qkv_weights(w_qkv):
    d = w_qkv.shape[0]
    hd = N_HEADS * HEAD_DIM
    qk = _rotary_layout(w_qkv[:, :2 * hd].reshape(d, 2 * N_HEADS, HEAD_DIM)).reshape(d, 2 * hd)
    return jnp.concatenate([qk, w_qkv[:, 2 * hd:]], axis=1).astype(BF16)


def kernel(x, e_mix_norm, e_w_in, e_conv_w, e_gmlp_ln_g, e_gmlp_ln_b, e_w_spatial, e_b_spatial, e_w_out, e_ffn_norm, e_w_gate, e_w_up, e_w_down, o_mix_norm, o_w_qkv, o_q_norm, o_k_norm, o_w_o, o_ffn_norm, o_w_router, o_w_gate, o_w_up, o_w_down):
    batch, seq, d = x.shape
    t = batch * seq
    xf = x.reshape(t, d)

    bias_full = jnp.repeat(jnp.transpose(e_b_spatial[0]), GMLP_GROUP_DIM, axis=1)
    h1, hn1 = _mixer0(
        xf, e_mix_norm[0][None, :], e_w_in[0].astype(BF16), e_conv_w[0],
        e_gmlp_ln_g[0][None, :], e_gmlp_ln_b[0][None, :], e_w_spatial[0], bias_full,
        e_w_out[0].astype(BF16), e_ffn_norm[0][None, :], seq=seq)
    h2 = _ffn(hn1, h1, e_w_gate[0].astype(BF16), e_w_up[0].astype(BF16), e_w_down[0].astype(BF16),
              tm=512)

    cos, sin = _rope_tables(seq)
    qg = _rotary_layout(o_q_norm[0][None, :])
    kg = _rotary_layout(o_k_norm[0][None, :])
    q, k, vt, km = _qkv(h2, o_mix_norm[0][None, :], _qkv_weights(o_w_qkv[0]), qg, kg, cos, sin, seq=seq)
    as_seq = lambda a: a.reshape(batch, -1, d)
    o = _attn(as_seq(q), as_seq(k), vt.reshape(batch, seq // MOBA_BLOCK, d, MOBA_BLOCK),
              as_seq(km), qg, kg)
    h3, hn3, route, counts = _wo_router(o.reshape(t, d), h2, o_w_o[0].astype(BF16),
                                        o_ffn_norm[0][None, :], jnp.transpose(o_w_router[0]))
    out = _moe(h3, hn3, route, counts, o_w_gate[0], o_w_up[0], o_w_down[0])
    return out.reshape(batch, seq, d)
```

```python
import functools
import math

import jax
import jax.numpy as jnp
from jax import lax
from jax.experimental import pallas as pl
from jax.experimental.pallas import tpu as pltpu

F32 = jnp.float32
BF16 = jnp.bfloat16

D_MODEL = 1024
CONV_DIM = 512
CONV_KERNEL = 3
GMLP_DIM = 512
GMLP_GROUPS = 8
GMLP_GROUP_DIM = GMLP_DIM // GMLP_GROUPS
GMLP_CHUNK = 128
MIX_IN = 3 * CONV_DIM + 2 * GMLP_DIM
N_HEADS = 8
HEAD_DIM = 128
ROT_DIM = HEAD_DIM // 4
ROPE_THETA = 500000.0
MOBA_BLOCK = 256
MOBA_TOPK = 3
N_EXPERTS = 8
EPS = 1e-6

V7X_LANES = 128
ROW_TILE = 8
VMEM_LIMIT = 56 * 1024 * 1024

NEG_INF = float("-inf")
QK_SCALE = HEAD_DIM ** -0.5 * math.log2(math.e)
MASK_MARGIN = 256.0
M_INIT = -1e30
ONES_ROWS = 16


def _rms(x, g):
    return x * lax.rsqrt(jnp.mean(x * x, axis=-1, keepdims=True) + EPS) * g


def _dot(a, b):
    return jnp.dot(a, b, preferred_element_type=F32)


def _dot_nt(a, b):
    return lax.dot_general(a, b, (((1,), (1,)), ((), ())), preferred_element_type=F32)


def _split_bf16(x):
    hi = x.astype(BF16)
    lo = (x - hi.astype(F32)).astype(BF16)
    return hi, lo


def _mixer0_kernel(x_ref, xres_ref, g_ref, win_ref, convw_ref, lng_ref, lnb_ref, ws_ref, bias_ref,
                   wout_ref, g2_ref, h_ref, hn_ref, carry_ref, y_ref, buf0_ref, buf1_ref,
                   *, tm, tiles_per_seq, n_tiles):
    def produce(proj_ref):
        hn = _rms(x_ref[...], g_ref[...]).astype(BF16)
        proj_ref[...] = _dot(hn, win_ref[...])

    def consume(proj_ref):
        tile = pl.program_id(0) - 1
        a_h = proj_ref[:, 0:CONV_DIM]
        a_c = proj_ref[:, CONV_DIM:2 * CONV_DIM]
        a_b = proj_ref[:, 2 * CONV_DIM:3 * CONV_DIM]
        g_v = proj_ref[:, 3 * CONV_DIM + GMLP_DIM:]

        z = a_c * a_h
        prev = jnp.where(tile % tiles_per_seq == 0, 0.0, carry_ref[...])
        row = lax.broadcasted_iota(jnp.int32, (tm, CONV_DIM), 0)
        z1 = jnp.where(row == 0, prev[7:8, :], pltpu.roll(z, 1, 0))
        z2 = jnp.where(row == 0, prev[6:7, :],
                       jnp.where(row == 1, prev[7:8, :], pltpu.roll(z, 2, 0)))
        carry_ref[...] = z[tm - 8:tm, :]
        cw = convw_ref[...]
        y_a = a_b * (cw[0:1, :] * z2 + cw[1:2, :] * z1 + cw[2:3, :] * z)
        y_ref[:, 0:CONV_DIM] = y_a.astype(BF16)

        mu = jnp.mean(g_v, axis=-1, keepdims=True)
        cen = g_v - mu
        var = jnp.mean(cen * cen, axis=-1, keepdims=True)
        v = cen * lax.rsqrt(var + EPS) * lng_ref[...] + lnb_ref[...]
        tr = lax.broadcasted_iota(jnp.int32, (GMLP_CHUNK, GMLP_CHUNK), 0)
        tc = lax.broadcasted_iota(jnp.int32, (GMLP_CHUNK, GMLP_CHUNK), 1)
        causal = tc <= tr
        lo_half = tc < GMLP_GROUP_DIM
        n_chunks = tm // GMLP_CHUNK
        for j in range(GMLP_DIM // V7X_LANES):
            cols = slice(j * V7X_LANES, (j + 1) * V7X_LANES)
            w_pair = jnp.concatenate(
                [jnp.where(causal, ws_ref[2 * j], 0.0), jnp.where(causal, ws_ref[2 * j + 1], 0.0)],
                axis=1).astype(BF16)
            stacked = []
            for c in range(n_chunks):
                vj = v[c * GMLP_CHUNK:(c + 1) * GMLP_CHUNK, cols]
                stacked.append(jnp.concatenate(
                    [jnp.where(lo_half, vj, 0.0), jnp.where(lo_half, 0.0, vj)], axis=0))
            mixed = _dot(w_pair, jnp.concatenate(stacked, axis=1).astype(BF16))
            for c in range(n_chunks):
                rows = slice(c * GMLP_CHUNK, (c + 1) * GMLP_CHUNK)
                g_u = proj_ref[rows, 3 * CONV_DIM + j * V7X_LANES:3 * CONV_DIM + (j + 1) * V7X_LANES]
                m_c = mixed[:, c * V7X_LANES:(c + 1) * V7X_LANES] + bias_ref[:, cols]
                y_ref[rows, CONV_DIM + j * V7X_LANES:CONV_DIM + (j + 1) * V7X_LANES] = (
                    g_u * m_c).astype(BF16)

        h = xres_ref[...] + _dot(y_ref[...], wout_ref[...])
        h_ref[...] = h
        hn_ref[...] = _rms(h, g2_ref[...]).astype(BF16)

    @pl.when(pl.program_id(0) == 0)
    def _():
        carry_ref[...] = jnp.zeros_like(carry_ref)

    _skewed(n_tiles, produce, consume, (buf0_ref, buf1_ref))


def _mixer0(x, g, w_in, conv_w, ln_g, ln_b, w_s, bias_full, w_out, g2, *, seq, tm=512):
    t, d = x.shape
    n_tiles = t // tm
    const = lambda *shape: pl.BlockSpec(shape, lambda i: (0,) * len(shape))
    done = pl.BlockSpec((tm, d), lambda i: (_consumed(i), 0))
    return pl.pallas_call(
        functools.partial(_mixer0_kernel, tm=tm, tiles_per_seq=seq // tm, n_tiles=n_tiles),
        grid=(n_tiles + 1,),
        in_specs=[
            pl.BlockSpec((tm, d), lambda i: (_produced(i, n_tiles), 0)),
            done,
            const(1, d),
            const(d, MIX_IN),
            const(CONV_KERNEL, CONV_DIM),
            const(1, GMLP_DIM),
            const(1, GMLP_DIM),
            const(GMLP_GROUPS, GMLP_CHUNK, GMLP_CHUNK),
            const(GMLP_CHUNK, GMLP_DIM),
            const(CONV_DIM + GMLP_DIM, d),
            const(1, d),
        ],
        out_specs=[done, done],
        out_shape=[jax.ShapeDtypeStruct((t, d), F32), jax.ShapeDtypeStruct((t, d), BF16)],
        scratch_shapes=[pltpu.VMEM((8, CONV_DIM), F32), pltpu.VMEM((tm, CONV_DIM + GMLP_DIM), BF16),
                        pltpu.VMEM((tm, MIX_IN), F32), pltpu.VMEM((tm, MIX_IN), F32)],
        compiler_params=pltpu.CompilerParams(dimension_semantics=("arbitrary",),
                                             vmem_limit_bytes=VMEM_LIMIT),
        name="mixer0",
    )(x, x, g, w_in, conv_w, ln_g, ln_b, w_s, bias_full, w_out, g2)


def _swiglu(x, wg, wu, wd):
    a = _dot(x, wg.astype(BF16))
    u = _dot(x, wu.astype(BF16))
    hact = (a * jax.nn.sigmoid(a) * u).astype(BF16)
    return _dot(hact, wd.astype(BF16))


def _ffn_kernel(x_ref, res_ref, wg_ref, wu_ref, wd_ref, o_ref):
    o_ref[...] = res_ref[...] + _swiglu(x_ref[...], wg_ref[...], wu_ref[...], wd_ref[...])


def _ffn(xn, res, w_gate, w_up, w_down, *, tm):
    t, d = xn.shape
    f_dim = w_gate.shape[1]
    row = pl.BlockSpec((tm, d), lambda i: (i, 0))
    resident = lambda shape: pl.BlockSpec(shape, lambda i: (0, 0), pipeline_mode=pl.Buffered(1))
    return pl.pallas_call(
        _ffn_kernel,
        grid=(t // tm,),
        in_specs=[row, row, resident((d, f_dim)), resident((d, f_dim)), resident((f_dim, d))],
        out_specs=row,
        out_shape=jax.ShapeDtypeStruct((t, d), F32),
        compiler_params=pltpu.CompilerParams(dimension_semantics=("parallel",),
                                             vmem_limit_bytes=VMEM_LIMIT),
        name="dense_ffn",
    )(xn, res, w_gate, w_up, w_down)


def _skewed(n_tiles, produce, consume, bufs):
    i = pl.program_id(0)
    b0, b1 = bufs

    @pl.when(i == 0)
    def _():
        produce(b0)

    for parity, (dst, src) in enumerate(((b0, b1), (b1, b0))):
        @pl.when((i > 0) & (i < n_tiles) & (i % 2 == parity))
        def _():
            produce(dst)
            consume(src)

    @pl.when(i == n_tiles)
    def _():
        consume(bufs[(n_tiles - 1) % 2])


def _produced(i, n_tiles):
    return jnp.minimum(i, n_tiles - 1)


def _consumed(i):
    return jnp.maximum(i - 1, 0)


def _qkv_kernel(h_ref, g_ref, w_ref, qg_ref, kg_ref, cos_ref, sin_ref,
                q_ref, k_ref, v_ref, km_ref, buf0_ref, buf1_ref, *, tm, n_tiles):
    hd = N_HEADS * HEAD_DIM

    def produce(buf_ref):
        hn = _rms(h_ref[...], g_ref[...]).astype(BF16)
        buf_ref[...] = _dot(hn, w_ref[...])

    def consume(buf_ref):
        cos = cos_ref[...]
        sin = sin_ref[...]

        def rope(x):
            return x * cos + pltpu.roll(x, HEAD_DIM // 2, 1) * sin

        for c in range(tm // MOBA_BLOCK):
            v_ref[c] = jnp.transpose(
                buf_ref[c * MOBA_BLOCK:(c + 1) * MOBA_BLOCK, 2 * hd:]).astype(BF16)
        for h in range(N_HEADS):
            cols = slice(h * HEAD_DIM, (h + 1) * HEAD_DIM)
            q = rope(_rms(buf_ref[:, cols], qg_ref[...]))
            k = rope(_rms(buf_ref[:, hd + h * HEAD_DIM:hd + (h + 1) * HEAD_DIM], kg_ref[...]))
            q_ref[:, cols] = (q * QK_SCALE).astype(BF16)
            k_ref[:, cols] = k.astype(BF16)
            for c in range(tm // MOBA_BLOCK):
                km_ref[c, :, cols] = jnp.mean(
                    k[c * MOBA_BLOCK:(c + 1) * MOBA_BLOCK, :], axis=0, keepdims=True)

    _skewed(n_tiles, produce, consume, (buf0_ref, buf1_ref))


def _qkv(h, g, w, qg, kg, cos, sin, *, seq, tm=512):
    t, d = h.shape
    ns = seq // tm
    n_tiles = t // tm
    hd = N_HEADS * HEAD_DIM
    const = lambda *shape: pl.BlockSpec(shape, lambda i: (0,) * len(shape))
    row_spec = pl.BlockSpec((tm, hd), lambda i: (_consumed(i), 0))
    row_shape = jax.ShapeDtypeStruct((t, hd), BF16)
    table = pl.BlockSpec((tm, HEAD_DIM), lambda i: (_consumed(i) % ns, 0))
    return pl.pallas_call(
        functools.partial(_qkv_kernel, tm=tm, n_tiles=n_tiles),
        grid=(n_tiles + 1,),
        in_specs=[
            pl.BlockSpec((tm, d), lambda i: (_produced(i, n_tiles), 0)),
            const(1, d),
            const(d, 3 * hd),
            const(1, HEAD_DIM),
            const(1, HEAD_DIM),
            table,
            table,
        ],
        out_specs=[row_spec, row_spec,
                   pl.BlockSpec((tm // MOBA_BLOCK, hd, MOBA_BLOCK), lambda i: (_consumed(i), 0, 0)),
                   pl.BlockSpec((tm // MOBA_BLOCK, 1, hd), lambda i: (_consumed(i), 0, 0))],
        out_shape=[row_shape, row_shape,
                   jax.ShapeDtypeStruct((t // MOBA_BLOCK, hd, MOBA_BLOCK), BF16),
                   jax.ShapeDtypeStruct((t // MOBA_BLOCK, 1, hd), F32)],
        scratch_shapes=[pltpu.VMEM((tm, 3 * hd), F32), pltpu.VMEM((tm, 3 * hd), F32)],
        compiler_params=pltpu.CompilerParams(dimension_semantics=("arbitrary",),
                                             vmem_limit_bytes=VMEM_LIMIT),
        name="qkv",
    )(h, g, w, qg, kg, cos, sin)


def _attn_kernel(q_ref, k_ref, v_ref, km_ref, qg_ref, kg_ref, o_ref, qx_ref, acc_ref, m_ref,
                 s0_ref, s1_ref, *, n_blocks):
    i = pl.program_id(1)
    bs = MOBA_BLOCK
    hd = HEAD_DIM
    bound = QK_SCALE * hd * jnp.max(jnp.abs(qg_ref[...]), axis=-1, keepdims=True) * jnp.max(
        jnp.abs(kg_ref[...]), axis=-1, keepdims=True)
    big = 2.1 * bound + MASK_MARGIN
    blk = lax.broadcasted_iota(jnp.int32, (n_blocks, bs), 0)
    past = blk < i
    key = lax.broadcasted_iota(jnp.int32, (bs, bs), 0)
    qry = lax.broadcasted_iota(jnp.int32, (bs, bs), 1)
    causal = key <= qry
    ones_rows = jnp.ones((ONES_ROWS, bs), BF16)
    own = pl.multiple_of(i * bs, bs)

    for h in range(N_HEADS):
        cols = slice(h * hd, (h + 1) * hd)
        q = q_ref[0, :, cols]
        km_hi, km_lo = _split_bf16(km_ref[0, :, cols])
        stacked = jnp.concatenate([km_hi, km_lo, k_ref[0, pl.ds(own, bs), cols]], axis=0)
        prod = _dot_nt(stacked, q)
        gate_t = prod[:n_blocks] + prod[n_blocks:2 * n_blocks]
        cur = jnp.where(past, gate_t, NEG_INF)
        sel = jnp.zeros((n_blocks, bs), F32)
        for _ in range(min(MOBA_TOPK, n_blocks)):
            top = jnp.max(cur, axis=0, keepdims=True)
            idx = jnp.min(jnp.where(cur == top, blk, n_blocks), axis=0, keepdims=True)
            pick = blk == idx
            sel = jnp.where(pick & past, 1.0, sel)
            cur = jnp.where(pick, NEG_INF, cur)
        bias_t = jnp.where(sel > 0.5, 0.0, -big)
        bias_t = jnp.concatenate([bias_t, jnp.zeros((hd - n_blocks, bs), F32)], axis=0)
        qx_ref[h] = jnp.concatenate([q, jnp.transpose(bias_t).astype(BF16)], axis=1)

        s0_ref[h] = jnp.where(causal, prod[2 * n_blocks:], NEG_INF)
        acc_ref[h] = jnp.zeros(acc_ref.shape[1:], F32)
        m_ref[h] = jnp.full(m_ref.shape[1:], M_INIT, F32)

    def consume(h, s_ref, v_blk):
        cols = slice(h * hd, (h + 1) * hd)
        s = s_ref[h]
        m_old = m_ref[h]
        m_new = jnp.maximum(m_old, jnp.max(s, axis=0, keepdims=True))
        p = jnp.exp2(s - m_new)
        vx = jnp.concatenate([v_ref[0, v_blk, cols, :], ones_rows], axis=0)
        acc_ref[h] = jnp.exp2(m_old - m_new) * acc_ref[h] + _dot(vx, p.astype(BF16))
        m_ref[h] = m_new

    def step(j, s_in, s_out):
        start = pl.multiple_of(j * bs, bs)
        onehot = (lax.broadcasted_iota(jnp.int32, (bs, hd), 1) == j).astype(BF16)
        v_blk = jnp.where(j == 0, i, j - 1)
        for h in range(N_HEADS):
            cols = slice(h * hd, (h + 1) * hd)
            kx = jnp.concatenate([k_ref[0, pl.ds(start, bs), cols], onehot], axis=1)
            s_out[h] = _dot_nt(kx, qx_ref[h])
            consume(h, s_in, v_blk)

    def body(j, carry):
        @pl.when(j % 2 == 0)
        def _():
            step(j, s0_ref, s1_ref)

        @pl.when(j % 2 == 1)
        def _():
            step(j, s1_ref, s0_ref)

        return carry

    lax.fori_loop(0, i, body, 0)

    def finish(s_ref):
        v_blk = jnp.where(i == 0, i, i - 1)
        for h in range(N_HEADS):
            consume(h, s_ref, v_blk)
            a = acc_ref[h]
            o_t = a[:hd, :] / a[hd:hd + 1, :]
            o_ref[0, :, h * hd:(h + 1) * hd] = jnp.transpose(o_t).astype(BF16)

    @pl.when(i % 2 == 0)
    def _():
        finish(s0_ref)

    @pl.when(i % 2 == 1)
    def _():
        finish(s1_ref)


def _attn(q, k, vt, km, qg, kg):
    batch, seq, d = q.shape
    nb = seq // MOBA_BLOCK
    tile = pl.BlockSpec((1, MOBA_BLOCK, d), lambda b, i: (b, i, 0))
    gain = pl.BlockSpec((1, HEAD_DIM), lambda b, i: (0, 0))
    return pl.pallas_call(
        functools.partial(_attn_kernel, n_blocks=nb),
        grid=(batch, nb),
        in_specs=[tile,
                  pl.BlockSpec((1, seq, d), lambda b, i: (b, 0, 0)),
                  pl.BlockSpec((1, nb, d, MOBA_BLOCK), lambda b, i: (b, 0, 0, 0)),
                  pl.BlockSpec((1, nb, d), lambda b, i: (b, 0, 0)), gain, gain],
        out_specs=tile,
        out_shape=jax.ShapeDtypeStruct((batch, seq, d), BF16),
        scratch_shapes=[
            pltpu.VMEM((N_HEADS, MOBA_BLOCK, 2 * HEAD_DIM), BF16),
            pltpu.VMEM((N_HEADS, HEAD_DIM + ONES_ROWS, MOBA_BLOCK), F32),
            pltpu.VMEM((N_HEADS, 1, MOBA_BLOCK), F32),
            pltpu.VMEM((N_HEADS, MOBA_BLOCK, MOBA_BLOCK), F32),
            pltpu.VMEM((N_HEADS, MOBA_BLOCK, MOBA_BLOCK), F32),
        ],
        compiler_params=pltpu.CompilerParams(dimension_semantics=("parallel", "arbitrary"),
                                             vmem_limit_bytes=VMEM_LIMIT),
        name="moba_attn",
    )(q, k, vt, km, qg, kg)


def _wo_router_kernel(o_ref, res_ref, wo_ref, g_ref, wrt_ref, h_ref, hn_ref, route_ref, counts_ref,
                      cnt_ref, buf0_ref, buf1_ref, *, tm, n_tiles):
    def produce(buf_ref):
        h = res_ref[...] + _dot(o_ref[...], wo_ref[...])
        h_ref[...] = h
        buf_ref[...] = h

    def consume(buf_ref):
        hn = _rms(buf_ref[...], g_ref[...])
        _store_rows_as_tiles(hn_ref, hn)

        x_hi, x_lo = _split_bf16(hn)
        w_hi, w_lo = _split_bf16(wrt_ref[...])
        part = _dot_nt(jnp.concatenate([w_hi, w_lo], axis=0), x_hi)
        logits = part[:N_EXPERTS] + part[N_EXPERTS:] + _dot_nt(w_hi, x_lo)
        e_id = lax.broadcasted_iota(jnp.int32, (N_EXPERTS, tm), 0)
        m1 = jnp.max(logits, axis=0, keepdims=True)
        i1 = jnp.min(jnp.where(logits == m1, e_id, N_EXPERTS), axis=0, keepdims=True)
        pick1 = e_id == i1
        rest = jnp.where(pick1, NEG_INF, logits)
        m2 = jnp.max(rest, axis=0, keepdims=True)
        i2 = jnp.min(jnp.where(rest == m2, e_id, N_EXPERTS), axis=0, keepdims=True)
        pick2 = e_id == i2
        e2 = jnp.exp(m2 - m1)
        denom = 1.0 + e2

        chosen = jnp.where(pick1 | pick2, 1.0, 0.0)
        tr = lax.broadcasted_iota(jnp.int32, (tm, tm), 0)
        tc = lax.broadcasted_iota(jnp.int32, (tm, tm), 1)
        earlier = jnp.where(tr < tc, 1.0, 0.0).astype(BF16)
        count = cnt_ref[...]
        rank = _dot(chosen.astype(BF16), earlier) + count
        count = count + jnp.sum(chosen, axis=1, keepdims=True)
        cnt_ref[...] = count
        counts_ref[...] = count
        rank1 = jnp.sum(jnp.where(pick1, rank, 0.0), axis=0, keepdims=True)
        rank2 = jnp.sum(jnp.where(pick2, rank, 0.0), axis=0, keepdims=True)
        fields = [i1.astype(F32), i2.astype(F32), 1.0 / denom, e2 / denom, rank1, rank2]
        route = jnp.zeros((N_EXPERTS, tm), F32)
        for slot, val in enumerate(fields):
            route = jnp.where(e_id == slot, val, route)
        route_ref[...] = route

    @pl.when(pl.program_id(0) == 0)
    def _():
        cnt_ref[...] = jnp.zeros_like(cnt_ref)

    _skewed(n_tiles, produce, consume, (buf0_ref, buf1_ref))


def _wo_router(o, res, w_o, g, w_r_t, *, tm=512):
    t, d = res.shape
    n_tiles = t // tm
    const = lambda *shape: pl.BlockSpec(shape, lambda i: (0,) * len(shape))
    new = pl.BlockSpec((tm, d), lambda i: (_produced(i, n_tiles), 0))
    return pl.pallas_call(
        functools.partial(_wo_router_kernel, tm=tm, n_tiles=n_tiles),
        grid=(n_tiles + 1,),
        in_specs=[new, new, const(d, d), const(1, d), const(N_EXPERTS, d)],
        out_specs=[new,
                   pl.BlockSpec((tm * ROW_TILE, V7X_LANES), lambda i: (_consumed(i), 0)),
                   pl.BlockSpec((N_EXPERTS, tm), lambda i: (0, _consumed(i))),
                   const(N_EXPERTS, 1)],
        out_shape=[jax.ShapeDtypeStruct((t, d), F32),
                   jax.ShapeDtypeStruct((t * ROW_TILE, V7X_LANES), F32),
                   jax.ShapeDtypeStruct((N_EXPERTS, t), F32),
                   jax.ShapeDtypeStruct((N_EXPERTS, 1), F32)],
        scratch_shapes=[pltpu.VMEM((N_EXPERTS, 1), F32), pltpu.VMEM((tm, d), F32),
                        pltpu.VMEM((tm, d), F32)],
        compiler_params=pltpu.CompilerParams(dimension_semantics=("arbitrary",),
                                             vmem_limit_bytes=VMEM_LIMIT),
        name="wo_router",
    )(o, res, w_o, g, w_r_t)


def _store_rows_as_tiles(dst_ref, x):
    n = x.shape[0]
    for s in range(ROW_TILE):
        dst_ref[pl.ds(s, n, stride=ROW_TILE), :] = x[:, s * V7X_LANES:(s + 1) * V7X_LANES]


def _load_lane_tile(src_ref, n, s):
    return src_ref[pl.ds(s, n, stride=ROW_TILE), :]


def _row_copy(src_ref, src_row, dst_ref, dst_row, sem):
    src = src_ref.at[pl.ds(pl.multiple_of(src_row * ROW_TILE, ROW_TILE), ROW_TILE), :]
    dst = dst_ref.at[pl.ds(pl.multiple_of(dst_row * ROW_TILE, ROW_TILE), ROW_TILE), :]
    return pltpu.make_async_copy(src, dst, sem)


def _moe_scatter_kernel(meta_ref, pos_ref, x_ref, xs_ref, zero_ref, sem, zero_sem, *, tm, tg):
    zr = zero_ref.shape[0] // ROW_TILE
    n_rows = xs_ref.shape[0] // ROW_TILE
    total = meta_ref[N_EXPERTS - 1]

    def zero_tile(base):
        base = pl.multiple_of(base * ROW_TILE, tg * ROW_TILE)
        return [pltpu.make_async_copy(
            zero_ref, xs_ref.at[pl.ds(base + c * zr * ROW_TILE, zr * ROW_TILE), :], zero_sem)
            for c in range(tg // zr)]

    @pl.when(pl.program_id(0) == 0)
    def _():
        zero_ref[...] = jnp.zeros_like(zero_ref)
        for wait in (False, True):
            for e in range(N_EXPERTS):
                @pl.when(meta_ref[N_EXPERTS + e] > 0)
                def _():
                    for cp in zero_tile(meta_ref[e] - tg):
                        cp.wait() if wait else cp.start()
            for k in range(N_EXPERTS - 1):
                @pl.when(total + k * tg < n_rows)
                def _():
                    for cp in zero_tile(total + k * tg):
                        cp.wait() if wait else cp.start()

    def start(r, carry):
        _row_copy(x_ref, r, xs_ref, pos_ref[0, 0, r], sem).start(priority=0)
        _row_copy(x_ref, r, xs_ref, pos_ref[0, 0, tm + r], sem).start(priority=1)
        return carry

    lax.fori_loop(0, tm, start, 0, unroll=8)

    def wait(r, carry):
        _row_copy(x_ref, 0, xs_ref, 0, sem).wait()
        _row_copy(x_ref, 0, xs_ref, 0, sem).wait()
        return carry

    lax.fori_loop(0, tm, wait, 0, unroll=8)


def _moe_scatter(meta, pos, x, n_rows, *, tg, tm=256, zero_rows=256):
    t = x.shape[0] // ROW_TILE
    return pl.pallas_call(
        functools.partial(_moe_scatter_kernel, tm=tm, tg=tg),
        grid_spec=pltpu.PrefetchScalarGridSpec(
            num_scalar_prefetch=1,
            grid=(t // tm,),
            in_specs=[pl.BlockSpec((1, 1, 2 * tm), lambda i, meta: (i, 0, 0), memory_space=pltpu.SMEM),
                      pl.BlockSpec((tm * ROW_TILE, V7X_LANES), lambda i, meta: (i, 0))],
            out_specs=pl.BlockSpec(memory_space=pl.ANY),
            scratch_shapes=[pltpu.VMEM((zero_rows * ROW_TILE, V7X_LANES), F32),
                            pltpu.SemaphoreType.DMA(()), pltpu.SemaphoreType.DMA(())]),
        out_shape=jax.ShapeDtypeStruct((n_rows * ROW_TILE, V7X_LANES), F32),
        compiler_params=pltpu.CompilerParams(dimension_semantics=("arbitrary",),
                                             vmem_limit_bytes=VMEM_LIMIT),
        name="moe_scatter",
    )(meta, pos, x)


def _moe_gemm_kernel(te_ref, used_ref, fill_ref, x_ref, wg_ref, wu_ref, wd_ref, y_ref, xb_ref,
                     acc_ref):
    del te_ref
    i = pl.program_id(0)
    f = pl.program_id(1)
    tg, d = acc_ref.shape
    live = i < used_ref[0]
    half_full = fill_ref[i] <= tg // 2
    first = f == 0
    last = f == pl.num_programs(1) - 1

    @pl.when(jnp.logical_not(live) & first)
    def _():
        y_ref[...] = jnp.zeros_like(y_ref)

    def steps(rows):
        weights = lambda: (wg_ref[0], wu_ref[0], wd_ref[0])

        @pl.when(first)
        def _():
            x = jnp.concatenate(
                [_load_lane_tile(x_ref, rows, s).astype(BF16) for s in range(d // V7X_LANES)], axis=1)
            xb_ref[:rows] = x
            acc_ref[:rows] = _swiglu(x, *weights())

        @pl.when(jnp.logical_not(first | last))
        def _():
            acc_ref[:rows] += _swiglu(xb_ref[:rows], *weights())

        @pl.when(last)
        def _():
            _store_rows_as_tiles(y_ref, acc_ref[:rows] + _swiglu(xb_ref[:rows], *weights()))
            if rows < tg:
                y_ref[rows * ROW_TILE:, :] = jnp.zeros(((tg - rows) * ROW_TILE, V7X_LANES), F32)

    @pl.when(live & jnp.logical_not(half_full))
    def _():
        steps(tg)

    @pl.when(live & half_full)
    def _():
        steps(tg // 2)


def _moe_gemm(tile_expert, n_used, tile_fill, xs, w_gate, w_up, w_down, *, tg, tf):
    n_rows = xs.shape[0] // ROW_TILE
    d, f_dim = w_gate.shape[-2:]
    row_block = (tg * ROW_TILE, V7X_LANES)
    x_row = pl.BlockSpec(row_block, lambda i, f, te, used, fill: (jnp.minimum(i, used[0] - 1), 0))
    y_row = pl.BlockSpec(row_block, lambda i, f, te, used, fill: (i, 0))
    w_tile = lambda i, f, used: jnp.where(i < used[0], f, f_dim // tf - 1)
    return pl.pallas_call(
        _moe_gemm_kernel,
        grid_spec=pltpu.PrefetchScalarGridSpec(
            num_scalar_prefetch=3,
            grid=(n_rows // tg, f_dim // tf),
            in_specs=[x_row,
                      pl.BlockSpec((1, d, tf), lambda i, f, te, used, fill: (te[i], 0, w_tile(i, f, used))),
                      pl.BlockSpec((1, d, tf), lambda i, f, te, used, fill: (te[i], 0, w_tile(i, f, used))),
                      pl.BlockSpec((1, tf, d), lambda i, f, te, used, fill: (te[i], w_tile(i, f, used), 0))],
            out_specs=y_row,
            scratch_shapes=[pltpu.VMEM((tg, d), BF16), pltpu.VMEM((tg, d), F32)]),
        out_shape=jax.ShapeDtypeStruct((n_rows * ROW_TILE, V7X_LANES), F32),
        compiler_params=pltpu.CompilerParams(dimension_semantics=("arbitrary", "arbitrary"),
                                             vmem_limit_bytes=VMEM_LIMIT),
        name="moe_gemm",
    )(tile_expert, n_used, tile_fill, xs, w_gate, w_up, w_down)


def _moe_combine_kernel(pos_ref, next_pos_ref, h_ref, prob_ref, ys_ref, o_ref, buf_ref, sem, *, tm):
    step = pl.program_id(0)
    slot = step % 2

    def fetch(table_ref, into):
        def start(r, carry):
            _row_copy(ys_ref, table_ref[0, 0, r], buf_ref.at[into, 0], r, sem.at[into]).start(
                priority=0)
            _row_copy(ys_ref, table_ref[0, 0, tm + r], buf_ref.at[into, 1], r, sem.at[into]).start(
                priority=1)
            return carry

        lax.fori_loop(0, tm, start, 0, unroll=8)

    @pl.when(step == 0)
    def _():
        fetch(pos_ref, 0)

    @pl.when(step + 1 < pl.num_programs(0))
    def _():
        fetch(next_pos_ref, 1 - slot)

    def wait(r, carry):
        _row_copy(ys_ref, 0, buf_ref.at[slot, 0], 0, sem.at[slot]).wait()
        _row_copy(ys_ref, 0, buf_ref.at[slot, 1], 0, sem.at[slot]).wait()
        return carry

    lax.fori_loop(0, tm, wait, 0, unroll=8)
    p1 = prob_ref[:, 0:1]
    p2 = prob_ref[:, 1:2]
    for s in range(o_ref.shape[1] // V7X_LANES):
        cols = slice(s * V7X_LANES, (s + 1) * V7X_LANES)
        o_ref[:, cols] = (h_ref[:, cols] + p1 * _load_lane_tile(buf_ref.at[slot, 0], tm, s)
                          + p2 * _load_lane_tile(buf_ref.at[slot, 1], tm, s))


def _moe_combine(pos, h, probs, ys, *, tm):
    t, d = h.shape
    n_steps = t // tm
    table = lambda index: pl.BlockSpec((1, 1, 2 * tm), index, memory_space=pltpu.SMEM)
    return pl.pallas_call(
        functools.partial(_moe_combine_kernel, tm=tm),
        grid=(n_steps,),
        in_specs=[table(lambda i: (i, 0, 0)),
                  table(lambda i: (jnp.minimum(i + 1, n_steps - 1), 0, 0)),
                  pl.BlockSpec((tm, d), lambda i: (i, 0)),
                  pl.BlockSpec((tm, 2), lambda i: (i, 0)),
                  pl.BlockSpec(memory_space=pl.ANY)],
        out_specs=pl.BlockSpec((tm, d), lambda i: (i, 0)),
        out_shape=jax.ShapeDtypeStruct((t, d), F32),
        scratch_shapes=[pltpu.VMEM((2, 2, tm * ROW_TILE, V7X_LANES), F32),
                        pltpu.SemaphoreType.DMA((2,))],
        compiler_params=pltpu.CompilerParams(dimension_semantics=("arbitrary",),
                                             vmem_limit_bytes=VMEM_LIMIT),
        name="moe_combine",
    )(pos, pos, h, probs, ys)


def _moe(h, hn, route, counts, w_gate, w_up, w_down, *, tg=1024, tf=512, tm_scatter=1024,
         tm_combine=256):
    t, d = h.shape
    n_tiles = (2 * t + N_EXPERTS * (tg - 1)) // tg
    counts = counts[:, 0].astype(jnp.int32)
    padded = (counts + tg - 1) // tg * tg
    e_ids = jnp.arange(N_EXPERTS, dtype=jnp.int32)
    ends = jnp.sum(jnp.where(e_ids[:, None] <= e_ids[None, :], padded[:, None], 0), axis=0)
    starts = ends - padded
    experts = route[0:2].astype(jnp.int32)
    start_of = jnp.sum(jnp.where(experts[:, :, None] == e_ids, starts, 0), axis=-1)
    pos = start_of + route[4:6].astype(jnp.int32)
    def per_tile(tm):
        return jnp.transpose(pos.reshape(2, t // tm, tm), (1, 0, 2)).reshape(t // tm, 1, 2 * tm)

    n_used = ends[-1:] // tg
    tile_start = jnp.minimum(jnp.arange(n_tiles, dtype=jnp.int32), n_used - 1) * tg
    tile_expert = jnp.sum((ends[None, :] <= tile_start[:, None]).astype(jnp.int32), axis=1)
    filled_to = jnp.sum(jnp.where(tile_expert[:, None] == e_ids, starts + counts, 0), axis=1)
    tile_fill = jnp.clip(filled_to - tile_start, 0, tg)
    xs = _moe_scatter(jnp.concatenate([ends, padded]), per_tile(tm_scatter), hn, n_tiles * tg,
                      tg=tg, tm=tm_scatter)
    ys = _moe_gemm(tile_expert, n_used, tile_fill, xs, w_gate, w_up, w_down, tg=tg, tf=tf)
    return _moe_combine(per_tile(tm_combine), h, jnp.transpose(route[2:4]), ys, tm=tm_combine)


def _rotary_layout(a):
    half = ROT_DIM // 2
    split = HEAD_DIM // 2 - half
    x1, x2, rest = a[..., :half], a[..., half:ROT_DIM], a[..., ROT_DIM:]
    return jnp.concatenate([x1, rest[..., :split], x2, rest[..., split:]], axis=-1)


def _rope_tables(seq):
    half = ROT_DIM // 2
    inv_freq = ROPE_THETA ** (-jnp.arange(half, dtype=F32) / half)
    ang = jnp.arange(seq, dtype=F32)[:, None] * inv_freq[None, :]
    cos = jnp.cos(ang)
    sin = jnp.sin(ang)
    rest = HEAD_DIM - ROT_DIM
    cos_full = jnp.concatenate([cos, cos, jnp.ones((seq, rest), F32)], axis=-1)
    sin_signed = jnp.concatenate([-sin, sin, jnp.zeros((seq, rest), F32)], axis=-1)
    return _rotary_layout(cos_full), _rotary_layout(sin_signed)


def _qkv_weights(w_qkv):
    d = w_qkv.shape[0]
    hd = N_HEADS * HEAD_DIM
    qk = _rotary_layout(w_qkv[:, :2 * hd].reshape(d, 2 * N_HEADS, HEAD_DIM)).reshape(d, 2 * hd)
    return jnp.concatenate([qk, w_qkv[:, 2 * hd:]], axis=1).astype(BF16)


def kernel(x, e_mix_norm, e_w_in, e_conv_w, e_gmlp_ln_g, e_gmlp_ln_b, e_w_spatial, e_b_spatial, e_w_out, e_ffn_norm, e_w_gate, e_w_up, e_w_down, o_mix_norm, o_w_qkv, o_q_norm, o_k_norm, o_w_o, o_ffn_norm, o_w_router, o_w_gate, o_w_up, o_w_down):
    batch, seq, d = x.shape
    t = batch * seq
    xf = x.reshape(t, d)

    bias_full = jnp.repeat(jnp.transpose(e_b_spatial[0]), GMLP_GROUP_DIM, axis=1)
    h1, hn1 = _mixer0(
        xf, e_mix_norm[0][None, :], e_w_in[0].astype(BF16), e_conv_w[0],
        e_gmlp_ln_g[0][None, :], e_gmlp_ln_b[0][None, :], e_w_spatial[0], bias_full,
        e_w_out[0].astype(BF16), e_ffn_norm[0][None, :], seq=seq)
    h2 = _ffn(hn1, h1, e_w_gate[0].astype(BF16), e_w_up[0].astype(BF16), e_w_down[0].astype(BF16),
              tm=512)

    cos, sin = _rope_tables(seq)
    qg = _rotary_layout(o_q_norm[0][None, :])
    kg = _rotary_layout(o_k_norm[0][None, :])
    q, k, vt, km = _qkv(h2, o_mix_norm[0][None, :], _qkv_weights(o_w_qkv[0]), qg, kg, cos, sin, seq=seq)
    as_seq = lambda a: a.reshape(batch, -1, d)
    o = _attn(as_seq(q), as_seq(k), vt.reshape(batch, seq // MOBA_BLOCK, d, MOBA_BLOCK),
              as_seq(km), qg, kg)
    h3, hn3, route, counts = _wo_router(o.reshape(t, d), h2, o_w_o[0].astype(BF16),
                                        o_ffn_norm[0][None, :], jnp.transpose(o_w_router[0]))
    out = _moe(h3, hn3, route, counts, o_w_gate[0], o_w_up[0], o_w_down[0])
    return out.reshape(batch, seq, d)
```

```python
import functools
import math

import jax
import jax.numpy as jnp
from jax import lax
from jax.experimental import pallas as pl
from jax.experimental.pallas import tpu as pltpu

F32 = jnp.float32
BF16 = jnp.bfloat16

D_MODEL = 1024
CONV_DIM = 512
CONV_KERNEL = 3
GMLP_DIM = 512
GMLP_GROUPS = 8
GMLP_GROUP_DIM = GMLP_DIM // GMLP_GROUPS
GMLP_CHUNK = 128
MIX_IN = 3 * CONV_DIM + 2 * GMLP_DIM
N_HEADS = 8
HEAD_DIM = 128
ROT_DIM = HEAD_DIM // 4
ROPE_THETA = 500000.0
MOBA_BLOCK = 256
MOBA_TOPK = 3
N_EXPERTS = 8
EPS = 1e-6

V7X_LANES = 128
ROW_TILE = 8
GEMM_ROW_PARTS = 4
VMEM_LIMIT = 56 * 1024 * 1024

NEG_INF = float("-inf")
QK_SCALE = HEAD_DIM ** -0.5 * math.log2(math.e)
MASK_MARGIN = 256.0
M_INIT = -1e30
ONES_ROWS = 16


def _rms(x, g):
    return x * lax.rsqrt(jnp.mean(x * x, axis=-1, keepdims=True) + EPS) * g


def _dot(a, b):
    return jnp.dot(a, b, preferred_element_type=F32)


def _dot_nt(a, b):
    return lax.dot_general(a, b, (((1,), (1,)), ((), ())), preferred_element_type=F32)


def _split_bf16(x):
    hi = x.astype(BF16)
    lo = (x - hi.astype(F32)).astype(BF16)
    return hi, lo


def _mixer0_kernel(x_ref, xres_ref, g_ref, win_ref, convw_ref, lng_ref, lnb_ref, ws_ref, bias_ref,
                   wout_ref, g2_ref, h_ref, hn_ref, carry_ref, y_ref, buf0_ref, buf1_ref,
                   *, tm, tiles_per_seq, n_tiles):
    def produce(proj_ref):
        hn = _rms(x_ref[...], g_ref[...]).astype(BF16)
        proj_ref[...] = _dot(hn, win_ref[...])

    def consume(proj_ref):
        tile = pl.program_id(0) - 1
        a_h = proj_ref[:, 0:CONV_DIM]
        a_c = proj_ref[:, CONV_DIM:2 * CONV_DIM]
        a_b = proj_ref[:, 2 * CONV_DIM:3 * CONV_DIM]
        g_v = proj_ref[:, 3 * CONV_DIM + GMLP_DIM:]

        z = a_c * a_h
        prev = jnp.where(tile % tiles_per_seq == 0, 0.0, carry_ref[...])
        row = lax.broadcasted_iota(jnp.int32, (tm, CONV_DIM), 0)
        z1 = jnp.where(row == 0, prev[7:8, :], pltpu.roll(z, 1, 0))
        z2 = jnp.where(row == 0, prev[6:7, :],
                       jnp.where(row == 1, prev[7:8, :], pltpu.roll(z, 2, 0)))
        carry_ref[...] = z[tm - 8:tm, :]
        cw = convw_ref[...]
        y_a = a_b * (cw[0:1, :] * z2 + cw[1:2, :] * z1 + cw[2:3, :] * z)
        y_ref[:, 0:CONV_DIM] = y_a.astype(BF16)

        mu = jnp.mean(g_v, axis=-1, keepdims=True)
        cen = g_v - mu
        var = jnp.mean(cen * cen, axis=-1, keepdims=True)
        v = cen * lax.rsqrt(var + EPS) * lng_ref[...] + lnb_ref[...]
        tr = lax.broadcasted_iota(jnp.int32, (GMLP_CHUNK, GMLP_CHUNK), 0)
        tc = lax.broadcasted_iota(jnp.int32, (GMLP_CHUNK, GMLP_CHUNK), 1)
        causal = tc <= tr
        lo_half = tc < GMLP_GROUP_DIM
        n_chunks = tm // GMLP_CHUNK
        for j in range(GMLP_DIM // V7X_LANES):
            cols = slice(j * V7X_LANES, (j + 1) * V7X_LANES)
            w_pair = jnp.concatenate(
                [jnp.where(causal, ws_ref[2 * j], 0.0), jnp.where(causal, ws_ref[2 * j + 1], 0.0)],
                axis=1).astype(BF16)
            stacked = []
            for c in range(n_chunks):
                vj = v[c * GMLP_CHUNK:(c + 1) * GMLP_CHUNK, cols]
                stacked.append(jnp.concatenate(
                    [jnp.where(lo_half, vj, 0.0), jnp.where(lo_half, 0.0, vj)], axis=0))
            mixed = _dot(w_pair, jnp.concatenate(stacked, axis=1).astype(BF16))
            for c in range(n_chunks):
                rows = slice(c * GMLP_CHUNK, (c + 1) * GMLP_CHUNK)
                g_u = proj_ref[rows, 3 * CONV_DIM + j * V7X_LANES:3 * CONV_DIM + (j + 1) * V7X_LANES]
                m_c = mixed[:, c * V7X_LANES:(c + 1) * V7X_LANES] + bias_ref[:, cols]
                y_ref[rows, CONV_DIM + j * V7X_LANES:CONV_DIM + (j + 1) * V7X_LANES] = (
                    g_u * m_c).astype(BF16)

        h = xres_ref[...] + _dot(y_ref[...], wout_ref[...])
        h_ref[...] = h
        hn_ref[...] = _rms(h, g2_ref[...]).astype(BF16)

    @pl.when(pl.program_id(0) == 0)
    def _():
        carry_ref[...] = jnp.zeros_like(carry_ref)

    _skewed(n_tiles, produce, consume, (buf0_ref, buf1_ref))


def _mixer0(x, g, w_in, conv_w, ln_g, ln_b, w_s, bias_full, w_out, g2, *, seq, tm=512):
    t, d = x.shape
    n_tiles = t // tm
    const = lambda *shape: pl.BlockSpec(shape, lambda i: (0,) * len(shape))
    done = pl.BlockSpec((tm, d), lambda i: (_consumed(i), 0))
    return pl.pallas_call(
        functools.partial(_mixer0_kernel, tm=tm, tiles_per_seq=seq // tm, n_tiles=n_tiles),
        grid=(n_tiles + 1,),
        in_specs=[
            pl.BlockSpec((tm, d), lambda i: (_produced(i, n_tiles), 0)),
            done,
            const(1, d),
            const(d, MIX_IN),
            const(CONV_KERNEL, CONV_DIM),
            const(1, GMLP_DIM),
            const(1, GMLP_DIM),
            const(GMLP_GROUPS, GMLP_CHUNK, GMLP_CHUNK),
            const(GMLP_CHUNK, GMLP_DIM),
            const(CONV_DIM + GMLP_DIM, d),
            const(1, d),
        ],
        out_specs=[done, done],
        out_shape=[jax.ShapeDtypeStruct((t, d), F32), jax.ShapeDtypeStruct((t, d), BF16)],
        scratch_shapes=[pltpu.VMEM((8, CONV_DIM), F32), pltpu.VMEM((tm, CONV_DIM + GMLP_DIM), BF16),
                        pltpu.VMEM((tm, MIX_IN), F32), pltpu.VMEM((tm, MIX_IN), F32)],
        compiler_params=pltpu.CompilerParams(dimension_semantics=("arbitrary",),
                                             vmem_limit_bytes=VMEM_LIMIT),
        name="mixer0",
    )(x, x, g, w_in, conv_w, ln_g, ln_b, w_s, bias_full, w_out, g2)


def _swiglu(x, wg, wu, wd):
    a = _dot(x, wg.astype(BF16))
    u = _dot(x, wu.astype(BF16))
    hact = (a * jax.nn.sigmoid(a) * u).astype(BF16)
    return _dot(hact, wd.astype(BF16))


def _ffn_kernel(x_ref, res_ref, wg_ref, wu_ref, wd_ref, o_ref):
    o_ref[...] = res_ref[...] + _swiglu(x_ref[...], wg_ref[...], wu_ref[...], wd_ref[...])


def _ffn(xn, res, w_gate, w_up, w_down, *, tm):
    t, d = xn.shape
    f_dim = w_gate.shape[1]
    row = pl.BlockSpec((tm, d), lambda i: (i, 0))
    resident = lambda shape: pl.BlockSpec(shape, lambda i: (0, 0), pipeline_mode=pl.Buffered(1))
    return pl.pallas_call(
        _ffn_kernel,
        grid=(t // tm,),
        in_specs=[row, row, resident((d, f_dim)), resident((d, f_dim)), resident((f_dim, d))],
        out_specs=row,
        out_shape=jax.ShapeDtypeStruct((t, d), F32),
        compiler_params=pltpu.CompilerParams(dimension_semantics=("parallel",),
                                             vmem_limit_bytes=VMEM_LIMIT),
        name="dense_ffn",
    )(xn, res, w_gate, w_up, w_down)


def _skewed(n_tiles, produce, consume, bufs):
    i = pl.program_id(0)
    b0, b1 = bufs

    @pl.when(i == 0)
    def _():
        produce(b0)

    for parity, (dst, src) in enumerate(((b0, b1), (b1, b0))):
        @pl.when((i > 0) & (i < n_tiles) & (i % 2 == parity))
        def _():
            produce(dst)
            consume(src)

    @pl.when(i == n_tiles)
    def _():
        consume(bufs[(n_tiles - 1) % 2])


def _produced(i, n_tiles):
    return jnp.minimum(i, n_tiles - 1)


def _consumed(i):
    return jnp.maximum(i - 1, 0)


def _qkv_kernel(h_ref, g_ref, w_ref, qg_ref, kg_ref, cos_ref, sin_ref,
                q_ref, k_ref, v_ref, km_ref, buf0_ref, buf1_ref, *, tm, n_tiles):
    hd = N_HEADS * HEAD_DIM

    def produce(buf_ref):
        hn = _rms(h_ref[...], g_ref[...]).astype(BF16)
        buf_ref[...] = _dot(hn, w_ref[...])

    def consume(buf_ref):
        cos = cos_ref[...]
        sin = sin_ref[...]

        def rope(x):
            return x * cos + pltpu.roll(x, HEAD_DIM // 2, 1) * sin

        for c in range(tm // MOBA_BLOCK):
            v_ref[c] = jnp.transpose(
                buf_ref[c * MOBA_BLOCK:(c + 1) * MOBA_BLOCK, 2 * hd:]).astype(BF16)
        for h in range(N_HEADS):
            cols = slice(h * HEAD_DIM, (h + 1) * HEAD_DIM)
            q = rope(_rms(buf_ref[:, cols], qg_ref[...]))
            k = rope(_rms(buf_ref[:, hd + h * HEAD_DIM:hd + (h + 1) * HEAD_DIM], kg_ref[...]))
            q_ref[:, cols] = (q * QK_SCALE).astype(BF16)
            k_ref[:, cols] = k.astype(BF16)
            for c in range(tm // MOBA_BLOCK):
                km_ref[c, :, cols] = jnp.mean(
                    k[c * MOBA_BLOCK:(c + 1) * MOBA_BLOCK, :], axis=0, keepdims=True)

    _skewed(n_tiles, produce, consume, (buf0_ref, buf1_ref))


def _qkv(h, g, w, qg, kg, cos, sin, *, seq, tm=512):
    t, d = h.shape
    ns = seq // tm
    n_tiles = t // tm
    hd = N_HEADS * HEAD_DIM
    const = lambda *shape: pl.BlockSpec(shape, lambda i: (0,) * len(shape))
    row_spec = pl.BlockSpec((tm, hd), lambda i: (_consumed(i), 0))
    row_shape = jax.ShapeDtypeStruct((t, hd), BF16)
    table = pl.BlockSpec((tm, HEAD_DIM), lambda i: (_consumed(i) % ns, 0))
    return pl.pallas_call(
        functools.partial(_qkv_kernel, tm=tm, n_tiles=n_tiles),
        grid=(n_tiles + 1,),
        in_specs=[
            pl.BlockSpec((tm, d), lambda i: (_produced(i, n_tiles), 0)),
            const(1, d),
            const(d, 3 * hd),
            const(1, HEAD_DIM),
            const(1, HEAD_DIM),
            table,
            table,
        ],
        out_specs=[row_spec, row_spec,
                   pl.BlockSpec((tm // MOBA_BLOCK, hd, MOBA_BLOCK), lambda i: (_consumed(i), 0, 0)),
                   pl.BlockSpec((tm // MOBA_BLOCK, 1, hd), lambda i: (_consumed(i), 0, 0))],
        out_shape=[row_shape, row_shape,
                   jax.ShapeDtypeStruct((t // MOBA_BLOCK, hd, MOBA_BLOCK), BF16),
                   jax.ShapeDtypeStruct((t // MOBA_BLOCK, 1, hd), F32)],
        scratch_shapes=[pltpu.VMEM((tm, 3 * hd), F32), pltpu.VMEM((tm, 3 * hd), F32)],
        compiler_params=pltpu.CompilerParams(dimension_semantics=("arbitrary",),
                                             vmem_limit_bytes=VMEM_LIMIT),
        name="qkv",
    )(h, g, w, qg, kg, cos, sin)


def _attn_kernel(q_ref, k_ref, v_ref, km_ref, qg_ref, kg_ref, o_ref, qx_ref, acc_ref, m_ref,
                 s0_ref, s1_ref, mb0_ref, mb1_ref, *, n_blocks):
    i = pl.program_id(1)
    bs = MOBA_BLOCK
    hd = HEAD_DIM
    bound = QK_SCALE * hd * jnp.max(jnp.abs(qg_ref[...]), axis=-1, keepdims=True) * jnp.max(
        jnp.abs(kg_ref[...]), axis=-1, keepdims=True)
    big = 2.1 * bound + MASK_MARGIN
    blk = lax.broadcasted_iota(jnp.int32, (n_blocks, bs), 0)
    past = blk < i
    key = lax.broadcasted_iota(jnp.int32, (bs, bs), 0)
    qry = lax.broadcasted_iota(jnp.int32, (bs, bs), 1)
    causal = key <= qry
    ones_rows = jnp.ones((ONES_ROWS, bs), BF16)
    own = pl.multiple_of(i * bs, bs)

    def produce(h, buf, s):
        s_ref, mb_ref = buf
        s_ref[h] = s
        mb_ref[h] = jnp.max(s, axis=0, keepdims=True)

    def consume(h, buf, v_blk):
        s_ref, mb_ref = buf
        cols = slice(h * hd, (h + 1) * hd)
        m_old = m_ref[h]
        m_new = jnp.maximum(m_old, mb_ref[h])
        p = jnp.exp2(s_ref[h] - m_new)
        vx = jnp.concatenate([v_ref[0, v_blk, cols, :], ones_rows], axis=0)
        acc_ref[h] = jnp.exp2(m_old - m_new) * acc_ref[h] + _dot(vx, p.astype(BF16))
        m_ref[h] = m_new

    for h in range(N_HEADS):
        cols = slice(h * hd, (h + 1) * hd)
        q = q_ref[0, :, cols]
        km_hi, km_lo = _split_bf16(km_ref[0, :, cols])
        stacked = jnp.concatenate([km_hi, km_lo, k_ref[0, pl.ds(own, bs), cols]], axis=0)
        prod = _dot_nt(stacked, q)
        gate_t = prod[:n_blocks] + prod[n_blocks:2 * n_blocks]
        cur = jnp.where(past, gate_t, NEG_INF)
        sel = jnp.zeros((n_blocks, bs), F32)
        for _ in range(min(MOBA_TOPK, n_blocks)):
            top = jnp.max(cur, axis=0, keepdims=True)
            idx = jnp.min(jnp.where(cur == top, blk, n_blocks), axis=0, keepdims=True)
            pick = blk == idx
            sel = jnp.where(pick & past, 1.0, sel)
            cur = jnp.where(pick, NEG_INF, cur)
        bias_t = jnp.where(sel > 0.5, 0.0, -big)
        bias_t = jnp.concatenate([bias_t, jnp.zeros((hd - n_blocks, bs), F32)], axis=0)
        qx_ref[h] = jnp.concatenate([q, jnp.transpose(bias_t).astype(BF16)], axis=1)

        produce(h, (s0_ref, mb0_ref), jnp.where(causal, prod[2 * n_blocks:], NEG_INF))
        acc_ref[h] = jnp.zeros(acc_ref.shape[1:], F32)
        m_ref[h] = jnp.full(m_ref.shape[1:], M_INIT, F32)

    def step(j, buf_in, buf_out):
        start = pl.multiple_of(j * bs, bs)
        onehot = (lax.broadcasted_iota(jnp.int32, (bs, hd), 1) == j).astype(BF16)
        v_blk = jnp.where(j == 0, i, j - 1)
        for h in range(N_HEADS):
            cols = slice(h * hd, (h + 1) * hd)
            kx = jnp.concatenate([k_ref[0, pl.ds(start, bs), cols], onehot], axis=1)
            produce(h, buf_out, _dot_nt(kx, qx_ref[h]))
            consume(h, buf_in, v_blk)

    bufs = ((s0_ref, mb0_ref), (s1_ref, mb1_ref))

    def body(j, carry):
        @pl.when(j % 2 == 0)
        def _():
            step(j, bufs[0], bufs[1])

        @pl.when(j % 2 == 1)
        def _():
            step(j, bufs[1], bufs[0])

        return carry

    lax.fori_loop(0, i, body, 0)

    def finish(buf):
        v_blk = jnp.where(i == 0, i, i - 1)
        for h in range(N_HEADS):
            consume(h, buf, v_blk)
            a = acc_ref[h]
            o_t = a[:hd, :] / a[hd:hd + 1, :]
            o_ref[0, :, h * hd:(h + 1) * hd] = jnp.transpose(o_t).astype(BF16)

    @pl.when(i % 2 == 0)
    def _():
        finish(bufs[0])

    @pl.when(i % 2 == 1)
    def _():
        finish(bufs[1])


def _attn(q, k, vt, km, qg, kg):
    batch, seq, d = q.shape
    nb = seq // MOBA_BLOCK
    tile = pl.BlockSpec((1, MOBA_BLOCK, d), lambda b, i: (b, i, 0))
    gain = pl.BlockSpec((1, HEAD_DIM), lambda b, i: (0, 0))
    return pl.pallas_call(
        functools.partial(_attn_kernel, n_blocks=nb),
        grid=(batch, nb),
        in_specs=[tile,
                  pl.BlockSpec((1, seq, d), lambda b, i: (b, 0, 0)),
                  pl.BlockSpec((1, nb, d, MOBA_BLOCK), lambda b, i: (b, 0, 0, 0)),
                  pl.BlockSpec((1, nb, d), lambda b, i: (b, 0, 0)), gain, gain],
        out_specs=tile,
        out_shape=jax.ShapeDtypeStruct((batch, seq, d), BF16),
        scratch_shapes=[
            pltpu.VMEM((N_HEADS, MOBA_BLOCK, 2 * HEAD_DIM), BF16),
            pltpu.VMEM((N_HEADS, HEAD_DIM + ONES_ROWS, MOBA_BLOCK), F32),
            pltpu.VMEM((N_HEADS, 1, MOBA_BLOCK), F32),
            pltpu.VMEM((N_HEADS, MOBA_BLOCK, MOBA_BLOCK), F32),
            pltpu.VMEM((N_HEADS, MOBA_BLOCK, MOBA_BLOCK), F32),
            pltpu.VMEM((N_HEADS, 1, MOBA_BLOCK), F32),
            pltpu.VMEM((N_HEADS, 1, MOBA_BLOCK), F32),
        ],
        compiler_params=pltpu.CompilerParams(dimension_semantics=("parallel", "arbitrary"),
                                             vmem_limit_bytes=VMEM_LIMIT),
        name="moba_attn",
    )(q, k, vt, km, qg, kg)


def _wo_router_kernel(o_ref, res_ref, wo_ref, g_ref, wrt_ref, h_ref, hn_ref, route_ref, counts_ref,
                      cnt_ref, buf0_ref, buf1_ref, *, tm, n_tiles):
    def produce(buf_ref):
        h = res_ref[...] + _dot(o_ref[...], wo_ref[...])
        h_ref[...] = h
        buf_ref[...] = h

    def consume(buf_ref):
        hn = _rms(buf_ref[...], g_ref[...])
        _store_rows_as_tiles(hn_ref, hn)

        x_hi, x_lo = _split_bf16(hn)
        w_hi, w_lo = _split_bf16(wrt_ref[...])
        part = _dot_nt(jnp.concatenate([w_hi, w_lo], axis=0), x_hi)
        logits = part[:N_EXPERTS] + part[N_EXPERTS:] + _dot_nt(w_hi, x_lo)
        e_id = lax.broadcasted_iota(jnp.int32, (N_EXPERTS, tm), 0)
        m1 = jnp.max(logits, axis=0, keepdims=True)
        i1 = jnp.min(jnp.where(logits == m1, e_id, N_EXPERTS), axis=0, keepdims=True)
        pick1 = e_id == i1
        rest = jnp.where(pick1, NEG_INF, logits)
        m2 = jnp.max(rest, axis=0, keepdims=True)
        i2 = jnp.min(jnp.where(rest == m2, e_id, N_EXPERTS), axis=0, keepdims=True)
        pick2 = e_id == i2
        e2 = jnp.exp(m2 - m1)
        denom = 1.0 + e2

        chosen = jnp.where(pick1 | pick2, 1.0, 0.0)
        tr = lax.broadcasted_iota(jnp.int32, (tm, tm), 0)
        tc = lax.broadcasted_iota(jnp.int32, (tm, tm), 1)
        earlier = jnp.where(tr < tc, 1.0, 0.0).astype(BF16)
        count = cnt_ref[...]
        rank = _dot(chosen.astype(BF16), earlier) + count
        count = count + jnp.sum(chosen, axis=1, keepdims=True)
        cnt_ref[...] = count
        counts_ref[...] = count
        rank1 = jnp.sum(jnp.where(pick1, rank, 0.0), axis=0, keepdims=True)
        rank2 = jnp.sum(jnp.where(pick2, rank, 0.0), axis=0, keepdims=True)
        fields = [i1.astype(F32), i2.astype(F32), 1.0 / denom, e2 / denom, rank1, rank2]
        route = jnp.zeros((N_EXPERTS, tm), F32)
        for slot, val in enumerate(fields):
            route = jnp.where(e_id == slot, val, route)
        route_ref[...] = route

    @pl.when(pl.program_id(0) == 0)
    def _():
        cnt_ref[...] = jnp.zeros_like(cnt_ref)

    _skewed(n_tiles, produce, consume, (buf0_ref, buf1_ref))


def _wo_router(o, res, w_o, g, w_r_t, *, tm=512):
    t, d = res.shape
    n_tiles = t // tm
    const = lambda *shape: pl.BlockSpec(shape, lambda i: (0,) * len(shape))
    new = pl.BlockSpec((tm, d), lambda i: (_produced(i, n_tiles), 0))
    return pl.pallas_call(
        functools.partial(_wo_router_kernel, tm=tm, n_tiles=n_tiles),
        grid=(n_tiles + 1,),
        in_specs=[new, new, const(d, d), const(1, d), const(N_EXPERTS, d)],
        out_specs=[new,
                   pl.BlockSpec((tm * ROW_TILE, V7X_LANES), lambda i: (_consumed(i), 0)),
                   pl.BlockSpec((N_EXPERTS, tm), lambda i: (0, _consumed(i))),
                   const(N_EXPERTS, 1)],
        out_shape=[jax.ShapeDtypeStruct((t, d), F32),
                   jax.ShapeDtypeStruct((t * ROW_TILE, V7X_LANES), F32),
                   jax.ShapeDtypeStruct((N_EXPERTS, t), F32),
                   jax.ShapeDtypeStruct((N_EXPERTS, 1), F32)],
        scratch_shapes=[pltpu.VMEM((N_EXPERTS, 1), F32), pltpu.VMEM((tm, d), F32),
                        pltpu.VMEM((tm, d), F32)],
        compiler_params=pltpu.CompilerParams(dimension_semantics=("arbitrary",),
                                             vmem_limit_bytes=VMEM_LIMIT),
        name="wo_router",
    )(o, res, w_o, g, w_r_t)


def _store_rows_as_tiles(dst_ref, x):
    n = x.shape[0]
    for s in range(ROW_TILE):
        dst_ref[pl.ds(s, n, stride=ROW_TILE), :] = x[:, s * V7X_LANES:(s + 1) * V7X_LANES]


def _load_lane_tile(src_ref, n, s):
    return src_ref[pl.ds(s, n, stride=ROW_TILE), :]


def _row_copy(src_ref, src_row, dst_ref, dst_row, sem):
    src = src_ref.at[pl.ds(pl.multiple_of(src_row * ROW_TILE, ROW_TILE), ROW_TILE), :]
    dst = dst_ref.at[pl.ds(pl.multiple_of(dst_row * ROW_TILE, ROW_TILE), ROW_TILE), :]
    return pltpu.make_async_copy(src, dst, sem)


def _moe_scatter_kernel(meta_ref, pos_ref, x_ref, xs_ref, zero_ref, sem, zero_sem, *, tm, tg):
    zr = zero_ref.shape[0] // ROW_TILE
    n_rows = xs_ref.shape[0] // ROW_TILE
    total = meta_ref[N_EXPERTS - 1]

    def zero_tile(base):
        base = pl.multiple_of(base * ROW_TILE, tg * ROW_TILE)
        return [pltpu.make_async_copy(
            zero_ref, xs_ref.at[pl.ds(base + c * zr * ROW_TILE, zr * ROW_TILE), :], zero_sem)
            for c in range(tg // zr)]

    @pl.when(pl.program_id(0) == 0)
    def _():
        zero_ref[...] = jnp.zeros_like(zero_ref)
        for wait in (False, True):
            for e in range(N_EXPERTS):
                @pl.when(meta_ref[N_EXPERTS + e] > 0)
                def _():
                    for cp in zero_tile(meta_ref[e] - tg):
                        cp.wait() if wait else cp.start()
            for k in range(N_EXPERTS - 1):
                @pl.when(total + k * tg < n_rows)
                def _():
                    for cp in zero_tile(total + k * tg):
                        cp.wait() if wait else cp.start()

    def start(r, carry):
        _row_copy(x_ref, r, xs_ref, pos_ref[0, 0, r], sem).start(priority=0)
        _row_copy(x_ref, r, xs_ref, pos_ref[0, 0, tm + r], sem).start(priority=1)
        return carry

    lax.fori_loop(0, tm, start, 0, unroll=8)

    def wait(r, carry):
        _row_copy(x_ref, 0, xs_ref, 0, sem).wait()
        _row_copy(x_ref, 0, xs_ref, 0, sem).wait()
        return carry

    lax.fori_loop(0, tm, wait, 0, unroll=8)


def _moe_scatter(meta, pos, x, n_rows, *, tg, tm=256, zero_rows=256):
    t = x.shape[0] // ROW_TILE
    return pl.pallas_call(
        functools.partial(_moe_scatter_kernel, tm=tm, tg=tg),
        grid_spec=pltpu.PrefetchScalarGridSpec(
            num_scalar_prefetch=1,
            grid=(t // tm,),
            in_specs=[pl.BlockSpec((1, 1, 2 * tm), lambda i, meta: (i, 0, 0), memory_space=pltpu.SMEM),
                      pl.BlockSpec((tm * ROW_TILE, V7X_LANES), lambda i, meta: (i, 0))],
            out_specs=pl.BlockSpec(memory_space=pl.ANY),
            scratch_shapes=[pltpu.VMEM((zero_rows * ROW_TILE, V7X_LANES), F32),
                            pltpu.SemaphoreType.DMA(()), pltpu.SemaphoreType.DMA(())]),
        out_shape=jax.ShapeDtypeStruct((n_rows * ROW_TILE, V7X_LANES), F32),
        compiler_params=pltpu.CompilerParams(dimension_semantics=("arbitrary",),
                                             vmem_limit_bytes=VMEM_LIMIT),
        name="moe_scatter",
    )(meta, pos, x)


def _moe_gemm_kernel(te_ref, used_ref, fill_ref, x_ref, wg_ref, wu_ref, wd_ref, y_ref, xb_ref,
                     acc_ref):
    del te_ref
    i = pl.program_id(0)
    f = pl.program_id(1)
    tg, d = acc_ref.shape
    live = i < used_ref[0]
    fill = fill_ref[i]
    first = f == 0
    last = f == pl.num_programs(1) - 1

    @pl.when(jnp.logical_not(live) & first)
    def _():
        y_ref[...] = jnp.zeros_like(y_ref)

    def steps(rows):
        weights = lambda: (wg_ref[0], wu_ref[0], wd_ref[0])

        @pl.when(first)
        def _():
            x = jnp.concatenate(
                [_load_lane_tile(x_ref, rows, s).astype(BF16) for s in range(d // V7X_LANES)], axis=1)
            xb_ref[:rows] = x
            acc_ref[:rows] = _swiglu(x, *weights())

        @pl.when(jnp.logical_not(first | last))
        def _():
            acc_ref[:rows] += _swiglu(xb_ref[:rows], *weights())

        @pl.when(last)
        def _():
            _store_rows_as_tiles(y_ref, acc_ref[:rows] + _swiglu(xb_ref[:rows], *weights()))
            if rows < tg:
                y_ref[rows * ROW_TILE:, :] = jnp.zeros(((tg - rows) * ROW_TILE, V7X_LANES), F32)

    quarter = tg // GEMM_ROW_PARTS
    for part in range(1, GEMM_ROW_PARTS + 1):
        @pl.when(live & (fill > (part - 1) * quarter) & (fill <= part * quarter))
        def _():
            steps(part * quarter)


def _moe_gemm(tile_expert, n_used, tile_fill, xs, w_gate, w_up, w_down, *, tg, tf):
    n_rows = xs.shape[0] // ROW_TILE
    d, f_dim = w_gate.shape[-2:]
    row_block = (tg * ROW_TILE, V7X_LANES)
    x_row = pl.BlockSpec(row_block, lambda i, f, te, used, fill: (jnp.minimum(i, used[0] - 1), 0))
    y_row = pl.BlockSpec(row_block, lambda i, f, te, used, fill: (i, 0))
    w_tile = lambda i, f, used: jnp.where(i < used[0], f, f_dim // tf - 1)
    return pl.pallas_call(
        _moe_gemm_kernel,
        grid_spec=pltpu.PrefetchScalarGridSpec(
            num_scalar_prefetch=3,
            grid=(n_rows // tg, f_dim // tf),
            in_specs=[x_row,
                      pl.BlockSpec((1, d, tf), lambda i, f, te, used, fill: (te[i], 0, w_tile(i, f, used))),
                      pl.BlockSpec((1, d, tf), lambda i, f, te, used, fill: (te[i], 0, w_tile(i, f, used))),
                      pl.BlockSpec((1, tf, d), lambda i, f, te, used, fill: (te[i], w_tile(i, f, used), 0))],
            out_specs=y_row,
            scratch_shapes=[pltpu.VMEM((tg, d), BF16), pltpu.VMEM((tg, d), F32)]),
        out_shape=jax.ShapeDtypeStruct((n_rows * ROW_TILE, V7X_LANES), F32),
        compiler_params=pltpu.CompilerParams(dimension_semantics=("arbitrary", "arbitrary"),
                                             vmem_limit_bytes=VMEM_LIMIT),
        name="moe_gemm",
    )(tile_expert, n_used, tile_fill, xs, w_gate, w_up, w_down)


def _moe_combine_kernel(pos_ref, next_pos_ref, h_ref, prob_ref, ys_ref, o_ref, buf_ref, sem, *, tm):
    step = pl.program_id(0)
    slot = step % 2

    def fetch(table_ref, into):
        def start(r, carry):
            _row_copy(ys_ref, table_ref[0, 0, r], buf_ref.at[into, 0], r, sem.at[into]).start(
                priority=0)
            _row_copy(ys_ref, table_ref[0, 0, tm + r], buf_ref.at[into, 1], r, sem.at[into]).start(
                priority=1)
            return carry

        lax.fori_loop(0, tm, start, 0, unroll=8)

    @pl.when(step == 0)
    def _():
        fetch(pos_ref, 0)

    @pl.when(step + 1 < pl.num_programs(0))
    def _():
        fetch(next_pos_ref, 1 - slot)

    def wait(r, carry):
        _row_copy(ys_ref, 0, buf_ref.at[slot, 0], 0, sem.at[slot]).wait()
        _row_copy(ys_ref, 0, buf_ref.at[slot, 1], 0, sem.at[slot]).wait()
        return carry

    lax.fori_loop(0, tm, wait, 0, unroll=8)
    p1 = prob_ref[:, 0:1]
    p2 = prob_ref[:, 1:2]
    for s in range(o_ref.shape[1] // V7X_LANES):
        cols = slice(s * V7X_LANES, (s + 1) * V7X_LANES)
        o_ref[:, cols] = (h_ref[:, cols] + p1 * _load_lane_tile(buf_ref.at[slot, 0], tm, s)
                          + p2 * _load_lane_tile(buf_ref.at[slot, 1], tm, s))


def _moe_combine(pos, h, probs, ys, *, tm):
    t, d = h.shape
    n_steps = t // tm
    table = lambda index: pl.BlockSpec((1, 1, 2 * tm), index, memory_space=pltpu.SMEM)
    return pl.pallas_call(
        functools.partial(_moe_combine_kernel, tm=tm),
        grid=(n_steps,),
        in_specs=[table(lambda i: (i, 0, 0)),
                  table(lambda i: (jnp.minimum(i + 1, n_steps - 1), 0, 0)),
                  pl.BlockSpec((tm, d), lambda i: (i, 0)),
                  pl.BlockSpec((tm, 2), lambda i: (i, 0)),
                  pl.BlockSpec(memory_space=pl.ANY)],
        out_specs=pl.BlockSpec((tm, d), lambda i: (i, 0)),
        out_shape=jax.ShapeDtypeStruct((t, d), F32),
        scratch_shapes=[pltpu.VMEM((2, 2, tm * ROW_TILE, V7X_LANES), F32),
                        pltpu.SemaphoreType.DMA((2,))],
        compiler_params=pltpu.CompilerParams(dimension_semantics=("arbitrary",),
                                             vmem_limit_bytes=VMEM_LIMIT),
        name="moe_combine",
    )(pos, pos, h, probs, ys)


def _moe(h, hn, route, counts, w_gate, w_up, w_down, *, tg=1024, tf=512, tm_scatter=1024,
         tm_combine=256):
    t, d = h.shape
    n_tiles = (2 * t + N_EXPERTS * (tg - 1)) // tg
    counts = counts[:, 0].astype(jnp.int32)
    padded = (counts + tg - 1) // tg * tg
    e_ids = jnp.arange(N_EXPERTS, dtype=jnp.int32)
    ends = jnp.sum(jnp.where(e_ids[:, None] <= e_ids[None, :], padded[:, None], 0), axis=0)
    starts = ends - padded
    experts = route[0:2].astype(jnp.int32)
    pos = route[4:6].astype(jnp.int32)
    for e in range(N_EXPERTS):
        pos = pos + jnp.where(experts == e, starts[e], 0)
    def per_tile(tm):
        return jnp.transpose(pos.reshape(2, t // tm, tm), (1, 0, 2)).reshape(t // tm, 1, 2 * tm)

    n_used = ends[-1:] // tg
    tile_start = jnp.minimum(jnp.arange(n_tiles, dtype=jnp.int32), n_used - 1) * tg
    tile_expert = jnp.sum((ends[None, :] <= tile_start[:, None]).astype(jnp.int32), axis=1)
    filled_to = jnp.sum(jnp.where(tile_expert[:, None] == e_ids, starts + counts, 0), axis=1)
    tile_fill = jnp.clip(filled_to - tile_start, 0, tg)
    xs = _moe_scatter(jnp.concatenate([ends, padded]), per_tile(tm_scatter), hn, n_tiles * tg,
                      tg=tg, tm=tm_scatter)
    ys = _moe_gemm(tile_expert, n_used, tile_fill, xs, w_gate, w_up, w_down, tg=tg, tf=tf)
    return _moe_combine(per_tile(tm_combine), h, jnp.transpose(route[2:4]), ys, tm=tm_combine)


def _rotary_layout(a):
    half = ROT_DIM // 2
    split = HEAD_DIM // 2 - half
    x1, x2, rest = a[..., :half], a[..., half:ROT_DIM], a[..., ROT_DIM:]
    return jnp.concatenate([x1, rest[..., :split], x2, rest[..., split:]], axis=-1)


def _rope_tables(seq):
    half = ROT_DIM // 2
    inv_freq = ROPE_THETA ** (-jnp.arange(half, dtype=F32) / half)
    ang = jnp.arange(seq, dtype=F32)[:, None] * inv_freq[None, :]
    cos = jnp.cos(ang)
    sin = jnp.sin(ang)
    rest = HEAD_DIM - ROT_DIM
    cos_full = jnp.concatenate([cos, cos, jnp.ones((seq, rest), F32)], axis=-1)
    sin_signed = jnp.concatenate([-sin, sin, jnp.zeros((seq, rest), F32)], axis=-1)
    return _rotary_layout(cos_full), _rotary_layout(sin_signed)


def _qkv_weights(w_qkv):
    d = w_qkv.shape[0]
    hd = N_HEADS * HEAD_DIM
    qk = _rotary_layout(w_qkv[:, :2 * hd].reshape(d, 2 * N_HEADS, HEAD_DIM)).reshape(d, 2 * hd)
    return jnp.concatenate([qk, w_qkv[:, 2 * hd:]], axis=1).astype(BF16)


def kernel(x, e_mix_norm, e_w_in, e_conv_w, e_gmlp_ln_g, e_gmlp_ln_b, e_w_spatial, e_b_spatial, e_w_out, e_ffn_norm, e_w_gate, e_w_up, e_w_down, o_mix_norm, o_w_qkv, o_q_norm, o_k_norm, o_w_o, o_ffn_norm, o_w_router, o_w_gate, o_w_up, o_w_down):
    batch, seq, d = x.shape
    t = batch * seq
    xf = x.reshape(t, d)

    bias_full = jnp.repeat(jnp.transpose(e_b_spatial[0]), GMLP_GROUP_DIM, axis=1)
    h1, hn1 = _mixer0(
        xf, e_mix_norm[0][None, :], e_w_in[0].astype(BF16), e_conv_w[0],
        e_gmlp_ln_g[0][None, :], e_gmlp_ln_b[0][None, :], e_w_spatial[0], bias_full,
        e_w_out[0].astype(BF16), e_ffn_norm[0][None, :], seq=seq)
    h2 = _ffn(hn1, h1, e_w_gate[0].astype(BF16), e_w_up[0].astype(BF16), e_w_down[0].astype(BF16),
              tm=512)

    cos, sin = _rope_tables(seq)
    qg = _rotary_layout(o_q_norm[0][None, :])
    kg = _rotary_layout(o_k_norm[0][None, :])
    q, k, vt, km = _qkv(h2, o_mix_norm[0][None, :], _qkv_weights(o_w_qkv[0]), qg, kg, cos, sin, seq=seq)
    as_seq = lambda a: a.reshape(batch, -1, d)
    o = _attn(as_seq(q), as_seq(k), vt.reshape(batch, seq // MOBA_BLOCK, d, MOBA_BLOCK),
              as_seq(km), qg, kg)
    h3, hn3, route, counts = _wo_router(o.reshape(t, d), h2, o_w_o[0].astype(BF16),
                                        o_ffn_norm[0][None, :], jnp.transpose(o_w_router[0]))
    out = _moe(h3, hn3, route, counts, o_w_gate[0], o_w_up[0], o_w_down[0])
    return out.reshape(batch, seq, d)
```

```python
import functools
import math

import jax
import jax.numpy as jnp
from jax import lax
from jax.experimental import pallas as pl
from jax.experimental.pallas import tpu as pltpu

F32 = jnp.float32
BF16 = jnp.bfloat16

D_MODEL = 1024
CONV_DIM = 512
CONV_KERNEL = 3
GMLP_DIM = 512
GMLP_GROUPS = 8
GMLP_GROUP_DIM = GMLP_DIM // GMLP_GROUPS
GMLP_CHUNK = 128
MIX_IN = 3 * CONV_DIM + 2 * GMLP_DIM
N_HEADS = 8
HEAD_DIM = 128
ROT_DIM = HEAD_DIM // 4
ROPE_THETA = 500000.0
MOBA_BLOCK = 256
MOBA_TOPK = 3
N_EXPERTS = 8
EPS = 1e-6

V7X_LANES = 128
ROW_TILE = 8
GEMM_ROW_PARTS = 4
VMEM_LIMIT = 56 * 1024 * 1024

NEG_INF = float("-inf")
QK_SCALE = HEAD_DIM ** -0.5 * math.log2(math.e)
MASK_MARGIN = 256.0
M_INIT = -1e30
ONES_ROWS = 16


def _rms(x, g):
    return x * lax.rsqrt(jnp.mean(x * x, axis=-1, keepdims=True) + EPS) * g


def _dot(a, b):
    return jnp.dot(a, b, preferred_element_type=F32)


def _dot_nt(a, b):
    return lax.dot_general(a, b, (((1,), (1,)), ((), ())), preferred_element_type=F32)


def _split_bf16(x):
    hi = x.astype(BF16)
    lo = (x - hi.astype(F32)).astype(BF16)
    return hi, lo


def _mixer0_kernel(x_ref, xres_ref, g_ref, win_ref, convw_ref, lng_ref, lnb_ref, ws_ref, bias_ref,
                   wout_ref, g2_ref, h_ref, hn_ref, carry_ref, y_ref, buf0_ref, buf1_ref,
                   *, tm, tiles_per_seq, n_tiles):
    def produce(proj_ref):
        hn = _rms(x_ref[...], g_ref[...]).astype(BF16)
        proj_ref[...] = _dot(hn, win_ref[...])

    def consume(proj_ref):
        tile = pl.program_id(0) - 1
        a_h = proj_ref[:, 0:CONV_DIM]
        a_c = proj_ref[:, CONV_DIM:2 * CONV_DIM]
        a_b = proj_ref[:, 2 * CONV_DIM:3 * CONV_DIM]
        g_v = proj_ref[:, 3 * CONV_DIM + GMLP_DIM:]

        z = a_c * a_h
        prev = jnp.where(tile % tiles_per_seq == 0, 0.0, carry_ref[...])
        row = lax.broadcasted_iota(jnp.int32, (tm, CONV_DIM), 0)
        z1 = jnp.where(row == 0, prev[7:8, :], pltpu.roll(z, 1, 0))
        z2 = jnp.where(row == 0, prev[6:7, :],
                       jnp.where(row == 1, prev[7:8, :], pltpu.roll(z, 2, 0)))
        carry_ref[...] = z[tm - 8:tm, :]
        cw = convw_ref[...]
        y_a = a_b * (cw[0:1, :] * z2 + cw[1:2, :] * z1 + cw[2:3, :] * z)
        y_ref[:, 0:CONV_DIM] = y_a.astype(BF16)

        mu = jnp.mean(g_v, axis=-1, keepdims=True)
        cen = g_v - mu
        var = jnp.mean(cen * cen, axis=-1, keepdims=True)
        v = cen * lax.rsqrt(var + EPS) * lng_ref[...] + lnb_ref[...]
        tr = lax.broadcasted_iota(jnp.int32, (GMLP_CHUNK, GMLP_CHUNK), 0)
        tc = lax.broadcasted_iota(jnp.int32, (GMLP_CHUNK, GMLP_CHUNK), 1)
        causal = tc <= tr
        lo_half = tc < GMLP_GROUP_DIM
        n_chunks = tm // GMLP_CHUNK
        for j in range(GMLP_DIM // V7X_LANES):
            cols = slice(j * V7X_LANES, (j + 1) * V7X_LANES)
            w_pair = jnp.concatenate(
                [jnp.where(causal, ws_ref[2 * j], 0.0), jnp.where(causal, ws_ref[2 * j + 1], 0.0)],
                axis=1).astype(BF16)
            stacked = []
            for c in range(n_chunks):
                vj = v[c * GMLP_CHUNK:(c + 1) * GMLP_CHUNK, cols]
                stacked.append(jnp.concatenate(
                    [jnp.where(lo_half, vj, 0.0), jnp.where(lo_half, 0.0, vj)], axis=0))
            mixed = _dot(w_pair, jnp.concatenate(stacked, axis=1).astype(BF16))
            for c in range(n_chunks):
                rows = slice(c * GMLP_CHUNK, (c + 1) * GMLP_CHUNK)
                g_u = proj_ref[rows, 3 * CONV_DIM + j * V7X_LANES:3 * CONV_DIM + (j + 1) * V7X_LANES]
                m_c = mixed[:, c * V7X_LANES:(c + 1) * V7X_LANES] + bias_ref[:, cols]
                y_ref[rows, CONV_DIM + j * V7X_LANES:CONV_DIM + (j + 1) * V7X_LANES] = (
                    g_u * m_c).astype(BF16)

        h = xres_ref[...] + _dot(y_ref[...], wout_ref[...])
        h_ref[...] = h
        hn_ref[...] = _rms(h, g2_ref[...]).astype(BF16)

    @pl.when(pl.program_id(0) == 0)
    def _():
        carry_ref[...] = jnp.zeros_like(carry_ref)

    _skewed(n_tiles, produce, consume, (buf0_ref, buf1_ref))


def _mixer0(x, g, w_in, conv_w, ln_g, ln_b, w_s, bias_full, w_out, g2, *, seq, tm=512):
    t, d = x.shape
    n_tiles = t // tm
    const = lambda *shape: pl.BlockSpec(shape, lambda i: (0,) * len(shape))
    done = pl.BlockSpec((tm, d), lambda i: (_consumed(i), 0))
    return pl.pallas_call(
        functools.partial(_mixer0_kernel, tm=tm, tiles_per_seq=seq // tm, n_tiles=n_tiles),
        grid=(n_tiles + 1,),
        in_specs=[
            pl.BlockSpec((tm, d), lambda i: (_produced(i, n_tiles), 0)),
            done,
            const(1, d),
            const(d, MIX_IN),
            const(CONV_KERNEL, CONV_DIM),
            const(1, GMLP_DIM),
            const(1, GMLP_DIM),
            const(GMLP_GROUPS, GMLP_CHUNK, GMLP_CHUNK),
            const(GMLP_CHUNK, GMLP_DIM),
            const(CONV_DIM + GMLP_DIM, d),
            const(1, d),
        ],
        out_specs=[done, done],
        out_shape=[jax.ShapeDtypeStruct((t, d), F32), jax.ShapeDtypeStruct((t, d), BF16)],
        scratch_shapes=[pltpu.VMEM((8, CONV_DIM), F32), pltpu.VMEM((tm, CONV_DIM + GMLP_DIM), BF16),
                        pltpu.VMEM((tm, MIX_IN), F32), pltpu.VMEM((tm, MIX_IN), F32)],
        compiler_params=pltpu.CompilerParams(dimension_semantics=("arbitrary",),
                                             vmem_limit_bytes=VMEM_LIMIT),
        name="mixer0",
    )(x, x, g, w_in, conv_w, ln_g, ln_b, w_s, bias_full, w_out, g2)


def _swiglu(x, wg, wu, wd):
    a = _dot(x, wg.astype(BF16))
    u = _dot(x, wu.astype(BF16))
    hact = (a * jax.nn.sigmoid(a) * u).astype(BF16)
    return _dot(hact, wd.astype(BF16))


def _ffn_kernel(x_ref, res_ref, wg_ref, wu_ref, wd_ref, o_ref):
    o_ref[...] = res_ref[...] + _swiglu(x_ref[...], wg_ref[...], wu_ref[...], wd_ref[...])


def _ffn(xn, res, w_gate, w_up, w_down, *, tm):
    t, d = xn.shape
    f_dim = w_gate.shape[1]
    row = pl.BlockSpec((tm, d), lambda i: (i, 0))
    resident = lambda shape: pl.BlockSpec(shape, lambda i: (0, 0), pipeline_mode=pl.Buffered(1))
    return pl.pallas_call(
        _ffn_kernel,
        grid=(t // tm,),
        in_specs=[row, row, resident((d, f_dim)), resident((d, f_dim)), resident((f_dim, d))],
        out_specs=row,
        out_shape=jax.ShapeDtypeStruct((t, d), F32),
        compiler_params=pltpu.CompilerParams(dimension_semantics=("parallel",),
                                             vmem_limit_bytes=VMEM_LIMIT),
        name="dense_ffn",
    )(xn, res, w_gate, w_up, w_down)


def _skewed(n_tiles, produce, consume, bufs):
    i = pl.program_id(0)
    b0, b1 = bufs

    @pl.when(i == 0)
    def _():
        produce(b0)

    for parity, (dst, src) in enumerate(((b0, b1), (b1, b0))):
        @pl.when((i > 0) & (i < n_tiles) & (i % 2 == parity))
        def _():
            produce(dst)
            consume(src)

    @pl.when(i == n_tiles)
    def _():
        consume(bufs[(n_tiles - 1) % 2])


def _produced(i, n_tiles):
    return jnp.minimum(i, n_tiles - 1)


def _consumed(i):
    return jnp.maximum(i - 1, 0)


def _qkv_kernel(h_ref, g_ref, w_ref, qg_ref, kg_ref, cos_ref, sin_ref,
                q_ref, k_ref, v_ref, km_ref, buf0_ref, buf1_ref, *, tm, n_tiles):
    hd = N_HEADS * HEAD_DIM

    def produce(buf_ref):
        hn = _rms(h_ref[...], g_ref[...]).astype(BF16)
        buf_ref[...] = _dot(hn, w_ref[...])

    def consume(buf_ref):
        cos = cos_ref[...]
        sin = sin_ref[...]

        def rope(x):
            return x * cos + pltpu.roll(x, HEAD_DIM // 2, 1) * sin

        for c in range(tm // MOBA_BLOCK):
            v_ref[c] = jnp.transpose(
                buf_ref[c * MOBA_BLOCK:(c + 1) * MOBA_BLOCK, 2 * hd:]).astype(BF16)
        for h in range(N_HEADS):
            cols = slice(h * HEAD_DIM, (h + 1) * HEAD_DIM)
            q = rope(_rms(buf_ref[:, cols], qg_ref[...]))
            k = rope(_rms(buf_ref[:, hd + h * HEAD_DIM:hd + (h + 1) * HEAD_DIM], kg_ref[...]))
            q_ref[:, cols] = (q * QK_SCALE).astype(BF16)
            k_ref[:, cols] = k.astype(BF16)
            for c in range(tm // MOBA_BLOCK):
                km_ref[c, :, cols] = jnp.mean(
                    k[c * MOBA_BLOCK:(c + 1) * MOBA_BLOCK, :], axis=0, keepdims=True)

    _skewed(n_tiles, produce, consume, (buf0_ref, buf1_ref))


def _qkv(h, g, w, qg, kg, cos, sin, *, seq, tm=512):
    t, d = h.shape
    ns = seq // tm
    n_tiles = t // tm
    hd = N_HEADS * HEAD_DIM
    const = lambda *shape: pl.BlockSpec(shape, lambda i: (0,) * len(shape))
    row_spec = pl.BlockSpec((tm, hd), lambda i: (_consumed(i), 0))
    row_shape = jax.ShapeDtypeStruct((t, hd), BF16)
    table = pl.BlockSpec((tm, HEAD_DIM), lambda i: (_consumed(i) % ns, 0))
    return pl.pallas_call(
        functools.partial(_qkv_kernel, tm=tm, n_tiles=n_tiles),
        grid=(n_tiles + 1,),
        in_specs=[
            pl.BlockSpec((tm, d), lambda i: (_produced(i, n_tiles), 0)),
            const(1, d),
            const(d, 3 * hd),
            const(1, HEAD_DIM),
            const(1, HEAD_DIM),
            table,
            table,
        ],
        out_specs=[row_spec, row_spec,
                   pl.BlockSpec((tm // MOBA_BLOCK, hd, MOBA_BLOCK), lambda i: (_consumed(i), 0, 0)),
                   pl.BlockSpec((tm // MOBA_BLOCK, 1, hd), lambda i: (_consumed(i), 0, 0))],
        out_shape=[row_shape, row_shape,
                   jax.ShapeDtypeStruct((t // MOBA_BLOCK, hd, MOBA_BLOCK), BF16),
                   jax.ShapeDtypeStruct((t // MOBA_BLOCK, 1, hd), F32)],
        scratch_shapes=[pltpu.VMEM((tm, 3 * hd), F32), pltpu.VMEM((tm, 3 * hd), F32)],
        compiler_params=pltpu.CompilerParams(dimension_semantics=("arbitrary",),
                                             vmem_limit_bytes=VMEM_LIMIT),
        name="qkv",
    )(h, g, w, qg, kg, cos, sin)


def _attn_kernel(q_ref, k_ref, v_ref, km_ref, qg_ref, kg_ref, o_ref, qx_ref, acc_ref, m_ref,
                 s0_ref, s1_ref, mb0_ref, mb1_ref, *, n_blocks):
    i = pl.program_id(1)
    bs = MOBA_BLOCK
    hd = HEAD_DIM
    bound = QK_SCALE * hd * jnp.max(jnp.abs(qg_ref[...]), axis=-1, keepdims=True) * jnp.max(
        jnp.abs(kg_ref[...]), axis=-1, keepdims=True)
    big = 2.1 * bound + MASK_MARGIN
    blk = lax.broadcasted_iota(jnp.int32, (n_blocks, bs), 0)
    past = blk < i
    key = lax.broadcasted_iota(jnp.int32, (bs, bs), 0)
    qry = lax.broadcasted_iota(jnp.int32, (bs, bs), 1)
    causal = key <= qry
    ones_rows = jnp.ones((ONES_ROWS, bs), BF16)
    own = pl.multiple_of(i * bs, bs)

    def produce(h, buf, s):
        s_ref, mb_ref = buf
        s_ref[h] = s
        mb_ref[h] = jnp.max(s, axis=0, keepdims=True)

    def consume(h, buf, v_blk):
        s_ref, mb_ref = buf
        cols = slice(h * hd, (h + 1) * hd)
        m_old = m_ref[h]
        m_new = jnp.maximum(m_old, mb_ref[h])
        p = jnp.exp2(s_ref[h] - m_new)
        vx = jnp.concatenate([v_ref[0, v_blk, cols, :], ones_rows], axis=0)
        acc_ref[h] = jnp.exp2(m_old - m_new) * acc_ref[h] + _dot(vx, p.astype(BF16))
        m_ref[h] = m_new

    for h in range(N_HEADS):
        cols = slice(h * hd, (h + 1) * hd)
        q = q_ref[0, :, cols]
        km_hi, km_lo = _split_bf16(km_ref[0, :, cols])
        stacked = jnp.concatenate([km_hi, km_lo, k_ref[0, pl.ds(own, bs), cols]], axis=0)
        prod = _dot_nt(stacked, q)
        gate_t = prod[:n_blocks] + prod[n_blocks:2 * n_blocks]
        cur = jnp.where(past, gate_t, NEG_INF)
        sel = jnp.zeros((n_blocks, bs), F32)
        for _ in range(min(MOBA_TOPK, n_blocks)):
            top = jnp.max(cur, axis=0, keepdims=True)
            idx = jnp.min(jnp.where(cur == top, blk, n_blocks), axis=0, keepdims=True)
            pick = blk == idx
            sel = jnp.where(pick & past, 1.0, sel)
            cur = jnp.where(pick, NEG_INF, cur)
        bias_t = jnp.where(sel > 0.5, 0.0, -big)
        bias_t = jnp.concatenate([bias_t, jnp.zeros((hd - n_blocks, bs), F32)], axis=0)
        qx_ref[h] = jnp.concatenate([q, jnp.transpose(bias_t).astype(BF16)], axis=1)

        produce(h, (s0_ref, mb0_ref), jnp.where(causal, prod[2 * n_blocks:], NEG_INF))
        acc_ref[h] = jnp.zeros(acc_ref.shape[1:], F32)
        m_ref[h] = jnp.full(m_ref.shape[1:], M_INIT, F32)

    def step(j, buf_in, buf_out):
        start = pl.multiple_of(j * bs, bs)
        onehot = (lax.broadcasted_iota(jnp.int32, (bs, hd), 1) == j).astype(BF16)
        v_blk = jnp.where(j == 0, i, j - 1)
        for h in range(N_HEADS):
            cols = slice(h * hd, (h + 1) * hd)
            kx = jnp.concatenate([k_ref[0, pl.ds(start, bs), cols], onehot], axis=1)
            produce(h, buf_out, _dot_nt(kx, qx_ref[h]))
            consume(h, buf_in, v_blk)

    bufs = ((s0_ref, mb0_ref), (s1_ref, mb1_ref))

    def run(first_block, count):
        for n in range(count):
            step(first_block + n, bufs[n % 2], bufs[(n + 1) % 2])

    def quad(trip, carry):
        run(4 * trip, 4)
        return carry

    n_quads = lax.shift_right_logical(i, 2)
    lax.fori_loop(0, n_quads, quad, 0)

    @pl.when((i & 2) != 0)
    def _():
        run(4 * n_quads, 2)

    @pl.when((i & 1) != 0)
    def _():
        run(i - 1, 1)

    def finish(buf):
        v_blk = jnp.where(i == 0, i, i - 1)
        for h in range(N_HEADS):
            consume(h, buf, v_blk)
            a = acc_ref[h]
            o_t = a[:hd, :] / a[hd:hd + 1, :]
            o_ref[0, :, h * hd:(h + 1) * hd] = jnp.transpose(o_t).astype(BF16)

    @pl.when(i % 2 == 0)
    def _():
        finish(bufs[0])

    @pl.when(i % 2 == 1)
    def _():
        finish(bufs[1])


def _attn(q, k, vt, km, qg, kg):
    batch, seq, d = q.shape
    nb = seq // MOBA_BLOCK
    tile = pl.BlockSpec((1, MOBA_BLOCK, d), lambda b, i: (b, i, 0))
    gain = pl.BlockSpec((1, HEAD_DIM), lambda b, i: (0, 0))
    return pl.pallas_call(
        functools.partial(_attn_kernel, n_blocks=nb),
        grid=(batch, nb),
        in_specs=[tile,
                  pl.BlockSpec((1, seq, d), lambda b, i: (b, 0, 0)),
                  pl.BlockSpec((1, nb, d, MOBA_BLOCK), lambda b, i: (b, 0, 0, 0)),
                  pl.BlockSpec((1, nb, d), lambda b, i: (b, 0, 0)), gain, gain],
        out_specs=tile,
        out_shape=jax.ShapeDtypeStruct((batch, seq, d), BF16),
        scratch_shapes=[
            pltpu.VMEM((N_HEADS, MOBA_BLOCK, 2 * HEAD_DIM), BF16),
            pltpu.VMEM((N_HEADS, HEAD_DIM + ONES_ROWS, MOBA_BLOCK), F32),
            pltpu.VMEM((N_HEADS, 1, MOBA_BLOCK), F32),
            pltpu.VMEM((N_HEADS, MOBA_BLOCK, MOBA_BLOCK), F32),
            pltpu.VMEM((N_HEADS, MOBA_BLOCK, MOBA_BLOCK), F32),
            pltpu.VMEM((N_HEADS, 1, MOBA_BLOCK), F32),
            pltpu.VMEM((N_HEADS, 1, MOBA_BLOCK), F32),
        ],
        compiler_params=pltpu.CompilerParams(dimension_semantics=("parallel", "arbitrary"),
                                             vmem_limit_bytes=VMEM_LIMIT),
        name="moba_attn",
    )(q, k, vt, km, qg, kg)


def _wo_router_kernel(o_ref, res_ref, wo_ref, g_ref, wrt_ref, h_ref, hn_ref, route_ref, counts_ref,
                      cnt_ref, buf0_ref, buf1_ref, *, tm, n_tiles):
    def produce(buf_ref):
        h = res_ref[...] + _dot(o_ref[...], wo_ref[...])
        h_ref[...] = h
        buf_ref[...] = h

    def consume(buf_ref):
        hn = _rms(buf_ref[...], g_ref[...])
        _store_rows_as_tiles(hn_ref, hn)

        x_hi, x_lo = _split_bf16(hn)
        w_hi, w_lo = _split_bf16(wrt_ref[...])
        part = _dot_nt(jnp.concatenate([w_hi, w_lo], axis=0), x_hi)
        logits = part[:N_EXPERTS] + part[N_EXPERTS:] + _dot_nt(w_hi, x_lo)
        e_id = lax.broadcasted_iota(jnp.int32, (N_EXPERTS, tm), 0)
        m1 = jnp.max(logits, axis=0, keepdims=True)
        i1 = jnp.min(jnp.where(logits == m1, e_id, N_EXPERTS), axis=0, keepdims=True)
        pick1 = e_id == i1
        rest = jnp.where(pick1, NEG_INF, logits)
        m2 = jnp.max(rest, axis=0, keepdims=True)
        i2 = jnp.min(jnp.where(rest == m2, e_id, N_EXPERTS), axis=0, keepdims=True)
        pick2 = e_id == i2
        e2 = jnp.exp(m2 - m1)
        denom = 1.0 + e2

        chosen = jnp.where(pick1 | pick2, 1.0, 0.0)
        tr = lax.broadcasted_iota(jnp.int32, (tm, tm), 0)
        tc = lax.broadcasted_iota(jnp.int32, (tm, tm), 1)
        earlier = jnp.where(tr < tc, 1.0, 0.0).astype(BF16)
        count = cnt_ref[...]
        rank = _dot(chosen.astype(BF16), earlier) + count
        count = count + jnp.sum(chosen, axis=1, keepdims=True)
        cnt_ref[...] = count
        counts_ref[...] = count
        rank1 = jnp.sum(jnp.where(pick1, rank, 0.0), axis=0, keepdims=True)
        rank2 = jnp.sum(jnp.where(pick2, rank, 0.0), axis=0, keepdims=True)
        fields = [i1.astype(F32), i2.astype(F32), 1.0 / denom, e2 / denom, rank1, rank2]
        route = jnp.zeros((N_EXPERTS, tm), F32)
        for slot, val in enumerate(fields):
            route = jnp.where(e_id == slot, val, route)
        route_ref[...] = route

    @pl.when(pl.program_id(0) == 0)
    def _():
        cnt_ref[...] = jnp.zeros_like(cnt_ref)

    _skewed(n_tiles, produce, consume, (buf0_ref, buf1_ref))


def _wo_router(o, res, w_o, g, w_r_t, *, tm=512):
    t, d = res.shape
    n_tiles = t // tm
    const = lambda *shape: pl.BlockSpec(shape, lambda i: (0,) * len(shape))
    new = pl.BlockSpec((tm, d), lambda i: (_produced(i, n_tiles), 0))
    return pl.pallas_call(
        functools.partial(_wo_router_kernel, tm=tm, n_tiles=n_tiles),
        grid=(n_tiles + 1,),
        in_specs=[new, new, const(d, d), const(1, d), const(N_EXPERTS, d)],
        out_specs=[new,
                   pl.BlockSpec((tm * ROW_TILE, V7X_LANES), lambda i: (_consumed(i), 0)),
                   pl.BlockSpec((N_EXPERTS, tm), lambda i: (0, _consumed(i))),
                   const(N_EXPERTS, 1)],
        out_shape=[jax.ShapeDtypeStruct((t, d), F32),
                   jax.ShapeDtypeStruct((t * ROW_TILE, V7X_LANES), F32),
                   jax.ShapeDtypeStruct((N_EXPERTS, t), F32),
                   jax.ShapeDtypeStruct((N_EXPERTS, 1), F32)],
        scratch_shapes=[pltpu.VMEM((N_EXPERTS, 1), F32), pltpu.VMEM((tm, d), F32),
                        pltpu.VMEM((tm, d), F32)],
        compiler_params=pltpu.CompilerParams(dimension_semantics=("arbitrary",),
                                             vmem_limit_bytes=VMEM_LIMIT),
        name="wo_router",
    )(o, res, w_o, g, w_r_t)


def _store_rows_as_tiles(dst_ref, x):
    n = x.shape[0]
    for s in range(ROW_TILE):
        dst_ref[pl.ds(s, n, stride=ROW_TILE), :] = x[:, s * V7X_LANES:(s + 1) * V7X_LANES]


def _load_lane_tile(src_ref, n, s):
    return src_ref[pl.ds(s, n, stride=ROW_TILE), :]


def _row_copy(src_ref, src_row, dst_ref, dst_row, sem):
    src = src_ref.at[pl.ds(pl.multiple_of(src_row * ROW_TILE, ROW_TILE), ROW_TILE), :]
    dst = dst_ref.at[pl.ds(pl.multiple_of(dst_row * ROW_TILE, ROW_TILE), ROW_TILE), :]
    return pltpu.make_async_copy(src, dst, sem)


def _moe_scatter_kernel(meta_ref, pos_ref, x_ref, xs_ref, zero_ref, sem, zero_sem, *, tm, tg):
    zr = zero_ref.shape[0] // ROW_TILE
    n_rows = xs_ref.shape[0] // ROW_TILE
    total = meta_ref[N_EXPERTS - 1]

    def zero_tile(base):
        base = pl.multiple_of(base * ROW_TILE, tg * ROW_TILE)
        return [pltpu.make_async_copy(
            zero_ref, xs_ref.at[pl.ds(base + c * zr * ROW_TILE, zr * ROW_TILE), :], zero_sem)
            for c in range(tg // zr)]

    @pl.when(pl.program_id(0) == 0)
    def _():
        zero_ref[...] = jnp.zeros_like(zero_ref)
        for wait in (False, True):
            for e in range(N_EXPERTS):
                @pl.when(meta_ref[N_EXPERTS + e] > 0)
                def _():
                    for cp in zero_tile(meta_ref[e] - tg):
                        cp.wait() if wait else cp.start()
            for k in range(N_EXPERTS - 1):
                @pl.when(total + k * tg < n_rows)
                def _():
                    for cp in zero_tile(total + k * tg):
                        cp.wait() if wait else cp.start()

    def start(r, carry):
        _row_copy(x_ref, r, xs_ref, pos_ref[0, 0, r], sem).start(priority=0)
        _row_copy(x_ref, r, xs_ref, pos_ref[0, 0, tm + r], sem).start(priority=1)
        return carry

    lax.fori_loop(0, tm, start, 0, unroll=8)

    def wait(r, carry):
        _row_copy(x_ref, 0, xs_ref, 0, sem).wait()
        _row_copy(x_ref, 0, xs_ref, 0, sem).wait()
        return carry

    lax.fori_loop(0, tm, wait, 0, unroll=8)


def _moe_scatter(meta, pos, x, n_rows, *, tg, tm=256, zero_rows=256):
    t = x.shape[0] // ROW_TILE
    return pl.pallas_call(
        functools.partial(_moe_scatter_kernel, tm=tm, tg=tg),
        grid_spec=pltpu.PrefetchScalarGridSpec(
            num_scalar_prefetch=1,
            grid=(t // tm,),
            in_specs=[pl.BlockSpec((1, 1, 2 * tm), lambda i, meta: (i, 0, 0), memory_space=pltpu.SMEM),
                      pl.BlockSpec((tm * ROW_TILE, V7X_LANES), lambda i, meta: (i, 0))],
            out_specs=pl.BlockSpec(memory_space=pl.ANY),
            scratch_shapes=[pltpu.VMEM((zero_rows * ROW_TILE, V7X_LANES), F32),
                            pltpu.SemaphoreType.DMA(()), pltpu.SemaphoreType.DMA(())]),
        out_shape=jax.ShapeDtypeStruct((n_rows * ROW_TILE, V7X_LANES), F32),
        compiler_params=pltpu.CompilerParams(dimension_semantics=("arbitrary",),
                                             vmem_limit_bytes=VMEM_LIMIT),
        name="moe_scatter",
    )(meta, pos, x)


def _moe_gemm_kernel(te_ref, used_ref, fill_ref, x_ref, wg_ref, wu_ref, wd_ref, y_ref, xb_ref,
                     acc_ref):
    del te_ref
    i = pl.program_id(0)
    f = pl.program_id(1)
    tg, d = acc_ref.shape
    live = i < used_ref[0]
    fill = fill_ref[i]
    first = f == 0
    last = f == pl.num_programs(1) - 1

    @pl.when(jnp.logical_not(live) & first)
    def _():
        y_ref[...] = jnp.zeros_like(y_ref)

    def steps(rows):
        weights = lambda: (wg_ref[0], wu_ref[0], wd_ref[0])

        @pl.when(first)
        def _():
            x = jnp.concatenate(
                [_load_lane_tile(x_ref, rows, s).astype(BF16) for s in range(d // V7X_LANES)], axis=1)
            xb_ref[:rows] = x
            acc_ref[:rows] = _swiglu(x, *weights())

        @pl.when(jnp.logical_not(first | last))
        def _():
            acc_ref[:rows] += _swiglu(xb_ref[:rows], *weights())

        @pl.when(last)
        def _():
            _store_rows_as_tiles(y_ref, acc_ref[:rows] + _swiglu(xb_ref[:rows], *weights()))
            if rows < tg:
                y_ref[rows * ROW_TILE:, :] = jnp.zeros(((tg - rows) * ROW_TILE, V7X_LANES), F32)

    quarter = tg // GEMM_ROW_PARTS
    for part in range(1, GEMM_ROW_PARTS + 1):
        @pl.when(live & (fill > (part - 1) * quarter) & (fill <= part * quarter))
        def _():
            steps(part * quarter)


def _moe_gemm(tile_expert, n_used, tile_fill, xs, w_gate, w_up, w_down, *, tg, tf):
    n_rows = xs.shape[0] // ROW_TILE
    d, f_dim = w_gate.shape[-2:]
    row_block = (tg * ROW_TILE, V7X_LANES)
    x_row = pl.BlockSpec(row_block, lambda i, f, te, used, fill: (jnp.minimum(i, used[0] - 1), 0))
    y_row = pl.BlockSpec(row_block, lambda i, f, te, used, fill: (i, 0))
    w_tile = lambda i, f, used: jnp.where(i < used[0], f, f_dim // tf - 1)
    return pl.pallas_call(
        _moe_gemm_kernel,
        grid_spec=pltpu.PrefetchScalarGridSpec(
            num_scalar_prefetch=3,
            grid=(n_rows // tg, f_dim // tf),
            in_specs=[x_row,
                      pl.BlockSpec((1, d, tf), lambda i, f, te, used, fill: (te[i], 0, w_tile(i, f, used))),
                      pl.BlockSpec((1, d, tf), lambda i, f, te, used, fill: (te[i], 0, w_tile(i, f, used))),
                      pl.BlockSpec((1, tf, d), lambda i, f, te, used, fill: (te[i], w_tile(i, f, used), 0))],
            out_specs=y_row,
            scratch_shapes=[pltpu.VMEM((tg, d), BF16), pltpu.VMEM((tg, d), F32)]),
        out_shape=jax.ShapeDtypeStruct((n_rows * ROW_TILE, V7X_LANES), F32),
        compiler_params=pltpu.CompilerParams(dimension_semantics=("arbitrary", "arbitrary"),
                                             vmem_limit_bytes=VMEM_LIMIT),
        name="moe_gemm",
    )(tile_expert, n_used, tile_fill, xs, w_gate, w_up, w_down)


def _moe_combine_kernel(pos_ref, next_pos_ref, h_ref, prob_ref, ys_ref, o_ref, buf_ref, sem, *, tm):
    step = pl.program_id(0)
    slot = step % 2

    def fetch(table_ref, into):
        def start(r, carry):
            _row_copy(ys_ref, table_ref[0, 0, r], buf_ref.at[into, 0], r, sem.at[into]).start(
                priority=0)
            _row_copy(ys_ref, table_ref[0, 0, tm + r], buf_ref.at[into, 1], r, sem.at[into]).start(
                priority=1)
            return carry

        lax.fori_loop(0, tm, start, 0, unroll=8)

    @pl.when(step == 0)
    def _():
        fetch(pos_ref, 0)

    @pl.when(step + 1 < pl.num_programs(0))
    def _():
        fetch(next_pos_ref, 1 - slot)

    def wait(r, carry):
        _row_copy(ys_ref, 0, buf_ref.at[slot, 0], 0, sem.at[slot]).wait()
        _row_copy(ys_ref, 0, buf_ref.at[slot, 1], 0, sem.at[slot]).wait()
        return carry

    lax.fori_loop(0, tm, wait, 0, unroll=8)
    p1 = prob_ref[:, 0:1]
    p2 = prob_ref[:, 1:2]
    for s in range(o_ref.shape[1] // V7X_LANES):
        cols = slice(s * V7X_LANES, (s + 1) * V7X_LANES)
        o_ref[:, cols] = (h_ref[:, cols] + p1 * _load_lane_tile(buf_ref.at[slot, 0], tm, s)
                          + p2 * _load_lane_tile(buf_ref.at[slot, 1], tm, s))


def _moe_combine(pos, h, probs, ys, *, tm):
    t, d = h.shape
    n_steps = t // tm
    table = lambda index: pl.BlockSpec((1, 1, 2 * tm), index, memory_space=pltpu.SMEM)
    return pl.pallas_call(
        functools.partial(_moe_combine_kernel, tm=tm),
        grid=(n_steps,),
        in_specs=[table(lambda i: (i, 0, 0)),
                  table(lambda i: (jnp.minimum(i + 1, n_steps - 1), 0, 0)),
                  pl.BlockSpec((tm, d), lambda i: (i, 0)),
                  pl.BlockSpec((tm, 2), lambda i: (i, 0)),
                  pl.BlockSpec(memory_space=pl.ANY)],
        out_specs=pl.BlockSpec((tm, d), lambda i: (i, 0)),
        out_shape=jax.ShapeDtypeStruct((t, d), F32),
        scratch_shapes=[pltpu.VMEM((2, 2, tm * ROW_TILE, V7X_LANES), F32),
                        pltpu.SemaphoreType.DMA((2,))],
        compiler_params=pltpu.CompilerParams(dimension_semantics=("arbitrary",),
                                             vmem_limit_bytes=VMEM_LIMIT),
        name="moe_combine",
    )(pos, pos, h, probs, ys)


def _moe(h, hn, route, counts, w_gate, w_up, w_down, *, tg=1024, tf=512, tm_scatter=1024,
         tm_combine=256):
    t, d = h.shape
    n_tiles = (2 * t + N_EXPERTS * (tg - 1)) // tg
    counts = counts[:, 0].astype(jnp.int32)
    padded = (counts + tg - 1) // tg * tg
    e_ids = jnp.arange(N_EXPERTS, dtype=jnp.int32)
    ends = jnp.sum(jnp.where(e_ids[:, None] <= e_ids[None, :], padded[:, None], 0), axis=0)
    starts = ends - padded
    experts = route[0:2].astype(jnp.int32)
    pos = route[4:6].astype(jnp.int32)
    for e in range(N_EXPERTS):
        pos = pos + jnp.where(experts == e, starts[e], 0)
    def per_tile(tm):
        return jnp.transpose(pos.reshape(2, t // tm, tm), (1, 0, 2)).reshape(t // tm, 1, 2 * tm)

    n_used = ends[-1:] // tg
    tile_start = jnp.minimum(jnp.arange(n_tiles, dtype=jnp.int32), n_used - 1) * tg
    tile_expert = jnp.sum((ends[None, :] <= tile_start[:, None]).astype(jnp.int32), axis=1)
    filled_to = jnp.sum(jnp.where(tile_expert[:, None] == e_ids, starts + counts, 0), axis=1)
    tile_fill = jnp.clip(filled_to - tile_start, 0, tg)
    xs = _moe_scatter(jnp.concatenate([ends, padded]), per_tile(tm_scatter), hn, n_tiles * tg,
                      tg=tg, tm=tm_scatter)
    ys = _moe_gemm(tile_expert, n_used, tile_fill, xs, w_gate, w_up, w_down, tg=tg, tf=tf)
    return _moe_combine(per_tile(tm_combine), h, jnp.transpose(route[2:4]), ys, tm=tm_combine)


def _rotary_layout(a):
    half = ROT_DIM // 2
    split = HEAD_DIM // 2 - half
    x1, x2, rest = a[..., :half], a[..., half:ROT_DIM], a[..., ROT_DIM:]
    return jnp.concatenate([x1, rest[..., :split], x2, rest[..., split:]], axis=-1)


def _rope_tables(seq):
    half = ROT_DIM // 2
    inv_freq = ROPE_THETA ** (-jnp.arange(half, dtype=F32) / half)
    ang = jnp.arange(seq, dtype=F32)[:, None] * inv_freq[None, :]
    cos = jnp.cos(ang)
    sin = jnp.sin(ang)
    rest = HEAD_DIM - ROT_DIM
    cos_full = jnp.concatenate([cos, cos, jnp.ones((seq, rest), F32)], axis=-1)
    sin_signed = jnp.concatenate([-sin, sin, jnp.zeros((seq, rest), F32)], axis=-1)
    return _rotary_layout(cos_full), _rotary_layout(sin_signed)


def _qkv_weights(w_qkv):
    d = w_qkv.shape[0]
    hd = N_HEADS * HEAD_DIM
    qk = _rotary_layout(w_qkv[:, :2 * hd].reshape(d, 2 * N_HEADS, HEAD_DIM)).reshape(d, 2 * hd)
    return jnp.concatenate([qk, w_qkv[:, 2 * hd:]], axis=1).astype(BF16)


def kernel(x, e_mix_norm, e_w_in, e_conv_w, e_gmlp_ln_g, e_gmlp_ln_b, e_w_spatial, e_b_spatial, e_w_out, e_ffn_norm, e_w_gate, e_w_up, e_w_down, o_mix_norm, o_w_qkv, o_q_norm, o_k_norm, o_w_o, o_ffn_norm, o_w_router, o_w_gate, o_w_up, o_w_down):
    batch, seq, d = x.shape
    t = batch * seq
    xf = x.reshape(t, d)

    bias_full = jnp.repeat(jnp.transpose(e_b_spatial[0]), GMLP_GROUP_DIM, axis=1)
    h1, hn1 = _mixer0(
        xf, e_mix_norm[0][None, :], e_w_in[0].astype(BF16), e_conv_w[0],
        e_gmlp_ln_g[0][None, :], e_gmlp_ln_b[0][None, :], e_w_spatial[0], bias_full,
        e_w_out[0].astype(BF16), e_ffn_norm[0][None, :], seq=seq)
    h2 = _ffn(hn1, h1, e_w_gate[0].astype(BF16), e_w_up[0].astype(BF16), e_w_down[0].astype(BF16),
              tm=512)

    cos, sin = _rope_tables(seq)
    qg = _rotary_layout(o_q_norm[0][None, :])
    kg = _rotary_layout(o_k_norm[0][None, :])
    q, k, vt, km = _qkv(h2, o_mix_norm[0][None, :], _qkv_weights(o_w_qkv[0]), qg, kg, cos, sin, seq=seq)
    as_seq = lambda a: a.reshape(batch, -1, d)
    o = _attn(as_seq(q), as_seq(k), vt.reshape(batch, seq // MOBA_BLOCK, d, MOBA_BLOCK),
              as_seq(km), qg, kg)
    h3, hn3, route, counts = _wo_router(o.reshape(t, d), h2, o_w_o[0].astype(BF16),
                                        o_ffn_norm[0][None, :], jnp.transpose(o_w_router[0]))
    out = _moe(h3, hn3, route, counts, o_w_gate[0], o_w_up[0], o_w_down[0])
    return out.reshape(batch, seq, d)
```

```python
import functools
import math

import jax
import jax.numpy as jnp
from jax import lax
from jax.experimental import pallas as pl
from jax.experimental.pallas import tpu as pltpu

F32 = jnp.float32
BF16 = jnp.bfloat16

D_MODEL = 1024
CONV_DIM = 512
CONV_KERNEL = 3
GMLP_DIM = 512
GMLP_GROUPS = 8
GMLP_GROUP_DIM = GMLP_DIM // GMLP_GROUPS
GMLP_CHUNK = 128
MIX_IN = 3 * CONV_DIM + 2 * GMLP_DIM
N_HEADS = 8
HEAD_DIM = 128
ROT_DIM = HEAD_DIM // 4
ROPE_THETA = 500000.0
MOBA_BLOCK = 256
MOBA_TOPK = 3
N_EXPERTS = 8
EPS = 1e-6

V7X_LANES = 128
ROW_TILE = 8
GEMM_ROW_PARTS = 4
VMEM_LIMIT = 56 * 1024 * 1024

NEG_INF = float("-inf")
QK_SCALE = HEAD_DIM ** -0.5 * math.log2(math.e)
MASK_MARGIN = 256.0
M_INIT = -1e30
ONES_ROWS = 16


def _rms(x, g):
    return x * lax.rsqrt(jnp.mean(x * x, axis=-1, keepdims=True) + EPS) * g


def _dot(a, b):
    return jnp.dot(a, b, preferred_element_type=F32)


def _dot_nt(a, b):
    return lax.dot_general(a, b, (((1,), (1,)), ((), ())), preferred_element_type=F32)


def _split_bf16(x):
    hi = x.astype(BF16)
    lo = (x - hi.astype(F32)).astype(BF16)
    return hi, lo


def _mixer0_kernel(x_ref, xres_ref, g_ref, win_ref, convw_ref, lng_ref, lnb_ref, ws_ref, bias_ref,
                   wout_ref, g2_ref, h_ref, hn_ref, carry_ref, y_ref, buf0_ref, buf1_ref,
                   *, tm, tiles_per_seq, n_tiles):
    def produce(proj_ref):
        hn = _rms(x_ref[...], g_ref[...]).astype(BF16)
        proj_ref[...] = _dot(hn, win_ref[...])

    def consume(proj_ref):
        tile = pl.program_id(0) - 1
        a_h = proj_ref[:, 0:CONV_DIM]
        a_c = proj_ref[:, CONV_DIM:2 * CONV_DIM]
        a_b = proj_ref[:, 2 * CONV_DIM:3 * CONV_DIM]
        g_v = proj_ref[:, 3 * CONV_DIM + GMLP_DIM:]

        z = a_c * a_h
        prev = jnp.where(tile % tiles_per_seq == 0, 0.0, carry_ref[...])
        row = lax.broadcasted_iota(jnp.int32, (tm, CONV_DIM), 0)
        z1 = jnp.where(row == 0, prev[7:8, :], pltpu.roll(z, 1, 0))
        z2 = jnp.where(row == 0, prev[6:7, :],
                       jnp.where(row == 1, prev[7:8, :], pltpu.roll(z, 2, 0)))
        carry_ref[...] = z[tm - 8:tm, :]
        cw = convw_ref[...]
        y_a = a_b * (cw[0:1, :] * z2 + cw[1:2, :] * z1 + cw[2:3, :] * z)
        y_ref[:, 0:CONV_DIM] = y_a.astype(BF16)

        mu = jnp.mean(g_v, axis=-1, keepdims=True)
        cen = g_v - mu
        var = jnp.mean(cen * cen, axis=-1, keepdims=True)
        v = cen * lax.rsqrt(var + EPS) * lng_ref[...] + lnb_ref[...]
        tr = lax.broadcasted_iota(jnp.int32, (GMLP_CHUNK, GMLP_CHUNK), 0)
        tc = lax.broadcasted_iota(jnp.int32, (GMLP_CHUNK, GMLP_CHUNK), 1)
        causal = tc <= tr
        lo_half = tc < GMLP_GROUP_DIM
        n_chunks = tm // GMLP_CHUNK
        for j in range(GMLP_DIM // V7X_LANES):
            cols = slice(j * V7X_LANES, (j + 1) * V7X_LANES)
            w_pair = jnp.concatenate(
                [jnp.where(causal, ws_ref[2 * j], 0.0), jnp.where(causal, ws_ref[2 * j + 1], 0.0)],
                axis=1).astype(BF16)
            stacked = []
            for c in range(n_chunks):
                vj = v[c * GMLP_CHUNK:(c + 1) * GMLP_CHUNK, cols]
                stacked.append(jnp.concatenate(
                    [jnp.where(lo_half, vj, 0.0), jnp.where(lo_half, 0.0, vj)], axis=0))
            mixed = _dot(w_pair, jnp.concatenate(stacked, axis=1).astype(BF16))
            for c in range(n_chunks):
                rows = slice(c * GMLP_CHUNK, (c + 1) * GMLP_CHUNK)
                g_u = proj_ref[rows, 3 * CONV_DIM + j * V7X_LANES:3 * CONV_DIM + (j + 1) * V7X_LANES]
                m_c = mixed[:, c * V7X_LANES:(c + 1) * V7X_LANES] + bias_ref[:, cols]
                y_ref[rows, CONV_DIM + j * V7X_LANES:CONV_DIM + (j + 1) * V7X_LANES] = (
                    g_u * m_c).astype(BF16)

        h = xres_ref[...] + _dot(y_ref[...], wout_ref[...])
        h_ref[...] = h
        hn_ref[...] = _rms(h, g2_ref[...]).astype(BF16)

    @pl.when(pl.program_id(0) == 0)
    def _():
        carry_ref[...] = jnp.zeros_like(carry_ref)

    _skewed(n_tiles, produce, consume, (buf0_ref, buf1_ref))


def _mixer0(x, g, w_in, conv_w, ln_g, ln_b, w_s, bias_full, w_out, g2, *, seq, tm=512):
    t, d = x.shape
    n_tiles = t // tm
    const = lambda *shape: pl.BlockSpec(shape, lambda i: (0,) * len(shape))
    done = pl.BlockSpec((tm, d), lambda i: (_consumed(i), 0))
    return pl.pallas_call(
        functools.partial(_mixer0_kernel, tm=tm, tiles_per_seq=seq // tm, n_tiles=n_tiles),
        grid=(n_tiles + 1,),
        in_specs=[
            pl.BlockSpec((tm, d), lambda i: (_produced(i, n_tiles), 0)),
            done,
            const(1, d),
            const(d, MIX_IN),
            const(CONV_KERNEL, CONV_DIM),
            const(1, GMLP_DIM),
            const(1, GMLP_DIM),
            const(GMLP_GROUPS, GMLP_CHUNK, GMLP_CHUNK),
            const(GMLP_CHUNK, GMLP_DIM),
            const(CONV_DIM + GMLP_DIM, d),
            const(1, d),
        ],
        out_specs=[done, done],
        out_shape=[jax.ShapeDtypeStruct((t, d), F32), jax.ShapeDtypeStruct((t, d), BF16)],
        scratch_shapes=[pltpu.VMEM((8, CONV_DIM), F32), pltpu.VMEM((tm, CONV_DIM + GMLP_DIM), BF16),
                        pltpu.VMEM((tm, MIX_IN), F32), pltpu.VMEM((tm, MIX_IN), F32)],
        compiler_params=pltpu.CompilerParams(dimension_semantics=("arbitrary",),
                                             vmem_limit_bytes=VMEM_LIMIT),
        name="mixer0",
    )(x, x, g, w_in, conv_w, ln_g, ln_b, w_s, bias_full, w_out, g2)


def _swiglu(x, wg, wu, wd):
    a = _dot(x, wg.astype(BF16))
    u = _dot(x, wu.astype(BF16))
    hact = (a * jax.nn.sigmoid(a) * u).astype(BF16)
    return _dot(hact, wd.astype(BF16))


def _ffn_kernel(x_ref, res_ref, wg_ref, wu_ref, wd_ref, o_ref):
    o_ref[...] = res_ref[...] + _swiglu(x_ref[...], wg_ref[...], wu_ref[...], wd_ref[...])


def _ffn(xn, res, w_gate, w_up, w_down, *, tm):
    t, d = xn.shape
    f_dim = w_gate.shape[1]
    row = pl.BlockSpec((tm, d), lambda i: (i, 0))
    resident = lambda shape: pl.BlockSpec(shape, lambda i: (0, 0), pipeline_mode=pl.Buffered(1))
    return pl.pallas_call(
        _ffn_kernel,
        grid=(t // tm,),
        in_specs=[row, row, resident((d, f_dim)), resident((d, f_dim)), resident((f_dim, d))],
        out_specs=row,
        out_shape=jax.ShapeDtypeStruct((t, d), F32),
        compiler_params=pltpu.CompilerParams(dimension_semantics=("parallel",),
                                             vmem_limit_bytes=VMEM_LIMIT),
        name="dense_ffn",
    )(xn, res, w_gate, w_up, w_down)


def _skewed(n_tiles, produce, consume, bufs):
    i = pl.program_id(0)
    b0, b1 = bufs

    @pl.when(i == 0)
    def _():
        produce(b0)

    for parity, (dst, src) in enumerate(((b0, b1), (b1, b0))):
        @pl.when((i > 0) & (i < n_tiles) & (i % 2 == parity))
        def _():
            produce(dst)
            consume(src)

    @pl.when(i == n_tiles)
    def _():
        consume(bufs[(n_tiles - 1) % 2])


def _produced(i, n_tiles):
    return jnp.minimum(i, n_tiles - 1)


def _consumed(i):
    return jnp.maximum(i - 1, 0)


def _qkv_kernel(h_ref, g_ref, w_ref, qg_ref, kg_ref, cos_ref, sin_ref,
                q_ref, k_ref, v_ref, km_ref, buf0_ref, buf1_ref, *, tm, n_tiles):
    hd = N_HEADS * HEAD_DIM

    def produce(buf_ref):
        hn = _rms(h_ref[...], g_ref[...]).astype(BF16)
        buf_ref[...] = _dot(hn, w_ref[...])

    def consume(buf_ref):
        cos = cos_ref[...]
        sin = sin_ref[...]

        def rope(x):
            return x * cos + pltpu.roll(x, HEAD_DIM // 2, 1) * sin

        for c in range(tm // MOBA_BLOCK):
            v_ref[c] = jnp.transpose(
                buf_ref[c * MOBA_BLOCK:(c + 1) * MOBA_BLOCK, 2 * hd:]).astype(BF16)
        for h in range(N_HEADS):
            cols = slice(h * HEAD_DIM, (h + 1) * HEAD_DIM)
            q = rope(_rms(buf_ref[:, cols], qg_ref[...]))
            k = rope(_rms(buf_ref[:, hd + h * HEAD_DIM:hd + (h + 1) * HEAD_DIM], kg_ref[...]))
            q_ref[:, cols] = (q * QK_SCALE).astype(BF16)
            k_ref[:, cols] = k.astype(BF16)
            for c in range(tm // MOBA_BLOCK):
                km_ref[c, :, cols] = jnp.mean(
                    k[c * MOBA_BLOCK:(c + 1) * MOBA_BLOCK, :], axis=0, keepdims=True)

    _skewed(n_tiles, produce, consume, (buf0_ref, buf1_ref))


def _qkv(h, g, w, qg, kg, cos, sin, *, seq, tm=512):
    t, d = h.shape
    ns = seq // tm
    n_tiles = t // tm
    hd = N_HEADS * HEAD_DIM
    const = lambda *shape: pl.BlockSpec(shape, lambda i: (0,) * len(shape))
    row_spec = pl.BlockSpec((tm, hd), lambda i: (_consumed(i), 0))
    row_shape = jax.ShapeDtypeStruct((t, hd), BF16)
    table = pl.BlockSpec((tm, HEAD_DIM), lambda i: (_consumed(i) % ns, 0))
    return pl.pallas_call(
        functools.partial(_qkv_kernel, tm=tm, n_tiles=n_tiles),
        grid=(n_tiles + 1,),
        in_specs=[
            pl.BlockSpec((tm, d), lambda i: (_produced(i, n_tiles), 0)),
            const(1, d),
            const(d, 3 * hd),
            const(1, HEAD_DIM),
            const(1, HEAD_DIM),
            table,
            table,
        ],
        out_specs=[row_spec, row_spec,
                   pl.BlockSpec((tm // MOBA_BLOCK, hd, MOBA_BLOCK), lambda i: (_consumed(i), 0, 0)),
                   pl.BlockSpec((tm // MOBA_BLOCK, 1, hd), lambda i: (_consumed(i), 0, 0))],
        out_shape=[row_shape, row_shape,
                   jax.ShapeDtypeStruct((t // MOBA_BLOCK, hd, MOBA_BLOCK), BF16),
                   jax.ShapeDtypeStruct((t // MOBA_BLOCK, 1, hd), F32)],
        scratch_shapes=[pltpu.VMEM((tm, 3 * hd), F32), pltpu.VMEM((tm, 3 * hd), F32)],
        compiler_params=pltpu.CompilerParams(dimension_semantics=("arbitrary",),
                                             vmem_limit_bytes=VMEM_LIMIT),
        name="qkv",
    )(h, g, w, qg, kg, cos, sin)


def _attn_kernel(q_ref, k_ref, v_ref, km_ref, qg_ref, kg_ref, o_ref, qx_ref, acc_ref, m_ref,
                 s0_ref, s1_ref, mb0_ref, mb1_ref, *, n_blocks):
    i = pl.program_id(1)
    bs = MOBA_BLOCK
    hd = HEAD_DIM
    bound = QK_SCALE * hd * jnp.max(jnp.abs(qg_ref[...]), axis=-1, keepdims=True) * jnp.max(
        jnp.abs(kg_ref[...]), axis=-1, keepdims=True)
    big = 2.1 * bound + MASK_MARGIN
    blk = lax.broadcasted_iota(jnp.int32, (n_blocks, bs), 0)
    past = blk < i
    key = lax.broadcasted_iota(jnp.int32, (bs, bs), 0)
    qry = lax.broadcasted_iota(jnp.int32, (bs, bs), 1)
    causal = key <= qry
    ones_rows = jnp.ones((ONES_ROWS, bs), BF16)
    own = pl.multiple_of(i * bs, bs)

    def produce(h, buf, s):
        s_ref, mb_ref = buf
        s_ref[h] = s
        mb_ref[h] = jnp.max(s, axis=0, keepdims=True)

    def consume(h, buf, v_blk):
        s_ref, mb_ref = buf
        cols = slice(h * hd, (h + 1) * hd)
        m_old = m_ref[h]
        m_new = jnp.maximum(m_old, mb_ref[h])
        p = jnp.exp2(s_ref[h] - m_new)
        vx = jnp.concatenate([v_ref[0, v_blk, cols, :], ones_rows], axis=0)
        acc_ref[h] = jnp.exp2(m_old - m_new) * acc_ref[h] + _dot(vx, p.astype(BF16))
        m_ref[h] = m_new

    for h in range(N_HEADS):
        cols = slice(h * hd, (h + 1) * hd)
        q = q_ref[0, :, cols]
        km_hi, km_lo = _split_bf16(km_ref[0, :, cols])
        stacked = jnp.concatenate([km_hi, km_lo, k_ref[0, pl.ds(own, bs), cols]], axis=0)
        prod = _dot_nt(stacked, q)
        gate_t = prod[:n_blocks] + prod[n_blocks:2 * n_blocks]
        cur = jnp.where(past, gate_t, NEG_INF)
        sel = jnp.zeros((n_blocks, bs), F32)
        for _ in range(min(MOBA_TOPK, n_blocks)):
            top = jnp.max(cur, axis=0, keepdims=True)
            idx = jnp.min(jnp.where(cur == top, blk, n_blocks), axis=0, keepdims=True)
            pick = blk == idx
            sel = jnp.where(pick & past, 1.0, sel)
            cur = jnp.where(pick, NEG_INF, cur)
        bias_t = jnp.where(sel > 0.5, 0.0, -big)
        bias_t = jnp.concatenate([bias_t, jnp.zeros((hd - n_blocks, bs), F32)], axis=0)
        qx_ref[h] = jnp.concatenate([q, jnp.transpose(bias_t).astype(BF16)], axis=1)

        produce(h, (s0_ref, mb0_ref), jnp.where(causal, prod[2 * n_blocks:], NEG_INF))
        acc_ref[h] = jnp.zeros(acc_ref.shape[1:], F32)
        m_ref[h] = jnp.full(m_ref.shape[1:], M_INIT, F32)

    def step(j, buf_in, buf_out):
        start = pl.multiple_of(j * bs, bs)
        onehot = (lax.broadcasted_iota(jnp.int32, (bs, hd), 1) == j).astype(BF16)
        v_blk = jnp.where(j == 0, i, j - 1)
        for h in range(N_HEADS):
            cols = slice(h * hd, (h + 1) * hd)
            kx = jnp.concatenate([k_ref[0, pl.ds(start, bs), cols], onehot], axis=1)
            produce(h, buf_out, _dot_nt(kx, qx_ref[h]))
            consume(h, buf_in, v_blk)

    bufs = ((s0_ref, mb0_ref), (s1_ref, mb1_ref))

    def run(first_block, count):
        for n in range(count):
            step(first_block + n, bufs[n % 2], bufs[(n + 1) % 2])

    def group_of_eight(trip, carry):
        run(8 * trip, 8)
        return carry

    lax.fori_loop(0, lax.shift_right_logical(i, 3), group_of_eight, 0)

    for size in (4, 2, 1):
        @pl.when((i & size) != 0)
        def _():
            run(i & ~(2 * size - 1), size)

    def finish(buf):
        v_blk = jnp.where(i == 0, i, i - 1)
        for h in range(N_HEADS):
            consume(h, buf, v_blk)
            a = acc_ref[h]
            o_t = a[:hd, :] / a[hd:hd + 1, :]
            o_ref[0, :, h * hd:(h + 1) * hd] = jnp.transpose(o_t).astype(BF16)

    @pl.when(i % 2 == 0)
    def _():
        finish(bufs[0])

    @pl.when(i % 2 == 1)
    def _():
        finish(bufs[1])


def _attn(q, k, vt, km, qg, kg):
    batch, seq, d = q.shape
    nb = seq // MOBA_BLOCK
    tile = pl.BlockSpec((1, MOBA_BLOCK, d), lambda b, i: (b, i, 0))
    gain = pl.BlockSpec((1, HEAD_DIM), lambda b, i: (0, 0))
    return pl.pallas_call(
        functools.partial(_attn_kernel, n_blocks=nb),
        grid=(batch, nb),
        in_specs=[tile,
                  pl.BlockSpec((1, seq, d), lambda b, i: (b, 0, 0)),
                  pl.BlockSpec((1, nb, d, MOBA_BLOCK), lambda b, i: (b, 0, 0, 0)),
                  pl.BlockSpec((1, nb, d), lambda b, i: (b, 0, 0)), gain, gain],
        out_specs=tile,
        out_shape=jax.ShapeDtypeStruct((batch, seq, d), BF16),
        scratch_shapes=[
            pltpu.VMEM((N_HEADS, MOBA_BLOCK, 2 * HEAD_DIM), BF16),
            pltpu.VMEM((N_HEADS, HEAD_DIM + ONES_ROWS, MOBA_BLOCK), F32),
            pltpu.VMEM((N_HEADS, 1, MOBA_BLOCK), F32),
            pltpu.VMEM((N_HEADS, MOBA_BLOCK, MOBA_BLOCK), F32),
            pltpu.VMEM((N_HEADS, MOBA_BLOCK, MOBA_BLOCK), F32),
            pltpu.VMEM((N_HEADS, 1, MOBA_BLOCK), F32),
            pltpu.VMEM((N_HEADS, 1, MOBA_BLOCK), F32),
        ],
        compiler_params=pltpu.CompilerParams(dimension_semantics=("parallel", "arbitrary"),
                                             vmem_limit_bytes=VMEM_LIMIT),
        name="moba_attn",
    )(q, k, vt, km, qg, kg)


def _wo_router_kernel(o_ref, res_ref, wo_ref, g_ref, wrt_ref, h_ref, hn_ref, route_ref, counts_ref,
                      cnt_ref, buf0_ref, buf1_ref, *, tm, n_tiles):
    def produce(buf_ref):
        h = res_ref[...] + _dot(o_ref[...], wo_ref[...])
        h_ref[...] = h
        buf_ref[...] = h

    def consume(buf_ref):
        hn = _rms(buf_ref[...], g_ref[...])
        _store_rows_as_tiles(hn_ref, hn)

        x_hi, x_lo = _split_bf16(hn)
        w_hi, w_lo = _split_bf16(wrt_ref[...])
        part = _dot_nt(jnp.concatenate([w_hi, w_lo], axis=0), x_hi)
        logits = part[:N_EXPERTS] + part[N_EXPERTS:] + _dot_nt(w_hi, x_lo)
        e_id = lax.broadcasted_iota(jnp.int32, (N_EXPERTS, tm), 0)
        m1 = jnp.max(logits, axis=0, keepdims=True)
        i1 = jnp.min(jnp.where(logits == m1, e_id, N_EXPERTS), axis=0, keepdims=True)
        pick1 = e_id == i1
        rest = jnp.where(pick1, NEG_INF, logits)
        m2 = jnp.max(rest, axis=0, keepdims=True)
        i2 = jnp.min(jnp.where(rest == m2, e_id, N_EXPERTS), axis=0, keepdims=True)
        pick2 = e_id == i2
        e2 = jnp.exp(m2 - m1)
        denom = 1.0 + e2

        chosen = jnp.where(pick1 | pick2, 1.0, 0.0)
        tr = lax.broadcasted_iota(jnp.int32, (tm, tm), 0)
        tc = lax.broadcasted_iota(jnp.int32, (tm, tm), 1)
        earlier = jnp.where(tr < tc, 1.0, 0.0).astype(BF16)
        count = cnt_ref[...]
        rank = _dot(chosen.astype(BF16), earlier) + count
        count = count + jnp.sum(chosen, axis=1, keepdims=True)
        cnt_ref[...] = count
        counts_ref[...] = count
        rank1 = jnp.sum(jnp.where(pick1, rank, 0.0), axis=0, keepdims=True)
        rank2 = jnp.sum(jnp.where(pick2, rank, 0.0), axis=0, keepdims=True)
        fields = [i1.astype(F32), i2.astype(F32), 1.0 / denom, e2 / denom, rank1, rank2]
        route = jnp.zeros((N_EXPERTS, tm), F32)
        for slot, val in enumerate(fields):
            route = jnp.where(e_id == slot, val, route)
        route_ref[...] = route

    @pl.when(pl.program_id(0) == 0)
    def _():
        cnt_ref[...] = jnp.zeros_like(cnt_ref)

    _skewed(n_tiles, produce, consume, (buf0_ref, buf1_ref))


def _wo_router(o, res, w_o, g, w_r_t, *, tm=512):
    t, d = res.shape
    n_tiles = t // tm
    const = lambda *shape: pl.BlockSpec(shape, lambda i: (0,) * len(shape))
    new = pl.BlockSpec((tm, d), lambda i: (_produced(i, n_tiles), 0))
    return pl.pallas_call(
        functools.partial(_wo_router_kernel, tm=tm, n_tiles=n_tiles),
        grid=(n_tiles + 1,),
        in_specs=[new, new, const(d, d), const(1, d), const(N_EXPERTS, d)],
        out_specs=[new,
                   pl.BlockSpec((tm * ROW_TILE, V7X_LANES), lambda i: (_consumed(i), 0)),
                   pl.BlockSpec((N_EXPERTS, tm), lambda i: (0, _consumed(i))),
                   const(N_EXPERTS, 1)],
        out_shape=[jax.ShapeDtypeStruct((t, d), F32),
                   jax.ShapeDtypeStruct((t * ROW_TILE, V7X_LANES), F32),
                   jax.ShapeDtypeStruct((N_EXPERTS, t), F32),
                   jax.ShapeDtypeStruct((N_EXPERTS, 1), F32)],
        scratch_shapes=[pltpu.VMEM((N_EXPERTS, 1), F32), pltpu.VMEM((tm, d), F32),
                        pltpu.VMEM((tm, d), F32)],
        compiler_params=pltpu.CompilerParams(dimension_semantics=("arbitrary",),
                                             vmem_limit_bytes=VMEM_LIMIT),
        name="wo_router",
    )(o, res, w_o, g, w_r_t)


def _store_rows_as_tiles(dst_ref, x):
    n = x.shape[0]
    for s in range(ROW_TILE):
        dst_ref[pl.ds(s, n, stride=ROW_TILE), :] = x[:, s * V7X_LANES:(s + 1) * V7X_LANES]


def _load_lane_tile(src_ref, n, s):
    return src_ref[pl.ds(s, n, stride=ROW_TILE), :]


def _row_copy(src_ref, src_row, dst_ref, dst_row, sem):
    src = src_ref.at[pl.ds(pl.multiple_of(src_row * ROW_TILE, ROW_TILE), ROW_TILE), :]
    dst = dst_ref.at[pl.ds(pl.multiple_of(dst_row * ROW_TILE, ROW_TILE), ROW_TILE), :]
    return pltpu.make_async_copy(src, dst, sem)


def _moe_scatter_kernel(meta_ref, pos_ref, x_ref, xs_ref, zero_ref, sem, zero_sem, *, tm, tg):
    zr = zero_ref.shape[0] // ROW_TILE
    n_rows = xs_ref.shape[0] // ROW_TILE
    total = meta_ref[N_EXPERTS - 1]

    def zero_tile(base):
        base = pl.multiple_of(base * ROW_TILE, tg * ROW_TILE)
        return [pltpu.make_async_copy(
            zero_ref, xs_ref.at[pl.ds(base + c * zr * ROW_TILE, zr * ROW_TILE), :], zero_sem)
            for c in range(tg // zr)]

    @pl.when(pl.program_id(0) == 0)
    def _():
        zero_ref[...] = jnp.zeros_like(zero_ref)
        for wait in (False, True):
            for e in range(N_EXPERTS):
                @pl.when(meta_ref[N_EXPERTS + e] > 0)
                def _():
                    for cp in zero_tile(meta_ref[e] - tg):
                        cp.wait() if wait else cp.start()
            for k in range(N_EXPERTS - 1):
                @pl.when(total + k * tg < n_rows)
                def _():
                    for cp in zero_tile(total + k * tg):
                        cp.wait() if wait else cp.start()

    def start(r, carry):
        _row_copy(x_ref, r, xs_ref, pos_ref[0, 0, r], sem).start(priority=0)
        _row_copy(x_ref, r, xs_ref, pos_ref[0, 0, tm + r], sem).start(priority=1)
        return carry

    lax.fori_loop(0, tm, start, 0, unroll=8)

    def wait(r, carry):
        _row_copy(x_ref, 0, xs_ref, 0, sem).wait()
        _row_copy(x_ref, 0, xs_ref, 0, sem).wait()
        return carry

    lax.fori_loop(0, tm, wait, 0, unroll=8)


def _moe_scatter(meta, pos, x, n_rows, *, tg, tm=256, zero_rows=256):
    t = x.shape[0] // ROW_TILE
    return pl.pallas_call(
        functools.partial(_moe_scatter_kernel, tm=tm, tg=tg),
        grid_spec=pltpu.PrefetchScalarGridSpec(
            num_scalar_prefetch=1,
            grid=(t // tm,),
            in_specs=[pl.BlockSpec((1, 1, 2 * tm), lambda i, meta: (i, 0, 0), memory_space=pltpu.SMEM),
                      pl.BlockSpec((tm * ROW_TILE, V7X_LANES), lambda i, meta: (i, 0))],
            out_specs=pl.BlockSpec(memory_space=pl.ANY),
            scratch_shapes=[pltpu.VMEM((zero_rows * ROW_TILE, V7X_LANES), F32),
                            pltpu.SemaphoreType.DMA(()), pltpu.SemaphoreType.DMA(())]),
        out_shape=jax.ShapeDtypeStruct((n_rows * ROW_TILE, V7X_LANES), F32),
        compiler_params=pltpu.CompilerParams(dimension_semantics=("arbitrary",),
                                             vmem_limit_bytes=VMEM_LIMIT),
        name="moe_scatter",
    )(meta, pos, x)


def _moe_gemm_kernel(te_ref, used_ref, fill_ref, x_ref, wg_ref, wu_ref, wd_ref, y_ref, xb_ref,
                     acc_ref):
    del te_ref
    i = pl.program_id(0)
    f = pl.program_id(1)
    tg, d = acc_ref.shape
    live = i < used_ref[0]
    fill = fill_ref[i]
    first = f == 0
    last = f == pl.num_programs(1) - 1

    @pl.when(jnp.logical_not(live) & first)
    def _():
        y_ref[...] = jnp.zeros_like(y_ref)

    def steps(rows):
        weights = lambda: (wg_ref[0], wu_ref[0], wd_ref[0])

        @pl.when(first)
        def _():
            x = jnp.concatenate(
                [_load_lane_tile(x_ref, rows, s).astype(BF16) for s in range(d // V7X_LANES)], axis=1)
            xb_ref[:rows] = x
            acc_ref[:rows] = _swiglu(x, *weights())

        @pl.when(jnp.logical_not(first | last))
        def _():
            acc_ref[:rows] += _swiglu(xb_ref[:rows], *weights())

        @pl.when(last)
        def _():
            _store_rows_as_tiles(y_ref, acc_ref[:rows] + _swiglu(xb_ref[:rows], *weights()))
            if rows < tg:
                y_ref[rows * ROW_TILE:, :] = jnp.zeros(((tg - rows) * ROW_TILE, V7X_LANES), F32)

    quarter = tg // GEMM_ROW_PARTS
    for part in range(1, GEMM_ROW_PARTS + 1):
        @pl.when(live & (fill > (part - 1) * quarter) & (fill <= part * quarter))
        def _():
            steps(part * quarter)


def _moe_gemm(tile_expert, n_used, tile_fill, xs, w_gate, w_up, w_down, *, tg, tf):
    n_rows = xs.shape[0] // ROW_TILE
    d, f_dim = w_gate.shape[-2:]
    row_block = (tg * ROW_TILE, V7X_LANES)
    x_row = pl.BlockSpec(row_block, lambda i, f, te, used, fill: (jnp.minimum(i, used[0] - 1), 0))
    y_row = pl.BlockSpec(row_block, lambda i, f, te, used, fill: (i, 0))
    w_tile = lambda i, f, used: jnp.where(i < used[0], f, f_dim // tf - 1)
    return pl.pallas_call(
        _moe_gemm_kernel,
        grid_spec=pltpu.PrefetchScalarGridSpec(
            num_scalar_prefetch=3,
            grid=(n_rows // tg, f_dim // tf),
            in_specs=[x_row,
                      pl.BlockSpec((1, d, tf), lambda i, f, te, used, fill: (te[i], 0, w_tile(i, f, used))),
                      pl.BlockSpec((1, d, tf), lambda i, f, te, used, fill: (te[i], 0, w_tile(i, f, used))),
                      pl.BlockSpec((1, tf, d), lambda i, f, te, used, fill: (te[i], w_tile(i, f, used), 0))],
            out_specs=y_row,
            scratch_shapes=[pltpu.VMEM((tg, d), BF16), pltpu.VMEM((tg, d), F32)]),
        out_shape=jax.ShapeDtypeStruct((n_rows * ROW_TILE, V7X_LANES), F32),
        compiler_params=pltpu.CompilerParams(dimension_semantics=("arbitrary", "arbitrary"),
                                             vmem_limit_bytes=VMEM_LIMIT),
        name="moe_gemm",
    )(tile_expert, n_used, tile_fill, xs, w_gate, w_up, w_down)


def _moe_combine_kernel(pos_ref, next_pos_ref, h_ref, prob_ref, ys_ref, o_ref, buf_ref, sem, *, tm):
    step = pl.program_id(0)
    slot = step % 2

    def fetch(table_ref, into):
        def start(r, carry):
            _row_copy(ys_ref, table_ref[0, 0, r], buf_ref.at[into, 0], r, sem.at[into]).start(
                priority=0)
            _row_copy(ys_ref, table_ref[0, 0, tm + r], buf_ref.at[into, 1], r, sem.at[into]).start(
                priority=1)
            return carry

        lax.fori_loop(0, tm, start, 0, unroll=8)

    @pl.when(step == 0)
    def _():
        fetch(pos_ref, 0)

    @pl.when(step + 1 < pl.num_programs(0))
    def _():
        fetch(next_pos_ref, 1 - slot)

    def wait(r, carry):
        _row_copy(ys_ref, 0, buf_ref.at[slot, 0], 0, sem.at[slot]).wait()
        _row_copy(ys_ref, 0, buf_ref.at[slot, 1], 0, sem.at[slot]).wait()
        return carry

    lax.fori_loop(0, tm, wait, 0, unroll=8)
    p1 = prob_ref[:, 0:1]
    p2 = prob_ref[:, 1:2]
    for s in range(o_ref.shape[1] // V7X_LANES):
        cols = slice(s * V7X_LANES, (s + 1) * V7X_LANES)
        o_ref[:, cols] = (h_ref[:, cols] + p1 * _load_lane_tile(buf_ref.at[slot, 0], tm, s)
                          + p2 * _load_lane_tile(buf_ref.at[slot, 1], tm, s))


def _moe_combine(pos, h, probs, ys, *, tm):
    t, d = h.shape
    n_steps = t // tm
    table = lambda index: pl.BlockSpec((1, 1, 2 * tm), index, memory_space=pltpu.SMEM)
    return pl.pallas_call(
        functools.partial(_moe_combine_kernel, tm=tm),
        grid=(n_steps,),
        in_specs=[table(lambda i: (i, 0, 0)),
                  table(lambda i: (jnp.minimum(i + 1, n_steps - 1), 0, 0)),
                  pl.BlockSpec((tm, d), lambda i: (i, 0)),
                  pl.BlockSpec((tm, 2), lambda i: (i, 0)),
                  pl.BlockSpec(memory_space=pl.ANY)],
        out_specs=pl.BlockSpec((tm, d), lambda i: (i, 0)),
        out_shape=jax.ShapeDtypeStruct((t, d), F32),
        scratch_shapes=[pltpu.VMEM((2, 2, tm * ROW_TILE, V7X_LANES), F32),
                        pltpu.SemaphoreType.DMA((2,))],
        compiler_params=pltpu.CompilerParams(dimension_semantics=("arbitrary",),
                                             vmem_limit_bytes=VMEM_LIMIT),
        name="moe_combine",
    )(pos, pos, h, probs, ys)


def _moe(h, hn, route, counts, w_gate, w_up, w_down, *, tg=1024, tf=512, tm_scatter=2048,
         tm_combine=256):
    t, d = h.shape
    n_tiles = (2 * t + N_EXPERTS * (tg - 1)) // tg
    counts = counts[:, 0].astype(jnp.int32)
    padded = (counts + tg - 1) // tg * tg
    e_ids = jnp.arange(N_EXPERTS, dtype=jnp.int32)
    ends = jnp.sum(jnp.where(e_ids[:, None] <= e_ids[None, :], padded[:, None], 0), axis=0)
    starts = ends - padded
    experts = route[0:2].astype(jnp.int32)
    pos = route[4:6].astype(jnp.int32)
    for e in range(N_EXPERTS):
        pos = pos + jnp.where(experts == e, starts[e], 0)
    def per_tile(tm):
        return jnp.transpose(pos.reshape(2, t // tm, tm), (1, 0, 2)).reshape(t // tm, 1, 2 * tm)

    n_used = ends[-1:] // tg
    tile_start = jnp.minimum(jnp.arange(n_tiles, dtype=jnp.int32), n_used - 1) * tg
    tile_expert = jnp.sum((ends[None, :] <= tile_start[:, None]).astype(jnp.int32), axis=1)
    filled_to = jnp.sum(jnp.where(tile_expert[:, None] == e_ids, starts + counts, 0), axis=1)
    tile_fill = jnp.clip(filled_to - tile_start, 0, tg)
    xs = _moe_scatter(jnp.concatenate([ends, padded]), per_tile(tm_scatter), hn, n_tiles * tg,
                      tg=tg, tm=tm_scatter)
    ys = _moe_gemm(tile_expert, n_used, tile_fill, xs, w_gate, w_up, w_down, tg=tg, tf=tf)
    return _moe_combine(per_tile(tm_combine), h, jnp.transpose(route[2:4]), ys, tm=tm_combine)


def _rotary_layout(a):
    half = ROT_DIM // 2
    split = HEAD_DIM // 2 - half
    x1, x2, rest = a[..., :half], a[..., half:ROT_DIM], a[..., ROT_DIM:]
    return jnp.concatenate([x1, rest[..., :split], x2, rest[..., split:]], axis=-1)


def _rope_tables(seq):
    half = ROT_DIM // 2
    inv_freq = ROPE_THETA ** (-jnp.arange(half, dtype=F32) / half)
    ang = jnp.arange(seq, dtype=F32)[:, None] * inv_freq[None, :]
    cos = jnp.cos(ang)
    sin = jnp.sin(ang)
    rest = HEAD_DIM - ROT_DIM
    cos_full = jnp.concatenate([cos, cos, jnp.ones((seq, rest), F32)], axis=-1)
    sin_signed = jnp.concatenate([-sin, sin, jnp.zeros((seq, rest), F32)], axis=-1)
    return _rotary_layout(cos_full), _rotary_layout(sin_signed)


def _qkv_weights(w_qkv):
    d = w_qkv.shape[0]
    hd = N_HEADS * HEAD_DIM
    qk = _rotary_layout(w_qkv[:, :2 * hd].reshape(d, 2 * N_HEADS, HEAD_DIM)).reshape(d, 2 * hd)
    return jnp.concatenate([qk, w_qkv[:, 2 * hd:]], axis=1).astype(BF16)


def kernel(x, e_mix_norm, e_w_in, e_conv_w, e_gmlp_ln_g, e_gmlp_ln_b, e_w_spatial, e_b_spatial, e_w_out, e_ffn_norm, e_w_gate, e_w_up, e_w_down, o_mix_norm, o_w_qkv, o_q_norm, o_k_norm, o_w_o, o_ffn_norm, o_w_router, o_w_gate, o_w_up, o_w_down):
    batch, seq, d = x.shape
    t = batch * seq
    xf = x.reshape(t, d)

    bias_full = jnp.repeat(jnp.transpose(e_b_spatial[0]), GMLP_GROUP_DIM, axis=1)
    h1, hn1 = _mixer0(
        xf, e_mix_norm[0][None, :], e_w_in[0].astype(BF16), e_conv_w[0],
        e_gmlp_ln_g[0][None, :], e_gmlp_ln_b[0][None, :], e_w_spatial[0], bias_full,
        e_w_out[0].astype(BF16), e_ffn_norm[0][None, :], seq=seq)
    h2 = _ffn(hn1, h1, e_w_gate[0].astype(BF16), e_w_up[0].astype(BF16), e_w_down[0].astype(BF16),
              tm=512)

    cos, sin = _rope_tables(seq)
    qg = _rotary_layout(o_q_norm[0][None, :])
    kg = _rotary_layout(o_k_norm[0][None, :])
    q, k, vt, km = _qkv(h2, o_mix_norm[0][None, :], _qkv_weights(o_w_qkv[0]), qg, kg, cos, sin, seq=seq)
    as_seq = lambda a: a.reshape(batch, -1, d)
    o = _attn(as_seq(q), as_seq(k), vt.reshape(batch, seq // MOBA_BLOCK, d, MOBA_BLOCK),
              as_seq(km), qg, kg)
    h3, hn3, route, counts = _wo_router(o.reshape(t, d), h2, o_w_o[0].astype(BF16),
                                        o_ffn_norm[0][None, :], jnp.transpose(o_w_router[0]))
    out = _moe(h3, hn3, route, counts, o_w_gate[0], o_w_up[0], o_w_down[0])
    return out.reshape(batch, seq, d)
```
